```python
import jax, jax.numpy as jnp
from jax import lax
import numpy as np

D_MODEL = 1024
BATCH = 8
SEQ = 8192
DEPTH = 2

A_HEADS = 4
A_HEAD_DIM = 128
D_A = A_HEADS * A_HEAD_DIM
CHUNK = 128
B_GROUPS = 4
D_B = D_MODEL - D_A
B_CONV = 3
D_IN_AB = 2 * D_A + 3 * D_B
D_RNN = D_MODEL
LRU_HEADS = 8
LRU_HEAD_DIM = D_RNN // LRU_HEADS
C_CONV = 4
LRU_C = 8.0
N_GROUPS = 4
EXPERTS_PER_GROUP = 8
N_EXPERTS = N_GROUPS * EXPERTS_PER_GROUP
TOP_K = 2
D_EXPERT = 512
MOE_BLOCK = 128
N_EVEN = (DEPTH + 1) // 2
N_ODD = DEPTH // 2
EPS = 1e-6

kernel_name = "hybrid_gmlp_shortconv_rglru_hmoe"


def rms_norm(x, g):
    xf = x.astype(jnp.float32)
    y = xf * lax.rsqrt(jnp.mean(jnp.square(xf), axis=-1, keepdims=True) + EPS)
    return y.astype(x.dtype) * g


def layer_norm(x, g):
    xf = x.astype(jnp.float32)
    mu = jnp.mean(xf, axis=-1, keepdims=True)
    var = jnp.mean(jnp.square(xf - mu), axis=-1, keepdims=True)
    return ((xf - mu) * lax.rsqrt(var + EPS)).astype(x.dtype) * g


def causal_conv(x, w):
    k_width = w.shape[0]
    t_len = x.shape[1]
    xp = jnp.pad(x, ((0, 0), (k_width - 1, 0), (0, 0)))
    y = xp[:, 0:t_len] * w[0]
    for k in range(1, k_width):
        y = y + xp[:, k:k + t_len] * w[k]
    return y


def mixer_ab(h, w_in, a_ln_g, a_ws, a_ws_b, b_conv_w, w_out):
    bsz, t_len, _ = h.shape
    z = h @ w_in
    u, v, gate_b, gate_c, xb = jnp.split(
        z, [D_A, 2 * D_A, 2 * D_A + D_B, 2 * D_A + 2 * D_B], axis=-1)
    u = jax.nn.gelu(u)
    v = layer_norm(jax.nn.gelu(v), a_ln_g)
    vc = v.reshape(bsz, t_len // CHUNK, CHUNK, A_HEADS, A_HEAD_DIM)
    causal = jnp.tril(jnp.ones((CHUNK, CHUNK), dtype=bool))
    ws = jnp.where(causal, a_ws, 0.0)
    mixed = jnp.einsum('hts,bcshd->bcthd', ws, vc) + a_ws_b.T[:, :, None]
    y_a = u * mixed.reshape(bsz, t_len, D_A)
    y_b = gate_b * causal_conv(gate_c * xb, b_conv_w)
    return jnp.concatenate([y_a, y_b], axis=-1) @ w_out


def mixer_rglru(h, w_in, conv_w, conv_b, w_a, b_a, w_x, b_x, lam, w_out):
    bsz, t_len, _ = h.shape
    gate, xr = jnp.split(h @ w_in, [D_RNN], axis=-1)
    gate = jax.nn.gelu(gate)
    xr = causal_conv(xr, conv_w) + conv_b
    xh = xr.reshape(bsz, t_len, LRU_HEADS, LRU_HEAD_DIM)
    r = jax.nn.sigmoid(jnp.einsum('bthi,hij->bthj', xh, w_a).reshape(bsz, t_len, D_RNN) + b_a)
    i = jax.nn.sigmoid(jnp.einsum('bthi,hij->bthj', xh, w_x).reshape(bsz, t_len, D_RNN) + b_x)
    log_a = (LRU_C * r.astype(jnp.float32)) * jax.nn.log_sigmoid(lam.astype(jnp.float32))
    a = jnp.exp(log_a)
    b = jnp.sqrt(-jnp.expm1(2.0 * log_a)) * (i * xr).astype(jnp.float32)

    def combine(left, right):
        a_l, b_l = left
        a_r, b_r = right
        return a_l * a_r, a_r * b_l + b_r

    _, hseq = lax.associative_scan(combine, (a, b), axis=1)
    return (gate * hseq.astype(h.dtype)) @ w_out


def hierarchical_moe(h, w_rg, b_rg, w_re, b_re, w_gate, w_up, w_down):
    bsz, t_len, d = h.shape
    n_tok = bsz * t_len
    xt = h.reshape(n_tok, d)
    g_logits = (xt @ w_rg + b_rg).astype(jnp.float32)
    g_idx = jnp.argmax(g_logits, axis=-1)
    g_prob = jnp.take_along_axis(jax.nn.softmax(g_logits, axis=-1), g_idx[:, None], axis=-1)
    e_logits = (xt @ w_re + b_re).astype(jnp.float32).reshape(n_tok, N_GROUPS, EXPERTS_PER_GROUP)
    e_sel = jnp.take_along_axis(e_logits, g_idx[:, None, None], axis=1)[:, 0]
    top_val, top_idx = lax.top_k(e_sel, TOP_K)
    gates = g_prob * jax.nn.softmax(top_val, axis=-1)
    expert_ids = g_idx[:, None].astype(jnp.int32) * EXPERTS_PER_GROUP + top_idx.astype(jnp.int32)

    n_slots = n_tok * TOP_K
    n_blocks = -(-n_slots // MOE_BLOCK) + N_EXPERTS
    flat_e = expert_ids.reshape(n_slots)
    flat_tok = jnp.repeat(jnp.arange(n_tok, dtype=jnp.int32), TOP_K)
    flat_g = gates.reshape(n_slots)
    order = jnp.argsort(flat_e)
    sorted_e = flat_e[order]
    counts = jnp.bincount(flat_e, length=N_EXPERTS)
    starts = jnp.cumsum(counts) - counts
    padded = (counts + MOE_BLOCK - 1) // MOE_BLOCK * MOE_BLOCK
    pad_ends = jnp.cumsum(padded)
    pad_starts = pad_ends - padded
    dest = pad_starts[sorted_e] + (jnp.arange(n_slots, dtype=jnp.int32) - starts[sorted_e])
    slot_tok = jnp.full((n_blocks * MOE_BLOCK,), n_tok, jnp.int32).at[dest].set(flat_tok[order])
    slot_g = jnp.zeros((n_blocks * MOE_BLOCK,), jnp.float32).at[dest].set(flat_g[order])
    block_e = jnp.minimum(
        jnp.searchsorted(pad_ends, jnp.arange(n_blocks, dtype=jnp.int32) * MOE_BLOCK, side='right'),
        N_EXPERTS - 1)
    x_pad = jnp.concatenate([xt, jnp.zeros((1, d), xt.dtype)], axis=0)

    def expert_block(args):
        tok_b, g_b, e = args
        xb = x_pad[tok_b]
        hid = jax.nn.silu(xb @ w_gate[e]) * (xb @ w_up[e])
        return (hid @ w_down[e]) * g_b[:, None].astype(xb.dtype)

    y = lax.map(expert_block, (slot_tok.reshape(n_blocks, MOE_BLOCK),
                               slot_g.reshape(n_blocks, MOE_BLOCK), block_e))
    out = jax.ops.segment_sum(y.reshape(-1, d), slot_tok, num_segments=n_tok + 1)[:n_tok]
    return out.reshape(bsz, t_len, d)


def setup_inputs(seed: int = 0) -> dict:
    key = jax.random.key(seed)
    ks = jax.random.split(key, 26)

    def nrm(k, shape, scale):
        return jax.random.normal(k, shape, jnp.float32) * scale

    a8 = jax.random.uniform(ks[17], (N_ODD, D_RNN), jnp.float32, 0.9, 0.999)
    a_base = a8 ** (1.0 / LRU_C)
    c_lambda = jnp.log(a_base) - jnp.log1p(-a_base)
    return {
        "x": nrm(ks[0], (BATCH, SEQ, D_MODEL), 1.0),
        "norm_mix_g": 1.0 + nrm(ks[1], (DEPTH, D_MODEL), 0.05),
        "norm_ffn_g": 1.0 + nrm(ks[2], (DEPTH, D_MODEL), 0.05),
        "norm_final_g": 1.0 + nrm(ks[3], (D_MODEL,), 0.05),
        "ab_w_in": nrm(ks[4], (N_EVEN, D_MODEL, D_IN_AB), D_MODEL ** -0.5),
        "a_ln_g": 1.0 + nrm(ks[5], (N_EVEN, D_A), 0.05),
        "a_ws": nrm(ks[6], (N_EVEN, A_HEADS, CHUNK, CHUNK), CHUNK ** -0.5),
        "a_ws_b": 1.0 + nrm(ks[7], (N_EVEN, A_HEADS, CHUNK), 0.1),
        "b_conv_w": nrm(ks[8], (N_EVEN, B_CONV, D_B), B_CONV ** -0.5),
        "ab_w_out": nrm(ks[9], (N_EVEN, D_A + D_B, D_MODEL), (D_A + D_B) ** -0.5),
        "c_w_in": nrm(ks[10], (N_ODD, D_MODEL, 2 * D_RNN), D_MODEL ** -0.5),
        "c_conv_w": nrm(ks[11], (N_ODD, C_CONV, D_RNN), C_CONV ** -0.5),
        "c_conv_b": nrm(ks[12], (N_ODD, D_RNN), 0.01),
        "c_w_a": nrm(ks[13], (N_ODD, LRU_HEADS, LRU_HEAD_DIM, LRU_HEAD_DIM), LRU_HEAD_DIM ** -0.5),
        "c_b_a": nrm(ks[14], (N_ODD, D_RNN), 0.01),
        "c_w_x": nrm(ks[15], (N_ODD, LRU_HEADS, LRU_HEAD_DIM, LRU_HEAD_DIM), LRU_HEAD_DIM ** -0.5),
        "c_b_x": nrm(ks[16], (N_ODD, D_RNN), 0.01),
        "c_lambda": c_lambda,
        "c_w_out": nrm(ks[18], (N_ODD, D_RNN, D_MODEL), D_RNN ** -0.5),
        "moe_w_rg": nrm(ks[19], (DEPTH, D_MODEL, N_GROUPS), D_MODEL ** -0.5),
        "moe_b_rg": nrm(ks[20], (DEPTH, N_GROUPS), 0.01),
        "moe_w_re": nrm(ks[21], (DEPTH, D_MODEL, N_EXPERTS), D_MODEL ** -0.5),
        "moe_b_re": nrm(ks[22], (DEPTH, N_EXPERTS), 0.01),
        "moe_w_gate": nrm(ks[23], (DEPTH, N_EXPERTS, D_MODEL, D_EXPERT), D_MODEL ** -0.5),
        "moe_w_up": nrm(ks[24], (DEPTH, N_EXPERTS, D_MODEL, D_EXPERT), D_MODEL ** -0.5),
        "moe_w_down": nrm(ks[25], (DEPTH, N_EXPERTS, D_EXPERT, D_MODEL), D_EXPERT ** -0.5),
    }


def reference(x, norm_mix_g, norm_ffn_g, norm_final_g,
              ab_w_in, a_ln_g, a_ws, a_ws_b, b_conv_w, ab_w_out,
              c_w_in, c_conv_w, c_conv_b, c_w_a, c_b_a, c_w_x, c_b_x, c_lambda, c_w_out,
              moe_w_rg, moe_b_rg, moe_w_re, moe_b_re, moe_w_gate, moe_w_up, moe_w_down):
    for layer in range(DEPTH):
        idx = layer // 2
        h = rms_norm(x, norm_mix_g[layer])
        if layer % 2 == 0:
            x = x + mixer_ab(h, ab_w_in[idx], a_ln_g[idx], a_ws[idx], a_ws_b[idx],
                             b_conv_w[idx], ab_w_out[idx])
        else:
            x = x + mixer_rglru(h, c_w_in[idx], c_conv_w[idx], c_conv_b[idx], c_w_a[idx],
                                c_b_a[idx], c_w_x[idx], c_b_x[idx], c_lambda[idx], c_w_out[idx])
        h = rms_norm(x, norm_ffn_g[layer])
        x = x + hierarchical_moe(h, moe_w_rg[layer], moe_b_rg[layer], moe_w_re[layer],
                                 moe_b_re[layer], moe_w_gate[layer], moe_w_up[layer],
                                 moe_w_down[layer])
    return rms_norm(x, norm_final_g)
```

```python
import functools

import jax
import jax.numpy as jnp
from jax import lax
from jax.experimental import pallas as pl
from jax.experimental.pallas import tpu as pltpu

D_MODEL = 1024
LANES = 128
SUBLANES = 8
TT = D_MODEL // LANES
assert TT == SUBLANES

A_HEADS = 4
A_HEAD_DIM = 128
D_A = A_HEADS * A_HEAD_DIM
CHUNK = 128
D_B = D_MODEL - D_A
B_CONV = 3
D_RNN = D_MODEL
LRU_HEADS = 8
LRU_HEAD_DIM = D_RNN // LRU_HEADS
C_CONV = 4
LRU_C = 8.0
N_GROUPS = 4
EXPERTS_PER_GROUP = 8
N_EXPERTS = N_GROUPS * EXPERTS_PER_GROUP
D_EXPERT = 512
EPS = 1e-6

ROUTER_ROWS = 48
VMEM_LIMIT = 56 * 1024 * 1024

BF16 = jnp.bfloat16
F32 = jnp.float32


def _tiles(n_tok):
    tm = 512 if n_tok % 512 == 0 else 256
    blk = 512 if n_tok >= 8192 else 128
    return tm, blk


def _load_tt(ref, nrows):
    return jnp.concatenate(
        [ref[pl.ds(s, nrows, stride=TT), :] for s in range(TT)], axis=1)


def _store_tt(ref, val):
    nrows = val.shape[0]
    for s in range(TT):
        ref[pl.ds(s, nrows, stride=TT), :] = val[:, s * LANES:(s + 1) * LANES]


def _rms(x, g):
    ms = jnp.mean(x * x, axis=-1, keepdims=True)
    return x * lax.rsqrt(ms + EPS) * g


def _dot(a, b):
    return jnp.dot(a, b, preferred_element_type=F32)


def _shift_rows(cur, prev, k):
    rolled = pltpu.roll(cur, k, axis=0)
    row = lax.broadcasted_iota(jnp.int32, cur.shape, 0)
    out = rolled
    for j in range(k):
        src = prev[SUBLANES - k + j:SUBLANES - k + j + 1, :]
        out = jnp.where(row == j, src, out)
    return out


def _mixer_ab_kernel(x_ref, g_ref, win_ref, lng_ref, ws_ref, wsb_ref, cw_ref,
                     wout_ref, o_ref, carry_ref, *, tm):
    t = pl.program_id(1)
    x = x_ref[0]
    h = _rms(x, g_ref[...]).astype(BF16)

    u = jax.nn.gelu(_dot(h, win_ref[:, 0:D_A]))
    v = jax.nn.gelu(_dot(h, win_ref[:, D_A:2 * D_A]))
    mu = jnp.mean(v, axis=-1, keepdims=True)
    vc = v - mu
    var = jnp.mean(vc * vc, axis=-1, keepdims=True)
    vn = (vc * lax.rsqrt(var + EPS) * lng_ref[...]).astype(BF16)

    r_i = lax.broadcasted_iota(jnp.int32, (CHUNK, CHUNK), 0)
    c_i = lax.broadcasted_iota(jnp.int32, (CHUNK, CHUNK), 1)
    causal = r_i >= c_i
    head_cols = []
    for hh in range(A_HEADS):
        wsh = jnp.where(causal, ws_ref[hh], 0.0).astype(BF16)
        rows = []
        for c in range(tm // CHUNK):
            blk = vn[c * CHUNK:(c + 1) * CHUNK,
                     hh * A_HEAD_DIM:(hh + 1) * A_HEAD_DIM]
            rows.append(_dot(wsh, blk))
        head_cols.append(jnp.concatenate(rows, axis=0))
    mixed = jnp.concatenate(head_cols, axis=1) + wsb_ref[...]
    y_a = (u * mixed).astype(BF16)

    gate_b = _dot(h, win_ref[:, 2 * D_A:2 * D_A + D_B])
    gate_c = _dot(h, win_ref[:, 2 * D_A + D_B:2 * D_A + 2 * D_B])
    xb = _dot(h, win_ref[:, 2 * D_A + 2 * D_B:2 * D_A + 3 * D_B])
    cx = gate_c * xb

    @pl.when(t == 0)
    def _():
        carry_ref[...] = jnp.zeros_like(carry_ref)

    prev = carry_ref[...]
    conv = (_shift_rows(cx, prev, 2) * cw_ref[0:1, :]
            + _shift_rows(cx, prev, 1) * cw_ref[1:2, :]
            + cx * cw_ref[2:3, :])
    carry_ref[...] = cx[tm - SUBLANES:tm, :]
    y_b = (gate_b * conv).astype(BF16)

    out = x + _dot(y_a, wout_ref[0:D_A, :]) + _dot(y_b, wout_ref[D_A:D_MODEL, :])
    _store_tt(o_ref, out)


def _mixer_ab(x, g, w_in, ln_g, ws, ws_b, conv_w, w_out, tm):
    bsz, t_len, d = x.shape
    nt = t_len // tm
    wsb_full = jnp.tile(jnp.repeat(ws_b.T, A_HEAD_DIM, axis=1), (tm // CHUNK, 1))
    full = lambda shape: pl.BlockSpec(shape, lambda b, t: (0,) * len(shape))
    return pl.pallas_call(
        functools.partial(_mixer_ab_kernel, tm=tm),
        grid=(bsz, nt),
        in_specs=[
            pl.BlockSpec((1, tm, d), lambda b, t: (b, t, 0)),
            full((1, d)),
            full(w_in.shape),
            full((1, D_A)),
            full(ws.shape),
            full((tm, D_A)),
            full(conv_w.shape),
            full(w_out.shape),
        ],
        out_specs=pl.BlockSpec((tm * TT, LANES), lambda b, t: (b * nt + t, 0)),
        out_shape=jax.ShapeDtypeStruct((bsz * t_len * TT, LANES), F32),
        scratch_shapes=[pltpu.VMEM((SUBLANES, D_B), F32)],
        compiler_params=pltpu.CompilerParams(
            dimension_semantics=("arbitrary", "arbitrary"),
            vmem_limit_bytes=VMEM_LIMIT),
        name="mixer_ab",
    )(x, g.reshape(1, d), w_in.astype(BF16), ln_g.reshape(1, D_A), ws,
      wsb_full, conv_w, w_out.astype(BF16))


def _mixer_rglru_kernel(x_ref, g_ref, win_ref, cw_ref, cb_ref, wa_ref, ba_ref,
                        wx_ref, bx_ref, lam_ref, wout_ref, o_ref,
                        carry_ref, hstate_ref, a_scr, b_scr, h_scr,
                        *, tm, nt, seg, pitch):
    t = pl.program_id(0) % nt
    x = _load_tt(x_ref, tm)
    h = _rms(x, g_ref[...]).astype(BF16)
    gate = jax.nn.gelu(_dot(h, win_ref[:, 0:D_RNN]))
    xr0 = _dot(h, win_ref[:, D_RNN:2 * D_RNN])

    @pl.when(t == 0)
    def _():
        carry_ref[...] = jnp.zeros_like(carry_ref)
        hstate_ref[...] = jnp.zeros_like(hstate_ref)

    prev = carry_ref[...]
    xr = (_shift_rows(xr0, prev, 3) * cw_ref[0:1, :]
          + _shift_rows(xr0, prev, 2) * cw_ref[1:2, :]
          + _shift_rows(xr0, prev, 1) * cw_ref[2:3, :]
          + xr0 * cw_ref[3:4, :]) + cb_ref[...]
    carry_ref[...] = xr0[tm - SUBLANES:tm, :]

    xrb = xr.astype(BF16)
    r_cols, i_cols = [], []
    for hh in range(LRU_HEADS):
        blk = xrb[:, hh * LRU_HEAD_DIM:(hh + 1) * LRU_HEAD_DIM]
        r_cols.append(_dot(blk, wa_ref[hh]))
        i_cols.append(_dot(blk, wx_ref[hh]))
    r = jax.nn.sigmoid(jnp.concatenate(r_cols, axis=1) + ba_ref[...])
    i = jax.nn.sigmoid(jnp.concatenate(i_cols, axis=1) + bx_ref[...])

    lam = lam_ref[...]
    log_sig = jnp.minimum(lam, 0.0) - jnp.log1p(jnp.exp(-jnp.abs(lam)))
    log_a = (LRU_C * r) * log_sig
    a = jnp.exp(log_a)
    b = jnp.sqrt(-jnp.tanh(log_a) * (a * a + 1.0)) * (i * xr)

    n_lg = D_RNN // LANES

    def put(scr, val):
        for c in range(n_lg):
            for s in range(SUBLANES):
                scr[c, s * pitch:s * pitch + seg, :] = (
                    val[s * seg:(s + 1) * seg, c * LANES:(c + 1) * LANES])

    def ld(scr, j):
        return jnp.concatenate(
            [scr[c, pl.ds(j, SUBLANES, stride=pitch), :] for c in range(n_lg)],
            axis=1)

    def st(scr, j, val):
        for c in range(n_lg):
            scr[c, pl.ds(j, SUBLANES, stride=pitch), :] = (
                val[:, c * LANES:(c + 1) * LANES])

    put(a_scr, a)
    put(b_scr, b)

    def seg_step(j, carry):
        hl, pl_ = carry
        aj = ld(a_scr, j)
        hl = aj * hl + ld(b_scr, j)
        pl_ = aj * pl_
        st(h_scr, j, hl)
        st(a_scr, j, pl_)
        return hl, pl_

    zeros = jnp.zeros((SUBLANES, D_RNN), F32)
    h_end, p_end = lax.fori_loop(0, seg, seg_step, (zeros, zeros + 1.0))

    row = lax.broadcasted_iota(jnp.int32, (SUBLANES, D_RNN), 0)
    c = hstate_ref[...]
    cin = jnp.zeros((SUBLANES, D_RNN), F32)
    for s in range(SUBLANES):
        cin = jnp.where(row == s, c, cin)
        c = h_end[s:s + 1, :] + p_end[s:s + 1, :] * c
    hstate_ref[...] = c

    def fix_step(j, _):
        st(h_scr, j, ld(h_scr, j) + ld(a_scr, j) * cin)
        return 0

    lax.fori_loop(0, seg, fix_step, 0)

    hseq = jnp.concatenate(
        [jnp.concatenate([h_scr[c, s * pitch:s * pitch + seg, :]
                          for c in range(n_lg)], axis=1)
         for s in range(SUBLANES)], axis=0)
    out = x + _dot((gate * hseq).astype(BF16), wout_ref[...])
    _store_tt(o_ref, out)


def _mixer_rglru(x_tt, bsz, t_len, g, w_in, conv_w, conv_b, w_a, b_a, w_x, b_x,
                 lam, w_out, tm):
    nt = t_len // tm
    seg = tm // SUBLANES
    pitch = seg + SUBLANES
    d = D_MODEL
    full = lambda shape: pl.BlockSpec(shape, lambda i: (0,) * len(shape))
    row = lambda v: v.reshape(1, -1)
    return pl.pallas_call(
        functools.partial(_mixer_rglru_kernel, tm=tm, nt=nt, seg=seg, pitch=pitch),
        grid=(bsz * nt,),
        in_specs=[
            pl.BlockSpec((tm * TT, LANES), lambda i: (i, 0)),
            full((1, d)),
            full(w_in.shape),
            full(conv_w.shape),
            full((1, D_RNN)),
            full(w_a.shape),
            full((1, D_RNN)),
            full(w_x.shape),
            full((1, D_RNN)),
            full((1, D_RNN)),
            full(w_out.shape),
        ],
        out_specs=pl.BlockSpec((tm * TT, LANES), lambda i: (i, 0)),
        out_shape=jax.ShapeDtypeStruct(x_tt.shape, F32),
        scratch_shapes=[
            pltpu.VMEM((SUBLANES, D_RNN), F32),
            pltpu.VMEM((1, D_RNN), F32),
            pltpu.VMEM((D_RNN // LANES, SUBLANES * pitch, LANES), F32),
            pltpu.VMEM((D_RNN // LANES, SUBLANES * pitch, LANES), F32),
            pltpu.VMEM((D_RNN // LANES, SUBLANES * pitch, LANES), F32),
        ],
        compiler_params=pltpu.CompilerParams(
            dimension_semantics=("arbitrary",),
            vmem_limit_bytes=VMEM_LIMIT),
        name="mixer_rglru",
    )(x_tt, row(g), w_in.astype(BF16), conv_w, row(conv_b), w_a.astype(BF16),
      row(b_a), w_x.astype(BF16), row(b_x), row(lam), w_out.astype(BF16))


def _router_kernel(x_ref, g_ref, wr_ref, br_ref, ids_ref, gates_ref, cnt_ref,
                   carry_ref, *, tm):
    i = pl.program_id(0)
    x = _load_tt(x_ref, tm)
    h = _rms(x, g_ref[...]).astype(BF16)
    lt = lax.dot_general(wr_ref[...], h, (((1,), (1,)), ((), ())),
                         preferred_element_type=F32) + br_ref[...]
    gl = lt[0:N_GROUPS, :]
    gmax = jnp.max(gl, axis=0, keepdims=True)
    g_iota = lax.broadcasted_iota(jnp.int32, gl.shape, 0)
    g_idx = jnp.min(jnp.where(gl == gmax, g_iota, N_GROUPS), axis=0, keepdims=True)
    g_prob = 1.0 / jnp.sum(jnp.exp(gl - gmax), axis=0, keepdims=True)

    esel = lt[8:8 + EXPERTS_PER_GROUP, :]
    for gidx in range(1, N_GROUPS):
        lo = 8 + gidx * EXPERTS_PER_GROUP
        esel = jnp.where(g_idx == gidx, lt[lo:lo + EXPERTS_PER_GROUP, :], esel)
    e_iota = lax.broadcasted_iota(jnp.int32, esel.shape, 0)
    top1 = jnp.max(esel, axis=0, keepdims=True)
    i1 = jnp.min(jnp.where(esel == top1, e_iota, EXPERTS_PER_GROUP), axis=0, keepdims=True)
    rest = jnp.where(e_iota == i1, -jnp.inf, esel)
    top2 = jnp.max(rest, axis=0, keepdims=True)
    i2 = jnp.min(jnp.where(rest == top2, e_iota, EXPERTS_PER_GROUP), axis=0, keepdims=True)
    e2 = jnp.exp(top2 - top1)
    denom = 1.0 + e2
    gate0 = g_prob * (1.0 / denom)
    gate1 = g_prob * (e2 / denom)
    e0 = g_idx * EXPERTS_PER_GROUP + i1
    e1 = g_idx * EXPERTS_PER_GROUP + i2

    x_iota = lax.broadcasted_iota(jnp.int32, (N_EXPERTS, tm), 0)
    hit0 = x_iota == e0
    hit1 = x_iota == e1
    onehot = jnp.where(hit0 | hit1, 1.0, 0.0)
    s_i = lax.broadcasted_iota(jnp.int32, (tm, tm), 0)
    t_i = lax.broadcasted_iota(jnp.int32, (tm, tm), 1)
    before = jnp.where(s_i < t_i, 1.0, 0.0).astype(BF16)

    @pl.when(i == 0)
    def _():
        carry_ref[...] = jnp.zeros_like(carry_ref)

    prefix = _dot(onehot.astype(BF16), before) + carry_ref[:, 0:1]
    rank0 = jnp.sum(jnp.where(hit0, prefix, 0.0), axis=0, keepdims=True)
    rank1 = jnp.sum(jnp.where(hit1, prefix, 0.0), axis=0, keepdims=True)
    carry_ref[...] = carry_ref[...] + jnp.sum(onehot, axis=1, keepdims=True)

    ids_ref[0:1, :] = e0
    ids_ref[1:2, :] = e1
    ids_ref[2:3, :] = rank0.astype(jnp.int32)
    ids_ref[3:4, :] = rank1.astype(jnp.int32)
    ids_ref[4:8, :] = jnp.zeros((4, tm), jnp.int32)
    gates_ref[0:1, :] = gate0
    gates_ref[1:2, :] = gate1
    gates_ref[2:8, :] = jnp.zeros((6, tm), F32)
    cnt_ref[...] = carry_ref[...].astype(jnp.int32)


def _router(x_tt, g, w_rg, b_rg, w_re, b_re, tm):
    n_tok = x_tt.shape[0] // TT
    n_tiles = n_tok // tm
    wr = jnp.zeros((ROUTER_ROWS, D_MODEL), F32)
    wr = wr.at[0:N_GROUPS].set(w_rg.T).at[8:8 + N_EXPERTS].set(w_re.T)
    br = jnp.zeros((ROUTER_ROWS, 1), F32)
    br = br.at[0:N_GROUPS, 0].set(b_rg).at[8:8 + N_EXPERTS, 0].set(b_re)
    full = lambda shape: pl.BlockSpec(shape, lambda i: (0,) * len(shape))
    return pl.pallas_call(
        functools.partial(_router_kernel, tm=tm),
        grid=(n_tiles,),
        in_specs=[
            pl.BlockSpec((tm * TT, LANES), lambda i: (i, 0)),
            full((1, D_MODEL)),
            full((ROUTER_ROWS, D_MODEL)),
            full((ROUTER_ROWS, 1)),
        ],
        out_specs=[
            pl.BlockSpec((8, tm), lambda i: (0, i)),
            pl.BlockSpec((8, tm), lambda i: (0, i)),
            full((N_EXPERTS, LANES)),
        ],
        out_shape=[
            jax.ShapeDtypeStruct((8, n_tok), jnp.int32),
            jax.ShapeDtypeStruct((8, n_tok), F32),
            jax.ShapeDtypeStruct((N_EXPERTS, LANES), jnp.int32),
        ],
        scratch_shapes=[pltpu.VMEM((N_EXPERTS, LANES), F32)],
        compiler_params=pltpu.CompilerParams(
            dimension_semantics=("arbitrary",),
            vmem_limit_bytes=VMEM_LIMIT),
        name="moe_router",
    )(x_tt, g.reshape(1, D_MODEL), wr.astype(BF16), br)


def _dest_kernel(ids_ref, pstart_ref, dest_ref):
    ids = ids_ref[...]
    x_iota = lax.broadcasted_iota(jnp.int32, (N_EXPERTS, ids.shape[1]), 0)
    pstart = pstart_ref[:, 0:1]
    for k in range(2):
        hit = x_iota == ids[k:k + 1, :]
        base = jnp.sum(jnp.where(hit, pstart, 0), axis=0, keepdims=True)
        dest_ref[k:k + 1, :] = base + ids[2 + k:3 + k, :]
    dest_ref[2:8, :] = jnp.zeros((6, ids.shape[1]), jnp.int32)


def _dest(ids, pad_starts, tm):
    n_tok = ids.shape[1]
    pstart = jnp.broadcast_to(pad_starts[:, None], (N_EXPERTS, LANES))
    return pl.pallas_call(
        _dest_kernel,
        grid=(n_tok // tm,),
        in_specs=[
            pl.BlockSpec((8, tm), lambda i: (0, i)),
            pl.BlockSpec((N_EXPERTS, LANES), lambda i: (0, 0)),
        ],
        out_specs=pl.BlockSpec((8, tm), lambda i: (0, i)),
        out_shape=jax.ShapeDtypeStruct((8, n_tok), jnp.int32),
        name="moe_dest",
    )(ids, pstart)


def _dispatch_kernel(pstart_ref, pend_ref, dest_ref, x_hbm, zero_ref, xs_hbm,
                     sem, zsem, *, tm, blk, n_blocks):
    i = pl.program_id(0)
    n_tiles = pl.num_programs(0)

    def tile_wait():
        pltpu.make_async_copy(x_hbm.at[pl.ds(0, 2 * tm * TT), :],
                              xs_hbm.at[pl.ds(0, 2 * tm * TT), :], sem).wait()

    @pl.when(i == 0)
    def _():
        def zcopy(e):
            start = pl.multiple_of((pend_ref[e] - blk) * TT, blk * TT)
            return pltpu.make_async_copy(
                zero_ref, xs_hbm.at[pl.ds(start, blk * TT), :], zsem)

        def zstart(e, _):
            @pl.when(pend_ref[e] > pstart_ref[e])
            def _():
                zcopy(e).start()
            return 0

        def zwait(e, _):
            @pl.when(pend_ref[e] > pstart_ref[e])
            def _():
                zcopy(e).wait()
            return 0

        def tcopy(b):
            return pltpu.make_async_copy(
                zero_ref,
                xs_hbm.at[pl.ds(pl.multiple_of(b * (blk * TT), blk * TT), blk * TT), :],
                zsem)

        def tstart(b, _):
            tcopy(b).start()
            return 0

        def twait(b, _):
            tcopy(b).wait()
            return 0

        n_used = pend_ref[N_EXPERTS - 1] // blk
        lax.fori_loop(0, N_EXPERTS, zstart, 0)
        lax.fori_loop(n_used, n_blocks, tstart, 0)
        lax.fori_loop(0, N_EXPERTS, zwait, 0)
        lax.fori_loop(n_used, n_blocks, twait, 0)

    @pl.when(i > 0)
    def _():
        tile_wait()

    def issue(j, _):
        src = x_hbm.at[pl.ds(pl.multiple_of((i * tm + j) * TT, TT), TT), :]
        for k in range(2):
            d = dest_ref[0, 0, k * tm + j]
            pltpu.make_async_copy(
                src, xs_hbm.at[pl.ds(pl.multiple_of(d * TT, TT), TT), :], sem).start()
        return 0

    lax.fori_loop(0, tm, issue, 0)

    @pl.when(i == n_tiles - 1)
    def _():
        tile_wait()


def _dispatch(x_tt, dest_flat, pad_starts, pad_ends, n_rows, tm, blk):
    n_tok = x_tt.shape[0] // TT
    n_tiles = n_tok // tm
    zeros = jnp.zeros((blk * TT, LANES), F32)
    return pl.pallas_call(
        functools.partial(_dispatch_kernel, tm=tm, blk=blk, n_blocks=n_rows // blk),
        grid_spec=pltpu.PrefetchScalarGridSpec(
            num_scalar_prefetch=2,
            grid=(n_tiles,),
            in_specs=[
                pl.BlockSpec((1, 1, 2 * tm), lambda i, ps, pe: (i, 0, 0),
                             memory_space=pltpu.SMEM),
                pl.BlockSpec(memory_space=pl.ANY),
                pl.BlockSpec((blk * TT, LANES), lambda i, ps, pe: (0, 0)),
            ],
            out_specs=pl.BlockSpec(memory_space=pl.ANY),
            scratch_shapes=[pltpu.SemaphoreType.DMA(()), pltpu.SemaphoreType.DMA(())],
        ),
        out_shape=jax.ShapeDtypeStruct((n_rows * TT, LANES), F32),
        compiler_params=pltpu.CompilerParams(dimension_semantics=("arbitrary",)),
        name="moe_dispatch",
    )(pad_starts, pad_ends, dest_flat, x_tt, zeros)


def _expert_kernel(be_ref, nu_ref, xs_ref, g_ref, wg_ref, wu_ref, wd_ref, y_ref,
                   *, blk):
    i = pl.program_id(0)

    @pl.when(i < nu_ref[0])
    def _():
        x = _load_tt(xs_ref, blk)
        h = _rms(x, g_ref[...]).astype(BF16)
        hid = jax.nn.silu(_dot(h, wg_ref[0])) * _dot(h, wu_ref[0])
        _store_tt(y_ref, _dot(hid.astype(BF16), wd_ref[0]))

    @pl.when(i >= nu_ref[0])
    def _():
        y_ref[...] = jnp.zeros_like(y_ref)


def _experts(xs, g, w_gate, w_up, w_down, block_e, n_used, blk):
    n_blocks = xs.shape[0] // (blk * TT)
    last = lambda i, be, nu: jnp.minimum(i, nu[0] - 1)
    return pl.pallas_call(
        functools.partial(_expert_kernel, blk=blk),
        grid_spec=pltpu.PrefetchScalarGridSpec(
            num_scalar_prefetch=2,
            grid=(n_blocks,),
            in_specs=[
                pl.BlockSpec((blk * TT, LANES), lambda i, be, nu: (last(i, be, nu), 0)),
                pl.BlockSpec((1, D_MODEL), lambda i, be, nu: (0, 0)),
                pl.BlockSpec((1, D_MODEL, D_EXPERT), lambda i, be, nu: (be[i], 0, 0)),
                pl.BlockSpec((1, D_MODEL, D_EXPERT), lambda i, be, nu: (be[i], 0, 0)),
                pl.BlockSpec((1, D_EXPERT, D_MODEL), lambda i, be, nu: (be[i], 0, 0)),
            ],
            out_specs=pl.BlockSpec((blk * TT, LANES), lambda i, be, nu: (i, 0)),
        ),
        out_shape=jax.ShapeDtypeStruct(xs.shape, F32),
        compiler_params=pltpu.CompilerParams(
            dimension_semantics=("arbitrary",),
            vmem_limit_bytes=VMEM_LIMIT),
        name="moe_experts",
    )(block_e, n_used, xs, g.reshape(1, D_MODEL), w_gate.astype(BF16),
      w_up.astype(BF16), w_down.astype(BF16))


def _combine_kernel(dest_ref, x_ref, gates_ref, y_hbm, gf_ref, o_ref,
                    ya_ref, yb_ref, sem, *, tm, final):
    def issue(j, _):
        for k, buf in enumerate((ya_ref, yb_ref)):
            d = dest_ref[0, 0, k * tm + j]
            pltpu.make_async_copy(
                y_hbm.at[pl.ds(pl.multiple_of(d * TT, TT), TT), :],
                buf.at[pl.ds(pl.multiple_of(j * TT, TT), TT), :], sem).start()
        return 0

    lax.fori_loop(0, tm, issue, 0)
    gpad = jnp.concatenate(
        [gates_ref[...], jnp.zeros((LANES - 8, tm), F32)], axis=0)
    gcol = gpad.T
    x = _load_tt(x_ref, tm)
    for buf in (ya_ref, yb_ref):
        pltpu.make_async_copy(y_hbm.at[pl.ds(0, tm * TT), :], buf, sem).wait()
    out = (x + gcol[:, 0:1] * _load_tt(ya_ref, tm)
           + gcol[:, 1:2] * _load_tt(yb_ref, tm))
    if final:
        o_ref[...] = _rms(out, gf_ref[...])
    else:
        _store_tt(o_ref, out)


def _combine(x_tt, y, dest_flat, gates, g_final, tm, final):
    n_tok = x_tt.shape[0] // TT
    n_tiles = n_tok // tm
    if final:
        out_spec = pl.BlockSpec((tm, D_MODEL), lambda i: (i, 0))
        out_shape = jax.ShapeDtypeStruct((n_tok, D_MODEL), F32)
    else:
        out_spec = pl.BlockSpec((tm * TT, LANES), lambda i: (i, 0))
        out_shape = jax.ShapeDtypeStruct(x_tt.shape, F32)
    return pl.pallas_call(
        functools.partial(_combine_kernel, tm=tm, final=final),
        grid=(n_tiles,),
        in_specs=[
            pl.BlockSpec((1, 1, 2 * tm), lambda i: (i, 0, 0),
                         memory_space=pltpu.SMEM),
            pl.BlockSpec((tm * TT, LANES), lambda i: (i, 0)),
            pl.BlockSpec((8, tm), lambda i: (0, i)),
            pl.BlockSpec(memory_space=pl.ANY),
            pl.BlockSpec((1, D_MODEL), lambda i: (0, 0)),
        ],
        out_specs=out_spec,
        out_shape=out_shape,
        scratch_shapes=[
            pltpu.VMEM((tm * TT, LANES), F32),
            pltpu.VMEM((tm * TT, LANES), F32),
            pltpu.SemaphoreType.DMA(()),
        ],
        compiler_params=pltpu.CompilerParams(
            dimension_semantics=("arbitrary",),
            vmem_limit_bytes=VMEM_LIMIT),
        name="moe_combine",
    )(dest_flat, x_tt, gates, y, g_final.reshape(1, D_MODEL))


def _moe(x_tt, g, w_rg, b_rg, w_re, b_re, w_gate, w_up, w_down, g_final, final):
    n_tok = x_tt.shape[0] // TT
    tm, blk = _tiles(n_tok)
    n_tiles = n_tok // tm
    n_blocks = (2 * n_tok) // blk + N_EXPERTS
    ids, gates, cnt = _router(x_tt, g, w_rg, b_rg, w_re, b_re, tm)

    counts = cnt[:, 0]
    padded = (counts + blk - 1) // blk * blk
    pad_ends = jnp.cumsum(padded).astype(jnp.int32)
    pad_starts = pad_ends - padded
    block_e = jnp.minimum(
        jnp.searchsorted(pad_ends, jnp.arange(n_blocks, dtype=jnp.int32) * blk,
                         side="right"), N_EXPERTS - 1).astype(jnp.int32)
    n_used = (pad_ends[-1:] // blk).astype(jnp.int32)

    dest = _dest(ids, pad_starts, tm)
    dest_flat = dest[0:2].reshape(2, n_tiles, tm).transpose(1, 0, 2).reshape(n_tiles, 1, 2 * tm)
    xs = _dispatch(x_tt, dest_flat, pad_starts, pad_ends, n_blocks * blk, tm, blk)
    y = _experts(xs, g, w_gate, w_up, w_down, block_e, n_used, blk)
    return _combine(x_tt, y, dest_flat, gates, g_final, tm, final)


def kernel(x, norm_mix_g, norm_ffn_g, norm_final_g, ab_w_in, a_ln_g, a_ws, a_ws_b, b_conv_w, ab_w_out, c_w_in, c_conv_w, c_conv_b, c_w_a, c_b_a, c_w_x, c_b_x, c_lambda, c_w_out, moe_w_rg, moe_b_rg, moe_w_re, moe_b_re, moe_w_gate, moe_w_up, moe_w_down):
    bsz, t_len, d = x.shape
    assert d == D_MODEL
    tm, _ = _tiles(bsz * t_len)
    assert t_len % tm == 0 and tm % CHUNK == 0

    x_tt = _mixer_ab(x, norm_mix_g[0], ab_w_in[0], a_ln_g[0], a_ws[0], a_ws_b[0],
                     b_conv_w[0], ab_w_out[0], tm)
    x_tt = _moe(x_tt, norm_ffn_g[0], moe_w_rg[0], moe_b_rg[0], moe_w_re[0],
                moe_b_re[0], moe_w_gate[0], moe_w_up[0], moe_w_down[0],
                norm_final_g, final=False)
    x_tt = _mixer_rglru(x_tt, bsz, t_len, norm_mix_g[1], c_w_in[0], c_conv_w[0],
                        c_conv_b[0], c_w_a[0], c_b_a[0], c_w_x[0], c_b_x[0],
                        c_lambda[0], c_w_out[0], tm)
    out = _moe(x_tt, norm_ffn_g[1], moe_w_rg[1], moe_b_rg[1], moe_w_re[1],
               moe_b_re[1], moe_w_gate[1], moe_w_up[1], moe_w_down[1],
               norm_final_g, final=True)
    return out.reshape(bsz, t_len, d)
```

```python
import functools

import jax
import jax.numpy as jnp
from jax import lax
from jax.experimental import pallas as pl
from jax.experimental.pallas import tpu as pltpu

D_MODEL = 1024
LANES = 128
SUBLANES = 8
TT = D_MODEL // LANES
assert TT == SUBLANES

A_HEADS = 4
A_HEAD_DIM = 128
D_A = A_HEADS * A_HEAD_DIM
CHUNK = 128
D_B = D_MODEL - D_A
B_CONV = 3
D_RNN = D_MODEL
LRU_HEADS = 8
LRU_HEAD_DIM = D_RNN // LRU_HEADS
C_CONV = 4
LRU_C = 8.0
N_GROUPS = 4
EXPERTS_PER_GROUP = 8
N_EXPERTS = N_GROUPS * EXPERTS_PER_GROUP
D_EXPERT = 512
EPS = 1e-6

ROUTER_ROWS = 48
VMEM_LIMIT = 56 * 1024 * 1024

BF16 = jnp.bfloat16
F32 = jnp.float32


def _tiles(n_tok):
    tm = 512 if n_tok % 512 == 0 else 256
    td = 1024 if n_tok % 1024 == 0 else tm
    blk = 512 if n_tok >= 8192 else 128
    return tm, td, blk


DMA_UNROLL = 8


def _load_tt(ref, nrows):
    return jnp.concatenate(
        [ref[pl.ds(s, nrows, stride=TT), :] for s in range(TT)], axis=1)


def _store_tt(ref, val):
    nrows = val.shape[0]
    for s in range(TT):
        ref[pl.ds(s, nrows, stride=TT), :] = val[:, s * LANES:(s + 1) * LANES]


def _rms(x, g):
    ms = jnp.mean(x * x, axis=-1, keepdims=True)
    return x * lax.rsqrt(ms + EPS) * g


def _dot(a, b):
    return jnp.dot(a, b, preferred_element_type=F32)


def _shift_rows(cur, prev, k):
    rolled = pltpu.roll(cur, k, axis=0)
    row = lax.broadcasted_iota(jnp.int32, cur.shape, 0)
    out = rolled
    for j in range(k):
        src = prev[SUBLANES - k + j:SUBLANES - k + j + 1, :]
        out = jnp.where(row == j, src, out)
    return out


def _mixer_ab_kernel(x_ref, g_ref, win_ref, lng_ref, ws_ref, wsb_ref, cw_ref,
                     wout_ref, o_ref, carry_ref, *, tm):
    t = pl.program_id(1)
    x = x_ref[0]
    h = _rms(x, g_ref[...]).astype(BF16)

    u = jax.nn.gelu(_dot(h, win_ref[:, 0:D_A]))
    v = jax.nn.gelu(_dot(h, win_ref[:, D_A:2 * D_A]))
    mu = jnp.mean(v, axis=-1, keepdims=True)
    vc = v - mu
    var = jnp.mean(vc * vc, axis=-1, keepdims=True)
    vn = (vc * lax.rsqrt(var + EPS) * lng_ref[...]).astype(BF16)

    r_i = lax.broadcasted_iota(jnp.int32, (CHUNK, CHUNK), 0)
    c_i = lax.broadcasted_iota(jnp.int32, (CHUNK, CHUNK), 1)
    causal = r_i >= c_i
    head_cols = []
    for hh in range(A_HEADS):
        wsh = jnp.where(causal, ws_ref[hh], 0.0).astype(BF16)
        rows = []
        for c in range(tm // CHUNK):
            blk = vn[c * CHUNK:(c + 1) * CHUNK,
                     hh * A_HEAD_DIM:(hh + 1) * A_HEAD_DIM]
            rows.append(_dot(wsh, blk))
        head_cols.append(jnp.concatenate(rows, axis=0))
    mixed = jnp.concatenate(head_cols, axis=1) + wsb_ref[...]
    y_a = (u * mixed).astype(BF16)

    gate_b = _dot(h, win_ref[:, 2 * D_A:2 * D_A + D_B])
    gate_c = _dot(h, win_ref[:, 2 * D_A + D_B:2 * D_A + 2 * D_B])
    xb = _dot(h, win_ref[:, 2 * D_A + 2 * D_B:2 * D_A + 3 * D_B])
    cx = gate_c * xb

    @pl.when(t == 0)
    def _():
        carry_ref[...] = jnp.zeros_like(carry_ref)

    prev = carry_ref[...]
    conv = (_shift_rows(cx, prev, 2) * cw_ref[0:1, :]
            + _shift_rows(cx, prev, 1) * cw_ref[1:2, :]
            + cx * cw_ref[2:3, :])
    carry_ref[...] = cx[tm - SUBLANES:tm, :]
    y_b = (gate_b * conv).astype(BF16)

    out = x + _dot(y_a, wout_ref[0:D_A, :]) + _dot(y_b, wout_ref[D_A:D_MODEL, :])
    _store_tt(o_ref, out)


def _mixer_ab(x, g, w_in, ln_g, ws, ws_b, conv_w, w_out, tm):
    bsz, t_len, d = x.shape
    nt = t_len // tm
    wsb_full = jnp.tile(jnp.repeat(ws_b.T, A_HEAD_DIM, axis=1), (tm // CHUNK, 1))
    full = lambda shape: pl.BlockSpec(shape, lambda b, t: (0,) * len(shape))
    return pl.pallas_call(
        functools.partial(_mixer_ab_kernel, tm=tm),
        grid=(bsz, nt),
        in_specs=[
            pl.BlockSpec((1, tm, d), lambda b, t: (b, t, 0)),
            full((1, d)),
            full(w_in.shape),
            full((1, D_A)),
            full(ws.shape),
            full((tm, D_A)),
            full(conv_w.shape),
            full(w_out.shape),
        ],
        out_specs=pl.BlockSpec((tm * TT, LANES), lambda b, t: (b * nt + t, 0)),
        out_shape=jax.ShapeDtypeStruct((bsz * t_len * TT, LANES), F32),
        scratch_shapes=[pltpu.VMEM((SUBLANES, D_B), F32)],
        compiler_params=pltpu.CompilerParams(
            dimension_semantics=("arbitrary", "arbitrary"),
            vmem_limit_bytes=VMEM_LIMIT),
        name="mixer_ab",
    )(x, g.reshape(1, d), w_in.astype(BF16), ln_g.reshape(1, D_A), ws,
      wsb_full, conv_w, w_out.astype(BF16))


def _mixer_rglru_kernel(x_ref, g_ref, win_ref, cw_ref, cb_ref, wa_ref, ba_ref,
                        wx_ref, bx_ref, lam_ref, wout_ref, o_ref,
                        carry_ref, hstate_ref, a_scr, b_scr, h_scr,
                        *, tm, nt, seg, pitch):
    t = pl.program_id(0) % nt
    x = _load_tt(x_ref, tm)
    h = _rms(x, g_ref[...]).astype(BF16)
    gate = jax.nn.gelu(_dot(h, win_ref[:, 0:D_RNN]))
    xr0 = _dot(h, win_ref[:, D_RNN:2 * D_RNN])

    @pl.when(t == 0)
    def _():
        carry_ref[...] = jnp.zeros_like(carry_ref)
        hstate_ref[...] = jnp.zeros_like(hstate_ref)

    prev = carry_ref[...]
    xr = (_shift_rows(xr0, prev, 3) * cw_ref[0:1, :]
          + _shift_rows(xr0, prev, 2) * cw_ref[1:2, :]
          + _shift_rows(xr0, prev, 1) * cw_ref[2:3, :]
          + xr0 * cw_ref[3:4, :]) + cb_ref[...]
    carry_ref[...] = xr0[tm - SUBLANES:tm, :]

    xrb = xr.astype(BF16)
    r_cols, i_cols = [], []
    for hh in range(LRU_HEADS):
        blk = xrb[:, hh * LRU_HEAD_DIM:(hh + 1) * LRU_HEAD_DIM]
        r_cols.append(_dot(blk, wa_ref[hh]))
        i_cols.append(_dot(blk, wx_ref[hh]))
    r = jax.nn.sigmoid(jnp.concatenate(r_cols, axis=1) + ba_ref[...])
    i = jax.nn.sigmoid(jnp.concatenate(i_cols, axis=1) + bx_ref[...])

    lam = lam_ref[...]
    log_sig = jnp.minimum(lam, 0.0) - jnp.log1p(jnp.exp(-jnp.abs(lam)))
    log_a = (LRU_C * r) * log_sig
    a = jnp.exp(log_a)
    b = jnp.sqrt(-jnp.tanh(log_a) * (a * a + 1.0)) * (i * xr)

    n_lg = D_RNN // LANES

    def put(scr, val):
        for c in range(n_lg):
            for s in range(SUBLANES):
                scr[c, s * pitch:s * pitch + seg, :] = (
                    val[s * seg:(s + 1) * seg, c * LANES:(c + 1) * LANES])

    def ld(scr, j):
        return jnp.concatenate(
            [scr[c, pl.ds(j, SUBLANES, stride=pitch), :] for c in range(n_lg)],
            axis=1)

    def st(scr, j, val):
        for c in range(n_lg):
            scr[c, pl.ds(j, SUBLANES, stride=pitch), :] = (
                val[:, c * LANES:(c + 1) * LANES])

    put(a_scr, a)
    put(b_scr, b)

    def seg_step(j, carry):
        hl, pl_ = carry
        aj = ld(a_scr, j)
        hl = aj * hl + ld(b_scr, j)
        pl_ = aj * pl_
        st(h_scr, j, hl)
        st(a_scr, j, pl_)
        return hl, pl_

    zeros = jnp.zeros((SUBLANES, D_RNN), F32)
    h_end, p_end = lax.fori_loop(0, seg, seg_step, (zeros, zeros + 1.0))

    row = lax.broadcasted_iota(jnp.int32, (SUBLANES, D_RNN), 0)
    c = hstate_ref[...]
    cin = jnp.zeros((SUBLANES, D_RNN), F32)
    for s in range(SUBLANES):
        cin = jnp.where(row == s, c, cin)
        c = h_end[s:s + 1, :] + p_end[s:s + 1, :] * c
    hstate_ref[...] = c

    def fix_step(j, _):
        st(h_scr, j, ld(h_scr, j) + ld(a_scr, j) * cin)
        return 0

    lax.fori_loop(0, seg, fix_step, 0)

    hseq = jnp.concatenate(
        [jnp.concatenate([h_scr[c, s * pitch:s * pitch + seg, :]
                          for c in range(n_lg)], axis=1)
         for s in range(SUBLANES)], axis=0)
    out = x + _dot((gate * hseq).astype(BF16), wout_ref[...])
    _store_tt(o_ref, out)


def _mixer_rglru(x_tt, bsz, t_len, g, w_in, conv_w, conv_b, w_a, b_a, w_x, b_x,
                 lam, w_out, tm):
    nt = t_len // tm
    seg = tm // SUBLANES
    pitch = seg + SUBLANES
    d = D_MODEL
    full = lambda shape: pl.BlockSpec(shape, lambda i: (0,) * len(shape))
    row = lambda v: v.reshape(1, -1)
    return pl.pallas_call(
        functools.partial(_mixer_rglru_kernel, tm=tm, nt=nt, seg=seg, pitch=pitch),
        grid=(bsz * nt,),
        in_specs=[
            pl.BlockSpec((tm * TT, LANES), lambda i: (i, 0)),
            full((1, d)),
            full(w_in.shape),
            full(conv_w.shape),
            full((1, D_RNN)),
            full(w_a.shape),
            full((1, D_RNN)),
            full(w_x.shape),
            full((1, D_RNN)),
            full((1, D_RNN)),
            full(w_out.shape),
        ],
        out_specs=pl.BlockSpec((tm * TT, LANES), lambda i: (i, 0)),
        out_shape=jax.ShapeDtypeStruct(x_tt.shape, F32),
        scratch_shapes=[
            pltpu.VMEM((SUBLANES, D_RNN), F32),
            pltpu.VMEM((1, D_RNN), F32),
            pltpu.VMEM((D_RNN // LANES, SUBLANES * pitch, LANES), F32),
            pltpu.VMEM((D_RNN // LANES, SUBLANES * pitch, LANES), F32),
            pltpu.VMEM((D_RNN // LANES, SUBLANES * pitch, LANES), F32),
        ],
        compiler_params=pltpu.CompilerParams(
            dimension_semantics=("arbitrary",),
            vmem_limit_bytes=VMEM_LIMIT),
        name="mixer_rglru",
    )(x_tt, row(g), w_in.astype(BF16), conv_w, row(conv_b), w_a.astype(BF16),
      row(b_a), w_x.astype(BF16), row(b_x), row(lam), w_out.astype(BF16))


def _router_kernel(x_ref, g_ref, wr_ref, br_ref, ids_ref, gates_ref, cnt_ref,
                   carry_ref, *, tm):
    i = pl.program_id(0)
    x = _load_tt(x_ref, tm)
    h = _rms(x, g_ref[...]).astype(BF16)
    lt = lax.dot_general(wr_ref[...], h, (((1,), (1,)), ((), ())),
                         preferred_element_type=F32) + br_ref[...]
    gl = lt[0:N_GROUPS, :]
    gmax = jnp.max(gl, axis=0, keepdims=True)
    g_iota = lax.broadcasted_iota(jnp.int32, gl.shape, 0)
    g_idx = jnp.min(jnp.where(gl == gmax, g_iota, N_GROUPS), axis=0, keepdims=True)
    g_prob = 1.0 / jnp.sum(jnp.exp(gl - gmax), axis=0, keepdims=True)

    esel = lt[8:8 + EXPERTS_PER_GROUP, :]
    for gidx in range(1, N_GROUPS):
        lo = 8 + gidx * EXPERTS_PER_GROUP
        esel = jnp.where(g_idx == gidx, lt[lo:lo + EXPERTS_PER_GROUP, :], esel)
    e_iota = lax.broadcasted_iota(jnp.int32, esel.shape, 0)
    top1 = jnp.max(esel, axis=0, keepdims=True)
    i1 = jnp.min(jnp.where(esel == top1, e_iota, EXPERTS_PER_GROUP), axis=0, keepdims=True)
    rest = jnp.where(e_iota == i1, -jnp.inf, esel)
    top2 = jnp.max(rest, axis=0, keepdims=True)
    i2 = jnp.min(jnp.where(rest == top2, e_iota, EXPERTS_PER_GROUP), axis=0, keepdims=True)
    e2 = jnp.exp(top2 - top1)
    denom = 1.0 + e2
    gate0 = g_prob * (1.0 / denom)
    gate1 = g_prob * (e2 / denom)
    e0 = g_idx * EXPERTS_PER_GROUP + i1
    e1 = g_idx * EXPERTS_PER_GROUP + i2

    x_iota = lax.broadcasted_iota(jnp.int32, (N_EXPERTS, tm), 0)
    hit0 = x_iota == e0
    hit1 = x_iota == e1
    onehot = jnp.where(hit0 | hit1, 1.0, 0.0)
    s_i = lax.broadcasted_iota(jnp.int32, (tm, tm), 0)
    t_i = lax.broadcasted_iota(jnp.int32, (tm, tm), 1)
    before = jnp.where(s_i < t_i, 1.0, 0.0).astype(BF16)

    @pl.when(i == 0)
    def _():
        carry_ref[...] = jnp.zeros_like(carry_ref)

    prefix = _dot(onehot.astype(BF16), before) + carry_ref[:, 0:1]
    rank0 = jnp.sum(jnp.where(hit0, prefix, 0.0), axis=0, keepdims=True)
    rank1 = jnp.sum(jnp.where(hit1, prefix, 0.0), axis=0, keepdims=True)
    carry_ref[...] = carry_ref[...] + jnp.sum(onehot, axis=1, keepdims=True)

    ids_ref[0:1, :] = e0
    ids_ref[1:2, :] = e1
    ids_ref[2:3, :] = rank0.astype(jnp.int32)
    ids_ref[3:4, :] = rank1.astype(jnp.int32)
    ids_ref[4:8, :] = jnp.zeros((4, tm), jnp.int32)
    gates_ref[0:1, :] = gate0
    gates_ref[1:2, :] = gate1
    gates_ref[2:8, :] = jnp.zeros((6, tm), F32)
    cnt_ref[...] = carry_ref[...].astype(jnp.int32)


def _router(x_tt, g, w_rg, b_rg, w_re, b_re, tm):
    n_tok = x_tt.shape[0] // TT
    n_tiles = n_tok // tm
    wr = jnp.zeros((ROUTER_ROWS, D_MODEL), F32)
    wr = wr.at[0:N_GROUPS].set(w_rg.T).at[8:8 + N_EXPERTS].set(w_re.T)
    br = jnp.zeros((ROUTER_ROWS, 1), F32)
    br = br.at[0:N_GROUPS, 0].set(b_rg).at[8:8 + N_EXPERTS, 0].set(b_re)
    full = lambda shape: pl.BlockSpec(shape, lambda i: (0,) * len(shape))
    return pl.pallas_call(
        functools.partial(_router_kernel, tm=tm),
        grid=(n_tiles,),
        in_specs=[
            pl.BlockSpec((tm * TT, LANES), lambda i: (i, 0)),
            full((1, D_MODEL)),
            full((ROUTER_ROWS, D_MODEL)),
            full((ROUTER_ROWS, 1)),
        ],
        out_specs=[
            pl.BlockSpec((8, tm), lambda i: (0, i)),
            pl.BlockSpec((8, tm), lambda i: (0, i)),
            full((N_EXPERTS, LANES)),
        ],
        out_shape=[
            jax.ShapeDtypeStruct((8, n_tok), jnp.int32),
            jax.ShapeDtypeStruct((8, n_tok), F32),
            jax.ShapeDtypeStruct((N_EXPERTS, LANES), jnp.int32),
        ],
        scratch_shapes=[pltpu.VMEM((N_EXPERTS, LANES), F32)],
        compiler_params=pltpu.CompilerParams(
            dimension_semantics=("arbitrary",),
            vmem_limit_bytes=VMEM_LIMIT),
        name="moe_router",
    )(x_tt, g.reshape(1, D_MODEL), wr.astype(BF16), br)


def _dest_kernel(ids_ref, pstart_ref, dest_ref):
    ids = ids_ref[...]
    x_iota = lax.broadcasted_iota(jnp.int32, (N_EXPERTS, ids.shape[1]), 0)
    pstart = pstart_ref[:, 0:1]
    for k in range(2):
        hit = x_iota == ids[k:k + 1, :]
        base = jnp.sum(jnp.where(hit, pstart, 0), axis=0, keepdims=True)
        dest_ref[k:k + 1, :] = base + ids[2 + k:3 + k, :]
    dest_ref[2:8, :] = jnp.zeros((6, ids.shape[1]), jnp.int32)


def _dest(ids, pad_starts, tm):
    n_tok = ids.shape[1]
    pstart = jnp.broadcast_to(pad_starts[:, None], (N_EXPERTS, LANES))
    return pl.pallas_call(
        _dest_kernel,
        grid=(n_tok // tm,),
        in_specs=[
            pl.BlockSpec((8, tm), lambda i: (0, i)),
            pl.BlockSpec((N_EXPERTS, LANES), lambda i: (0, 0)),
        ],
        out_specs=pl.BlockSpec((8, tm), lambda i: (0, i)),
        out_shape=jax.ShapeDtypeStruct((8, n_tok), jnp.int32),
        name="moe_dest",
    )(ids, pstart)


def _dispatch_kernel(pstart_ref, pend_ref, dest_ref, x_ref, zero_ref, xs_hbm,
                     sem, zsem, *, tm, blk, n_blocks):
    i = pl.program_id(0)

    @pl.when(i == 0)
    def _():
        def zcopy(e):
            start = pl.multiple_of((pend_ref[e] - blk) * TT, blk * TT)
            return pltpu.make_async_copy(
                zero_ref, xs_hbm.at[pl.ds(start, blk * TT), :], zsem)

        def zstart(e, _):
            @pl.when(pend_ref[e] > pstart_ref[e])
            def _():
                zcopy(e).start()
            return 0

        def zwait(e, _):
            @pl.when(pend_ref[e] > pstart_ref[e])
            def _():
                zcopy(e).wait()
            return 0

        def tcopy(b):
            return pltpu.make_async_copy(
                zero_ref,
                xs_hbm.at[pl.ds(pl.multiple_of(b * (blk * TT), blk * TT), blk * TT), :],
                zsem)

        def tstart(b, _):
            tcopy(b).start()
            return 0

        def twait(b, _):
            tcopy(b).wait()
            return 0

        n_used = pend_ref[N_EXPERTS - 1] // blk
        lax.fori_loop(0, N_EXPERTS, zstart, 0)
        lax.fori_loop(n_used, n_blocks, tstart, 0)
        lax.fori_loop(0, N_EXPERTS, zwait, 0)
        lax.fori_loop(n_used, n_blocks, twait, 0)

    def issue(jj, _):
        for u in range(DMA_UNROLL):
            j = jj * DMA_UNROLL + u
            src = x_ref.at[pl.ds(pl.multiple_of(j * TT, TT), TT), :]
            for k in range(2):
                d = dest_ref[0, 0, k * tm + j]
                pltpu.make_async_copy(
                    src, xs_hbm.at[pl.ds(pl.multiple_of(d * TT, TT), TT), :],
                    sem).start()
        return 0

    lax.fori_loop(0, tm // DMA_UNROLL, issue, 0)
    for _ in range(2):
        pltpu.make_async_copy(x_ref, xs_hbm.at[pl.ds(0, tm * TT), :], sem).wait()


def _dispatch(x_tt, dest_flat, pad_starts, pad_ends, n_rows, tm, blk):
    n_tok = x_tt.shape[0] // TT
    n_tiles = n_tok // tm
    zeros = jnp.zeros((blk * TT, LANES), F32)
    return pl.pallas_call(
        functools.partial(_dispatch_kernel, tm=tm, blk=blk, n_blocks=n_rows // blk),
        grid_spec=pltpu.PrefetchScalarGridSpec(
            num_scalar_prefetch=2,
            grid=(n_tiles,),
            in_specs=[
                pl.BlockSpec((1, 1, 2 * tm), lambda i, ps, pe: (i, 0, 0),
                             memory_space=pltpu.SMEM),
                pl.BlockSpec((tm * TT, LANES), lambda i, ps, pe: (i, 0)),
                pl.BlockSpec((blk * TT, LANES), lambda i, ps, pe: (0, 0)),
            ],
            out_specs=pl.BlockSpec(memory_space=pl.ANY),
            scratch_shapes=[pltpu.SemaphoreType.DMA(()), pltpu.SemaphoreType.DMA(())],
        ),
        out_shape=jax.ShapeDtypeStruct((n_rows * TT, LANES), F32),
        compiler_params=pltpu.CompilerParams(
            dimension_semantics=("arbitrary",), vmem_limit_bytes=VMEM_LIMIT),
        name="moe_dispatch",
    )(pad_starts, pad_ends, dest_flat, x_tt, zeros)


def _expert_kernel(be_ref, nu_ref, xs_ref, g_ref, wg_ref, wu_ref, wd_ref, y_ref,
                   *, blk):
    i = pl.program_id(0)

    @pl.when(i < nu_ref[0])
    def _():
        x = _load_tt(xs_ref, blk)
        h = _rms(x, g_ref[...]).astype(BF16)
        hid = jax.nn.silu(_dot(h, wg_ref[0])) * _dot(h, wu_ref[0])
        _store_tt(y_ref, _dot(hid.astype(BF16), wd_ref[0]))

    @pl.when(i >= nu_ref[0])
    def _():
        y_ref[...] = jnp.zeros_like(y_ref)


def _experts(xs, g, w_gate, w_up, w_down, block_e, n_used, blk):
    n_blocks = xs.shape[0] // (blk * TT)
    last = lambda i, be, nu: jnp.minimum(i, nu[0] - 1)
    return pl.pallas_call(
        functools.partial(_expert_kernel, blk=blk),
        grid_spec=pltpu.PrefetchScalarGridSpec(
            num_scalar_prefetch=2,
            grid=(n_blocks,),
            in_specs=[
                pl.BlockSpec((blk * TT, LANES), lambda i, be, nu: (last(i, be, nu), 0)),
                pl.BlockSpec((1, D_MODEL), lambda i, be, nu: (0, 0)),
                pl.BlockSpec((1, D_MODEL, D_EXPERT), lambda i, be, nu: (be[i], 0, 0)),
                pl.BlockSpec((1, D_MODEL, D_EXPERT), lambda i, be, nu: (be[i], 0, 0)),
                pl.BlockSpec((1, D_EXPERT, D_MODEL), lambda i, be, nu: (be[i], 0, 0)),
            ],
            out_specs=pl.BlockSpec((blk * TT, LANES), lambda i, be, nu: (i, 0)),
        ),
        out_shape=jax.ShapeDtypeStruct(xs.shape, F32),
        compiler_params=pltpu.CompilerParams(
            dimension_semantics=("arbitrary",),
            vmem_limit_bytes=VMEM_LIMIT),
        name="moe_experts",
    )(block_e, n_used, xs, g.reshape(1, D_MODEL), w_gate.astype(BF16),
      w_up.astype(BF16), w_down.astype(BF16))


def _combine_kernel(dest_ref, x_ref, gates_ref, y_hbm, gf_ref, o_ref,
                    ya_ref, yb_ref, sem, *, tm, final):
    def issue(j, _):
        for k, buf in enumerate((ya_ref, yb_ref)):
            d = dest_ref[0, 0, k * tm + j]
            pltpu.make_async_copy(
                y_hbm.at[pl.ds(pl.multiple_of(d * TT, TT), TT), :],
                buf.at[pl.ds(pl.multiple_of(j * TT, TT), TT), :], sem).start()
        return 0

    lax.fori_loop(0, tm, issue, 0)
    gpad = jnp.concatenate(
        [gates_ref[...], jnp.zeros((LANES - 8, tm), F32)], axis=0)
    gcol = gpad.T
    x = _load_tt(x_ref, tm)
    for buf in (ya_ref, yb_ref):
        pltpu.make_async_copy(y_hbm.at[pl.ds(0, tm * TT), :], buf, sem).wait()
    out = (x + gcol[:, 0:1] * _load_tt(ya_ref, tm)
           + gcol[:, 1:2] * _load_tt(yb_ref, tm))
    if final:
        o_ref[...] = _rms(out, gf_ref[...])
    else:
        _store_tt(o_ref, out)


def _combine(x_tt, y, dest_flat, gates, g_final, tm, final):
    n_tok = x_tt.shape[0] // TT
    n_tiles = n_tok // tm
    if final:
        out_spec = pl.BlockSpec((tm, D_MODEL), lambda i: (i, 0))
        out_shape = jax.ShapeDtypeStruct((n_tok, D_MODEL), F32)
    else:
        out_spec = pl.BlockSpec((tm * TT, LANES), lambda i: (i, 0))
        out_shape = jax.ShapeDtypeStruct(x_tt.shape, F32)
    return pl.pallas_call(
        functools.partial(_combine_kernel, tm=tm, final=final),
        grid=(n_tiles,),
        in_specs=[
            pl.BlockSpec((1, 1, 2 * tm), lambda i: (i, 0, 0),
                         memory_space=pltpu.SMEM),
            pl.BlockSpec((tm * TT, LANES), lambda i: (i, 0)),
            pl.BlockSpec((8, tm), lambda i: (0, i)),
            pl.BlockSpec(memory_space=pl.ANY),
            pl.BlockSpec((1, D_MODEL), lambda i: (0, 0)),
        ],
        out_specs=out_spec,
        out_shape=out_shape,
        scratch_shapes=[
            pltpu.VMEM((tm * TT, LANES), F32),
            pltpu.VMEM((tm * TT, LANES), F32),
            pltpu.SemaphoreType.DMA(()),
        ],
        compiler_params=pltpu.CompilerParams(
            dimension_semantics=("arbitrary",),
            vmem_limit_bytes=VMEM_LIMIT),
        name="moe_combine",
    )(dest_flat, x_tt, gates, y, g_final.reshape(1, D_MODEL))


def _moe(x_tt, g, w_rg, b_rg, w_re, b_re, w_gate, w_up, w_down, g_final, final):
    n_tok = x_tt.shape[0] // TT
    tm, td, blk = _tiles(n_tok)
    n_blocks = (2 * n_tok) // blk + N_EXPERTS
    ids, gates, cnt = _router(x_tt, g, w_rg, b_rg, w_re, b_re, tm)

    counts = cnt[:, 0]
    padded = (counts + blk - 1) // blk * blk
    pad_ends = jnp.cumsum(padded).astype(jnp.int32)
    pad_starts = pad_ends - padded
    block_start = jnp.arange(n_blocks, dtype=jnp.int32) * blk
    block_e = jnp.minimum(
        jnp.sum(pad_ends[None, :] <= block_start[:, None], axis=1),
        N_EXPERTS - 1).astype(jnp.int32)
    n_used = (pad_ends[-1:] // blk).astype(jnp.int32)

    dest = _dest(ids, pad_starts, tm)

    def per_tile(t):
        return dest[0:2].reshape(2, n_tok // t, t).transpose(1, 0, 2).reshape(
            n_tok // t, 1, 2 * t)

    xs = _dispatch(x_tt, per_tile(td), pad_starts, pad_ends, n_blocks * blk, td, blk)
    y = _experts(xs, g, w_gate, w_up, w_down, block_e, n_used, blk)
    return _combine(x_tt, y, per_tile(tm), gates, g_final, tm, final)


def kernel(x, norm_mix_g, norm_ffn_g, norm_final_g, ab_w_in, a_ln_g, a_ws, a_ws_b, b_conv_w, ab_w_out, c_w_in, c_conv_w, c_conv_b, c_w_a, c_b_a, c_w_x, c_b_x, c_lambda, c_w_out, moe_w_rg, moe_b_rg, moe_w_re, moe_b_re, moe_w_gate, moe_w_up, moe_w_down):
    bsz, t_len, d = x.shape
    assert d == D_MODEL
    tm, _, _ = _tiles(bsz * t_len)
    assert t_len % tm == 0 and tm % CHUNK == 0

    x_tt = _mixer_ab(x, norm_mix_g[0], ab_w_in[0], a_ln_g[0], a_ws[0], a_ws_b[0],
                     b_conv_w[0], ab_w_out[0], tm)
    x_tt = _moe(x_tt, norm_ffn_g[0], moe_w_rg[0], moe_b_rg[0], moe_w_re[0],
                moe_b_re[0], moe_w_gate[0], moe_w_up[0], moe_w_down[0],
                norm_final_g, final=False)
    x_tt = _mixer_rglru(x_tt, bsz, t_len, norm_mix_g[1], c_w_in[0], c_conv_w[0],
                        c_conv_b[0], c_w_a[0], c_b_a[0], c_w_x[0], c_b_x[0],
                        c_lambda[0], c_w_out[0], tm)
    out = _moe(x_tt, norm_ffn_g[1], moe_w_rg[1], moe_b_rg[1], moe_w_re[1],
               moe_b_re[1], moe_w_gate[1], moe_w_up[1], moe_w_down[1],
               norm_final_g, final=True)
    return out.reshape(bsz, t_len, d)
```

```python
import functools

import jax
import jax.numpy as jnp
from jax import lax
from jax.experimental import pallas as pl
from jax.experimental.pallas import tpu as pltpu

D_MODEL = 1024
LANES = 128
SUBLANES = 8
TT = D_MODEL // LANES
assert TT == SUBLANES

A_HEADS = 4
A_HEAD_DIM = 128
D_A = A_HEADS * A_HEAD_DIM
CHUNK = 128
D_B = D_MODEL - D_A
B_CONV = 3
D_RNN = D_MODEL
LRU_HEADS = 8
LRU_HEAD_DIM = D_RNN // LRU_HEADS
C_CONV = 4
LRU_C = 8.0
N_GROUPS = 4
EXPERTS_PER_GROUP = 8
N_EXPERTS = N_GROUPS * EXPERTS_PER_GROUP
D_EXPERT = 512
EPS = 1e-6

ROUTER_ROWS = 48
VMEM_LIMIT = 56 * 1024 * 1024

BF16 = jnp.bfloat16
F32 = jnp.float32


def _tiles(n_tok):
    tm = 512 if n_tok % 512 == 0 else 256
    td = 1024 if n_tok % 1024 == 0 else tm
    blk = 512 if n_tok >= 8192 else 128
    return tm, td, blk


DMA_UNROLL = 8
STAGE_COLS = 512
DEST_TILE = 4096


def _load_tt(ref, nrows):
    return jnp.concatenate(
        [ref[pl.ds(s, nrows, stride=TT), :] for s in range(TT)], axis=1)


def _store_tt(ref, val):
    nrows = val.shape[0]
    for s in range(TT):
        ref[pl.ds(s, nrows, stride=TT), :] = val[:, s * LANES:(s + 1) * LANES]


def _rms(x, g):
    ms = jnp.mean(x * x, axis=-1, keepdims=True)
    return x * lax.rsqrt(ms + EPS) * g


def _dot(a, b):
    return jnp.dot(a, b, preferred_element_type=F32)


def _shift_rows(cur, prev, k):
    rolled = pltpu.roll(cur, k, axis=0)
    row = lax.broadcasted_iota(jnp.int32, prev.shape, 0)
    head = jnp.where(row < k, pltpu.roll(prev, k, axis=0), rolled[0:SUBLANES, :])
    return jnp.concatenate([head, rolled[SUBLANES:, :]], axis=0)


def _stage_bf16(w_hbm, w_bf, stage, sem):
    cols = w_hbm.shape[1]
    cw = min(cols, stage.shape[1])
    for c0 in range(0, cols, cw):
        cp = pltpu.make_async_copy(w_hbm.at[:, pl.ds(c0, cw)],
                                   stage.at[:, pl.ds(0, cw)], sem)
        cp.start()
        cp.wait()
        w_bf[:, c0:c0 + cw] = stage[:, 0:cw].astype(BF16)


def _gather_combine(dcur_ref, dnext_ref, x_ref, gates_ref, y_hbm, ya_ref, yb_ref,
                    sems, tm):
    i = pl.program_id(0)
    slot = i % 2

    def issue(dref, sl):
        def body(jj, _):
            for u in range(DMA_UNROLL):
                j = jj * DMA_UNROLL + u
                for k, buf in enumerate((ya_ref, yb_ref)):
                    d = dref[0, 0, k * tm + j]
                    pltpu.make_async_copy(
                        y_hbm.at[pl.ds(pl.multiple_of(d * TT, TT), TT), :],
                        buf.at[sl, pl.ds(pl.multiple_of(j * TT, TT), TT), :],
                        sems.at[sl]).start()
            return 0

        lax.fori_loop(0, tm // DMA_UNROLL, body, 0)

    @pl.when(i == 0)
    def _():
        issue(dcur_ref, 0)

    @pl.when(i + 1 < pl.num_programs(0))
    def _():
        issue(dnext_ref, 1 - slot)

    for buf in (ya_ref, yb_ref):
        pltpu.make_async_copy(y_hbm.at[pl.ds(0, tm * TT), :], buf.at[slot],
                              sems.at[slot]).wait()
    gpad = jnp.concatenate(
        [gates_ref[...], jnp.zeros((LANES - 8, tm), F32)], axis=0)
    gcol = gpad.T
    return (_load_tt(x_ref, tm)
            + gcol[:, 0:1] * _load_tt(ya_ref.at[slot], tm)
            + gcol[:, 1:2] * _load_tt(yb_ref.at[slot], tm))


def _gather_combine_specs(n_tiles, tm):
    idx = lambda f: (lambda i, *_: f(i))
    in_specs = [
        pl.BlockSpec((1, 1, 2 * tm), idx(lambda i: (i, 0, 0)), memory_space=pltpu.SMEM),
        pl.BlockSpec((1, 1, 2 * tm),
                     idx(lambda i: (jnp.minimum(i + 1, n_tiles - 1), 0, 0)),
                     memory_space=pltpu.SMEM),
        pl.BlockSpec((tm * TT, LANES), idx(lambda i: (i, 0))),
        pl.BlockSpec((8, tm), idx(lambda i: (0, i))),
        pl.BlockSpec(memory_space=pl.ANY),
    ]
    scratch = [
        pltpu.VMEM((2, tm * TT, LANES), F32),
        pltpu.VMEM((2, tm * TT, LANES), F32),
        pltpu.SemaphoreType.DMA((2,)),
    ]
    return in_specs, scratch


def _mixer_ab_kernel(x_ref, g_ref, win_hbm, lng_ref, ws_ref, wsb_ref, cw_ref,
                     wout_hbm, o_ref, carry_ref, win_ref, wout_ref, stage_ref,
                     wsem, *, tm):
    t = pl.program_id(1)

    @pl.when((pl.program_id(0) == 0) & (t == 0))
    def _():
        _stage_bf16(win_hbm, win_ref, stage_ref, wsem)
        _stage_bf16(wout_hbm, wout_ref, stage_ref, wsem)

    x = x_ref[0]
    h = _rms(x, g_ref[...]).astype(BF16)

    u = jax.nn.gelu(_dot(h, win_ref[:, 0:D_A]))
    v = jax.nn.gelu(_dot(h, win_ref[:, D_A:2 * D_A]))
    mu = jnp.mean(v, axis=-1, keepdims=True)
    vc = v - mu
    var = jnp.mean(vc * vc, axis=-1, keepdims=True)
    vn = (vc * lax.rsqrt(var + EPS) * lng_ref[...]).astype(BF16)

    r_i = lax.broadcasted_iota(jnp.int32, (CHUNK, CHUNK), 0)
    c_i = lax.broadcasted_iota(jnp.int32, (CHUNK, CHUNK), 1)
    causal = r_i >= c_i
    head_cols = []
    for hh in range(A_HEADS):
        wsh = jnp.where(causal, ws_ref[hh], 0.0).astype(BF16)
        rows = []
        for c in range(tm // CHUNK):
            blk = vn[c * CHUNK:(c + 1) * CHUNK,
                     hh * A_HEAD_DIM:(hh + 1) * A_HEAD_DIM]
            rows.append(_dot(wsh, blk))
        head_cols.append(jnp.concatenate(rows, axis=0))
    mixed = jnp.concatenate(head_cols, axis=1) + wsb_ref[...]
    y_a = (u * mixed).astype(BF16)

    gate_b = _dot(h, win_ref[:, 2 * D_A:2 * D_A + D_B])
    gate_c = _dot(h, win_ref[:, 2 * D_A + D_B:2 * D_A + 2 * D_B])
    xb = _dot(h, win_ref[:, 2 * D_A + 2 * D_B:2 * D_A + 3 * D_B])
    cx = gate_c * xb

    @pl.when(t == 0)
    def _():
        carry_ref[...] = jnp.zeros_like(carry_ref)

    prev = carry_ref[...]
    conv = (_shift_rows(cx, prev, 2) * cw_ref[0:1, :]
            + _shift_rows(cx, prev, 1) * cw_ref[1:2, :]
            + cx * cw_ref[2:3, :])
    carry_ref[...] = cx[tm - SUBLANES:tm, :]
    y_b = (gate_b * conv).astype(BF16)

    out = x + _dot(y_a, wout_ref[0:D_A, :]) + _dot(y_b, wout_ref[D_A:D_MODEL, :])
    _store_tt(o_ref, out)


def _mixer_ab(x, g, w_in, ln_g, ws, ws_b, conv_w, w_out, tm):
    bsz, t_len, d = x.shape
    nt = t_len // tm
    wsb_full = jnp.tile(jnp.repeat(ws_b.T, A_HEAD_DIM, axis=1), (tm // CHUNK, 1))
    full = lambda shape: pl.BlockSpec(shape, lambda b, t: (0,) * len(shape))
    hbm = pl.BlockSpec(memory_space=pl.ANY)
    return pl.pallas_call(
        functools.partial(_mixer_ab_kernel, tm=tm),
        grid=(bsz, nt),
        in_specs=[
            pl.BlockSpec((1, tm, d), lambda b, t: (b, t, 0)),
            full((1, d)),
            hbm,
            full((1, D_A)),
            full(ws.shape),
            full((tm, D_A)),
            full(conv_w.shape),
            hbm,
        ],
        out_specs=pl.BlockSpec((tm * TT, LANES), lambda b, t: (b * nt + t, 0)),
        out_shape=jax.ShapeDtypeStruct((bsz * t_len * TT, LANES), F32),
        scratch_shapes=[
            pltpu.VMEM((SUBLANES, D_B), F32),
            pltpu.VMEM(w_in.shape, BF16),
            pltpu.VMEM(w_out.shape, BF16),
            pltpu.VMEM((d, STAGE_COLS), F32),
            pltpu.SemaphoreType.DMA(()),
        ],
        compiler_params=pltpu.CompilerParams(
            dimension_semantics=("arbitrary", "arbitrary"),
            vmem_limit_bytes=VMEM_LIMIT),
        name="mixer_ab",
    )(x, g.reshape(1, d), w_in, ln_g.reshape(1, D_A), ws, wsb_full, conv_w, w_out)


def _mixer_rglru_kernel(dcur_ref, dnext_ref, x_ref, gates_ref, y_hbm,
                        g_ref, win_hbm, cw_ref, cb_ref, wa_hbm, ba_ref,
                        wx_hbm, bx_ref, lam_ref, wout_hbm, o_ref,
                        ya_ref, yb_ref, gsems,
                        carry_ref, hstate_ref, a_scr, b_scr, h_scr,
                        win_ref, wa_ref, wx_ref, wout_ref, stage_ref, wsem,
                        *, tm, nt, seg, pitch):
    t = pl.program_id(0) % nt

    @pl.when(pl.program_id(0) == 0)
    def _():
        _stage_bf16(win_hbm, win_ref, stage_ref, wsem)
        _stage_bf16(wa_hbm, wa_ref, stage_ref, wsem)
        _stage_bf16(wx_hbm, wx_ref, stage_ref, wsem)
        _stage_bf16(wout_hbm, wout_ref, stage_ref, wsem)

    x = _gather_combine(dcur_ref, dnext_ref, x_ref, gates_ref, y_hbm,
                        ya_ref, yb_ref, gsems, tm)
    h = _rms(x, g_ref[...]).astype(BF16)
    gate = jax.nn.gelu(_dot(h, win_ref[:, 0:D_RNN]))
    xr0 = _dot(h, win_ref[:, D_RNN:2 * D_RNN])

    @pl.when(t == 0)
    def _():
        carry_ref[...] = jnp.zeros_like(carry_ref)
        hstate_ref[...] = jnp.zeros_like(hstate_ref)

    prev = carry_ref[...]
    xr = (_shift_rows(xr0, prev, 3) * cw_ref[0:1, :]
          + _shift_rows(xr0, prev, 2) * cw_ref[1:2, :]
          + _shift_rows(xr0, prev, 1) * cw_ref[2:3, :]
          + xr0 * cw_ref[3:4, :]) + cb_ref[...]
    carry_ref[...] = xr0[tm - SUBLANES:tm, :]

    xrb = xr.astype(BF16)
    r_cols, i_cols = [], []
    for hh in range(LRU_HEADS):
        blk = xrb[:, hh * LRU_HEAD_DIM:(hh + 1) * LRU_HEAD_DIM]
        rows = slice(hh * LRU_HEAD_DIM, (hh + 1) * LRU_HEAD_DIM)
        r_cols.append(_dot(blk, wa_ref[rows, :]))
        i_cols.append(_dot(blk, wx_ref[rows, :]))
    r = jax.nn.sigmoid(jnp.concatenate(r_cols, axis=1) + ba_ref[...])
    i = jax.nn.sigmoid(jnp.concatenate(i_cols, axis=1) + bx_ref[...])

    lam = lam_ref[...]
    log_sig = jnp.minimum(lam, 0.0) - jnp.log1p(jnp.exp(-jnp.abs(lam)))
    log_a = (LRU_C * r) * log_sig
    a = jnp.exp(log_a)
    b = jnp.sqrt(-jnp.tanh(log_a) * (a * a + 1.0)) * (i * xr)

    n_lg = D_RNN // LANES

    def put(scr, val):
        for c in range(n_lg):
            for s in range(SUBLANES):
                scr[c, s * pitch:s * pitch + seg, :] = (
                    val[s * seg:(s + 1) * seg, c * LANES:(c + 1) * LANES])

    def ld(scr, j):
        return jnp.concatenate(
            [scr[c, pl.ds(j, SUBLANES, stride=pitch), :] for c in range(n_lg)],
            axis=1)

    def st(scr, j, val):
        for c in range(n_lg):
            scr[c, pl.ds(j, SUBLANES, stride=pitch), :] = (
                val[:, c * LANES:(c + 1) * LANES])

    put(a_scr, a)
    put(b_scr, b)

    def seg_step(j, carry):
        hl, pl_ = carry
        aj = ld(a_scr, j)
        hl = aj * hl + ld(b_scr, j)
        pl_ = aj * pl_
        st(h_scr, j, hl)
        st(a_scr, j, pl_)
        return hl, pl_

    zeros = jnp.zeros((SUBLANES, D_RNN), F32)
    h_end, p_end = lax.fori_loop(0, seg, seg_step, (zeros, zeros + 1.0))

    def seg_rows(scr, s):
        return jnp.concatenate(
            [scr[c, s * pitch:s * pitch + seg, :] for c in range(n_lg)], axis=1)

    c = hstate_ref[...]
    segs = []
    for s in range(SUBLANES):
        segs.append(seg_rows(h_scr, s) + seg_rows(a_scr, s) * c)
        c = h_end[s:s + 1, :] + p_end[s:s + 1, :] * c
    hstate_ref[...] = c
    hseq = jnp.concatenate(segs, axis=0)
    out = x + _dot((gate * hseq).astype(BF16), wout_ref[...])
    _store_tt(o_ref, out)


def _mixer_rglru(x_tt, moe_out, bsz, t_len, g, w_in, conv_w, conv_b, w_a, b_a,
                 w_x, b_x, lam, w_out, tm):
    y, dest_tiles, gates = moe_out
    nt = t_len // tm
    n_tiles = bsz * nt
    seg = tm // SUBLANES
    pitch = seg + SUBLANES
    d = D_MODEL
    full = lambda shape: pl.BlockSpec(shape, lambda i: (0,) * len(shape))
    hbm = pl.BlockSpec(memory_space=pl.ANY)
    row = lambda v: v.reshape(1, -1)
    gc_specs, gc_scratch = _gather_combine_specs(n_tiles, tm)
    return pl.pallas_call(
        functools.partial(_mixer_rglru_kernel, tm=tm, nt=nt, seg=seg, pitch=pitch),
        grid=(n_tiles,),
        in_specs=gc_specs + [
            full((1, d)),
            hbm,
            full(conv_w.shape),
            full((1, D_RNN)),
            hbm,
            full((1, D_RNN)),
            hbm,
            full((1, D_RNN)),
            full((1, D_RNN)),
            hbm,
        ],
        out_specs=pl.BlockSpec((tm * TT, LANES), lambda i: (i, 0)),
        out_shape=jax.ShapeDtypeStruct(x_tt.shape, F32),
        scratch_shapes=gc_scratch + [
            pltpu.VMEM((SUBLANES, D_RNN), F32),
            pltpu.VMEM((1, D_RNN), F32),
            pltpu.VMEM((D_RNN // LANES, SUBLANES * pitch, LANES), F32),
            pltpu.VMEM((D_RNN // LANES, SUBLANES * pitch, LANES), F32),
            pltpu.VMEM((D_RNN // LANES, SUBLANES * pitch, LANES), F32),
            pltpu.VMEM((d, 2 * D_RNN), BF16),
            pltpu.VMEM((D_RNN, LRU_HEAD_DIM), BF16),
            pltpu.VMEM((D_RNN, LRU_HEAD_DIM), BF16),
            pltpu.VMEM((D_RNN, d), BF16),
            pltpu.VMEM((d, STAGE_COLS), F32),
            pltpu.SemaphoreType.DMA(()),
        ],
        compiler_params=pltpu.CompilerParams(
            dimension_semantics=("arbitrary",),
            vmem_limit_bytes=VMEM_LIMIT),
        name="mixer_rglru",
    )(dest_tiles, dest_tiles, x_tt, gates, y, row(g), w_in, conv_w, row(conv_b),
      w_a.reshape(D_RNN, LRU_HEAD_DIM), row(b_a), w_x.reshape(D_RNN, LRU_HEAD_DIM),
      row(b_x), row(lam), w_out)


def _router_kernel(x_ref, g_ref, wr_ref, br_ref, ids_ref, gates_ref, cnt_ref,
                   carry_ref, *, tm):
    i = pl.program_id(0)
    x = _load_tt(x_ref, tm)
    h = _rms(x, g_ref[...]).astype(BF16)
    lt = lax.dot_general(wr_ref[...], h, (((1,), (1,)), ((), ())),
                         preferred_element_type=F32) + br_ref[...]
    gl = lt[0:N_GROUPS, :]
    gmax = jnp.max(gl, axis=0, keepdims=True)
    g_iota = lax.broadcasted_iota(jnp.int32, gl.shape, 0)
    g_idx = jnp.min(jnp.where(gl == gmax, g_iota, N_GROUPS), axis=0, keepdims=True)
    g_prob = 1.0 / jnp.sum(jnp.exp(gl - gmax), axis=0, keepdims=True)

    esel = lt[8:8 + EXPERTS_PER_GROUP, :]
    for gidx in range(1, N_GROUPS):
        lo = 8 + gidx * EXPERTS_PER_GROUP
        esel = jnp.where(g_idx == gidx, lt[lo:lo + EXPERTS_PER_GROUP, :], esel)
    e_iota = lax.broadcasted_iota(jnp.int32, esel.shape, 0)
    top1 = jnp.max(esel, axis=0, keepdims=True)
    i1 = jnp.min(jnp.where(esel == top1, e_iota, EXPERTS_PER_GROUP), axis=0, keepdims=True)
    rest = jnp.where(e_iota == i1, -jnp.inf, esel)
    top2 = jnp.max(rest, axis=0, keepdims=True)
    i2 = jnp.min(jnp.where(rest == top2, e_iota, EXPERTS_PER_GROUP), axis=0, keepdims=True)
    e2 = jnp.exp(top2 - top1)
    denom = 1.0 + e2
    gate0 = g_prob * (1.0 / denom)
    gate1 = g_prob * (e2 / denom)
    e0 = g_idx * EXPERTS_PER_GROUP + i1
    e1 = g_idx * EXPERTS_PER_GROUP + i2

    x_iota = lax.broadcasted_iota(jnp.int32, (N_EXPERTS, tm), 0)
    hit0 = x_iota == e0
    hit1 = x_iota == e1
    onehot = jnp.where(hit0 | hit1, 1.0, 0.0)
    s_i = lax.broadcasted_iota(jnp.int32, (tm, tm), 0)
    t_i = lax.broadcasted_iota(jnp.int32, (tm, tm), 1)
    before = jnp.where(s_i < t_i, 1.0, 0.0).astype(BF16)

    @pl.when(i == 0)
    def _():
        carry_ref[...] = jnp.zeros_like(carry_ref)

    prefix = _dot(onehot.astype(BF16), before) + carry_ref[:, 0:1]
    rank0 = jnp.sum(jnp.where(hit0, prefix, 0.0), axis=0, keepdims=True)
    rank1 = jnp.sum(jnp.where(hit1, prefix, 0.0), axis=0, keepdims=True)
    carry_ref[...] = carry_ref[...] + jnp.sum(onehot, axis=1, keepdims=True)

    ids_ref[0:1, :] = e0
    ids_ref[1:2, :] = e1
    ids_ref[2:3, :] = rank0.astype(jnp.int32)
    ids_ref[3:4, :] = rank1.astype(jnp.int32)
    ids_ref[4:8, :] = jnp.zeros((4, tm), jnp.int32)
    gates_ref[0:1, :] = gate0
    gates_ref[1:2, :] = gate1
    gates_ref[2:8, :] = jnp.zeros((6, tm), F32)
    cnt_ref[...] = carry_ref[...].astype(jnp.int32)


def _router(x_tt, g, w_rg, b_rg, w_re, b_re, tm):
    n_tok = x_tt.shape[0] // TT
    n_tiles = n_tok // tm
    wr = jnp.zeros((ROUTER_ROWS, D_MODEL), F32)
    wr = wr.at[0:N_GROUPS].set(w_rg.T).at[8:8 + N_EXPERTS].set(w_re.T)
    br = jnp.zeros((ROUTER_ROWS, 1), F32)
    br = br.at[0:N_GROUPS, 0].set(b_rg).at[8:8 + N_EXPERTS, 0].set(b_re)
    full = lambda shape: pl.BlockSpec(shape, lambda i: (0,) * len(shape))
    return pl.pallas_call(
        functools.partial(_router_kernel, tm=tm),
        grid=(n_tiles,),
        in_specs=[
            pl.BlockSpec((tm * TT, LANES), lambda i: (i, 0)),
            full((1, D_MODEL)),
            full((ROUTER_ROWS, D_MODEL)),
            full((ROUTER_ROWS, 1)),
        ],
        out_specs=[
            pl.BlockSpec((8, tm), lambda i: (0, i)),
            pl.BlockSpec((8, tm), lambda i: (0, i)),
            full((N_EXPERTS, LANES)),
        ],
        out_shape=[
            jax.ShapeDtypeStruct((8, n_tok), jnp.int32),
            jax.ShapeDtypeStruct((8, n_tok), F32),
            jax.ShapeDtypeStruct((N_EXPERTS, LANES), jnp.int32),
        ],
        scratch_shapes=[pltpu.VMEM((N_EXPERTS, LANES), F32)],
        compiler_params=pltpu.CompilerParams(
            dimension_semantics=("arbitrary",),
            vmem_limit_bytes=VMEM_LIMIT),
        name="moe_router",
    )(x_tt, g.reshape(1, D_MODEL), wr.astype(BF16), br)


def _dest_kernel(ids_ref, pstart_ref, dest_ref):
    ids = ids_ref[...]
    x_iota = lax.broadcasted_iota(jnp.int32, (N_EXPERTS, ids.shape[1]), 0)
    pstart = pstart_ref[:, 0:1]
    for k in range(2):
        hit = x_iota == ids[k:k + 1, :]
        base = jnp.sum(jnp.where(hit, pstart, 0), axis=0, keepdims=True)
        dest_ref[k:k + 1, :] = base + ids[2 + k:3 + k, :]
    dest_ref[2:8, :] = jnp.zeros((6, ids.shape[1]), jnp.int32)


def _dest(ids, pad_starts, tm):
    n_tok = ids.shape[1]
    pstart = jnp.broadcast_to(pad_starts[:, None], (N_EXPERTS, LANES))
    return pl.pallas_call(
        _dest_kernel,
        grid=(n_tok // tm,),
        in_specs=[
            pl.BlockSpec((8, tm), lambda i: (0, i)),
            pl.BlockSpec((N_EXPERTS, LANES), lambda i: (0, 0)),
        ],
        out_specs=pl.BlockSpec((8, tm), lambda i: (0, i)),
        out_shape=jax.ShapeDtypeStruct((8, n_tok), jnp.int32),
        name="moe_dest",
    )(ids, pstart)


def _dispatch_kernel(pstart_ref, pend_ref, dest_ref, x_ref, zero_ref, xs_hbm,
                     sem, zsem, *, tm, blk, n_blocks):
    i = pl.program_id(0)

    @pl.when(i == 0)
    def _():
        def zcopy(e):
            start = pl.multiple_of((pend_ref[e] - blk) * TT, blk * TT)
            return pltpu.make_async_copy(
                zero_ref, xs_hbm.at[pl.ds(start, blk * TT), :], zsem)

        def zstart(e, _):
            @pl.when(pend_ref[e] > pstart_ref[e])
            def _():
                zcopy(e).start()
            return 0

        def zwait(e, _):
            @pl.when(pend_ref[e] > pstart_ref[e])
            def _():
                zcopy(e).wait()
            return 0

        def tcopy(b):
            return pltpu.make_async_copy(
                zero_ref,
                xs_hbm.at[pl.ds(pl.multiple_of(b * (blk * TT), blk * TT), blk * TT), :],
                zsem)

        def tstart(b, _):
            tcopy(b).start()
            return 0

        def twait(b, _):
            tcopy(b).wait()
            return 0

        n_used = pend_ref[N_EXPERTS - 1] // blk
        lax.fori_loop(0, N_EXPERTS, zstart, 0)
        lax.fori_loop(n_used, n_blocks, tstart, 0)
        lax.fori_loop(0, N_EXPERTS, zwait, 0)
        lax.fori_loop(n_used, n_blocks, twait, 0)

    def issue(jj, _):
        for u in range(DMA_UNROLL):
            j = jj * DMA_UNROLL + u
            src = x_ref.at[pl.ds(pl.multiple_of(j * TT, TT), TT), :]
            for k in range(2):
                d = dest_ref[0, 0, k * tm + j]
                pltpu.make_async_copy(
                    src, xs_hbm.at[pl.ds(pl.multiple_of(d * TT, TT), TT), :],
                    sem).start()
        return 0

    lax.fori_loop(0, tm // DMA_UNROLL, issue, 0)
    for _ in range(2):
        pltpu.make_async_copy(x_ref, xs_hbm.at[pl.ds(0, tm * TT), :], sem).wait()


def _dispatch(x_tt, dest_flat, pad_starts, pad_ends, n_rows, tm, blk):
    n_tok = x_tt.shape[0] // TT
    n_tiles = n_tok // tm
    zeros = jnp.zeros((blk * TT, LANES), F32)
    return pl.pallas_call(
        functools.partial(_dispatch_kernel, tm=tm, blk=blk, n_blocks=n_rows // blk),
        grid_spec=pltpu.PrefetchScalarGridSpec(
            num_scalar_prefetch=2,
            grid=(n_tiles,),
            in_specs=[
                pl.BlockSpec((1, 1, 2 * tm), lambda i, ps, pe: (i, 0, 0),
                             memory_space=pltpu.SMEM),
                pl.BlockSpec((tm * TT, LANES), lambda i, ps, pe: (i, 0)),
                pl.BlockSpec((blk * TT, LANES), lambda i, ps, pe: (0, 0)),
            ],
            out_specs=pl.BlockSpec(memory_space=pl.ANY),
            scratch_shapes=[pltpu.SemaphoreType.DMA(()), pltpu.SemaphoreType.DMA(())],
        ),
        out_shape=jax.ShapeDtypeStruct((n_rows * TT, LANES), F32),
        compiler_params=pltpu.CompilerParams(
            dimension_semantics=("arbitrary",), vmem_limit_bytes=VMEM_LIMIT),
        name="moe_dispatch",
    )(pad_starts, pad_ends, dest_flat, x_tt, zeros)


def _expert_kernel(be_ref, nu_ref, xs_ref, g_ref, wg_ref, wu_ref, wd_ref, y_ref,
                   wg_bf, wu_bf, wd_bf, *, blk):
    i = pl.program_id(0)

    @pl.when((i == 0) | (be_ref[i] != be_ref[jnp.maximum(i - 1, 0)]))
    def _():
        wg_bf[...] = wg_ref[0, 0].astype(BF16)
        wu_bf[...] = wu_ref[0, 0].astype(BF16)
        wd_bf[...] = wd_ref[0, 0].astype(BF16)

    @pl.when(i < nu_ref[0])
    def _():
        x = _load_tt(xs_ref, blk)
        h = _rms(x, g_ref[...]).astype(BF16)
        hid = jax.nn.silu(_dot(h, wg_bf[...])) * _dot(h, wu_bf[...])
        _store_tt(y_ref, _dot(hid.astype(BF16), wd_bf[...]))

    @pl.when(i >= nu_ref[0])
    def _():
        y_ref[...] = jnp.zeros_like(y_ref)


def _experts(xs, g, w_gate, w_up, w_down, layer, block_e, n_used, blk):
    n_blocks = xs.shape[0] // (blk * TT)
    last = lambda i, be, nu: jnp.minimum(i, nu[0] - 1)
    w_idx = lambda i, be, nu: (layer, be[i], 0, 0)
    return pl.pallas_call(
        functools.partial(_expert_kernel, blk=blk),
        grid_spec=pltpu.PrefetchScalarGridSpec(
            num_scalar_prefetch=2,
            grid=(n_blocks,),
            in_specs=[
                pl.BlockSpec((blk * TT, LANES), lambda i, be, nu: (last(i, be, nu), 0)),
                pl.BlockSpec((1, D_MODEL), lambda i, be, nu: (0, 0)),
                pl.BlockSpec((1, 1, D_MODEL, D_EXPERT), w_idx),
                pl.BlockSpec((1, 1, D_MODEL, D_EXPERT), w_idx),
                pl.BlockSpec((1, 1, D_EXPERT, D_MODEL), w_idx),
            ],
            out_specs=pl.BlockSpec((blk * TT, LANES), lambda i, be, nu: (i, 0)),
            scratch_shapes=[
                pltpu.VMEM((D_MODEL, D_EXPERT), BF16),
                pltpu.VMEM((D_MODEL, D_EXPERT), BF16),
                pltpu.VMEM((D_EXPERT, D_MODEL), BF16),
            ],
        ),
        out_shape=jax.ShapeDtypeStruct(xs.shape, F32),
        compiler_params=pltpu.CompilerParams(
            dimension_semantics=("arbitrary",),
            vmem_limit_bytes=VMEM_LIMIT),
        name="moe_experts",
    )(block_e, n_used, xs, g.reshape(1, D_MODEL), w_gate, w_up, w_down)


def _final_kernel(dcur_ref, dnext_ref, x_ref, gates_ref, y_hbm, gf_ref, o_ref,
                  ya_ref, yb_ref, gsems, *, tm):
    out = _gather_combine(dcur_ref, dnext_ref, x_ref, gates_ref, y_hbm,
                          ya_ref, yb_ref, gsems, tm)
    o_ref[...] = _rms(out, gf_ref[...])


def _final(x_tt, moe_out, g_final, tm):
    y, dest_tiles, gates = moe_out
    n_tok = x_tt.shape[0] // TT
    n_tiles = n_tok // tm
    gc_specs, gc_scratch = _gather_combine_specs(n_tiles, tm)
    return pl.pallas_call(
        functools.partial(_final_kernel, tm=tm),
        grid=(n_tiles,),
        in_specs=gc_specs + [pl.BlockSpec((1, D_MODEL), lambda i: (0, 0))],
        out_specs=pl.BlockSpec((tm, D_MODEL), lambda i: (i, 0)),
        out_shape=jax.ShapeDtypeStruct((n_tok, D_MODEL), F32),
        scratch_shapes=gc_scratch,
        compiler_params=pltpu.CompilerParams(
            dimension_semantics=("arbitrary",),
            vmem_limit_bytes=VMEM_LIMIT),
        name="moe_combine_final",
    )(dest_tiles, dest_tiles, x_tt, gates, y, g_final.reshape(1, D_MODEL))


def _moe(x_tt, g, w_rg, b_rg, w_re, b_re, w_gate, w_up, w_down, layer):
    n_tok = x_tt.shape[0] // TT
    tm, td, blk = _tiles(n_tok)
    n_blocks = (2 * n_tok) // blk + N_EXPERTS
    ids, gates, cnt = _router(x_tt, g, w_rg, b_rg, w_re, b_re, tm)

    counts = cnt[:, 0]
    padded = (counts + blk - 1) // blk * blk
    pad_ends = jnp.cumsum(padded).astype(jnp.int32)
    pad_starts = pad_ends - padded
    block_start = jnp.arange(n_blocks, dtype=jnp.int32) * blk
    block_e = jnp.minimum(
        jnp.sum(pad_ends[None, :] <= block_start[:, None], axis=1),
        N_EXPERTS - 1).astype(jnp.int32)
    n_used = (pad_ends[-1:] // blk).astype(jnp.int32)

    dest = _dest(ids, pad_starts, DEST_TILE if n_tok % DEST_TILE == 0 else tm)

    def per_tile(t):
        return dest[0:2].reshape(2, n_tok // t, t).transpose(1, 0, 2).reshape(
            n_tok // t, 1, 2 * t)

    xs = _dispatch(x_tt, per_tile(td), pad_starts, pad_ends, n_blocks * blk, td, blk)
    y = _experts(xs, g, w_gate, w_up, w_down, layer, block_e, n_used, blk)
    return y, per_tile(tm), gates


def kernel(x, norm_mix_g, norm_ffn_g, norm_final_g, ab_w_in, a_ln_g, a_ws, a_ws_b, b_conv_w, ab_w_out, c_w_in, c_conv_w, c_conv_b, c_w_a, c_b_a, c_w_x, c_b_x, c_lambda, c_w_out, moe_w_rg, moe_b_rg, moe_w_re, moe_b_re, moe_w_gate, moe_w_up, moe_w_down):
    bsz, t_len, d = x.shape
    assert d == D_MODEL
    tm, _, _ = _tiles(bsz * t_len)
    assert t_len % tm == 0 and tm % CHUNK == 0

    def moe(x_tt, layer):
        return _moe(x_tt, norm_ffn_g[layer], moe_w_rg[layer], moe_b_rg[layer],
                    moe_w_re[layer], moe_b_re[layer], moe_w_gate, moe_w_up,
                    moe_w_down, layer)

    x1 = _mixer_ab(x, norm_mix_g[0], ab_w_in[0], a_ln_g[0], a_ws[0], a_ws_b[0],
                   b_conv_w[0], ab_w_out[0], tm)
    x3 = _mixer_rglru(x1, moe(x1, 0), bsz, t_len, norm_mix_g[1], c_w_in[0],
                      c_conv_w[0], c_conv_b[0], c_w_a[0], c_b_a[0], c_w_x[0],
                      c_b_x[0], c_lambda[0], c_w_out[0], tm)
    out = _final(x3, moe(x3, 1), norm_final_g, tm)
    return out.reshape(bsz, t_len, d)
```

```python
import functools

import numpy as np
import jax
import jax.numpy as jnp
from jax import lax
from jax.experimental import pallas as pl
from jax.experimental.pallas import tpu as pltpu

D_MODEL = 1024
LANES = 128
SUBLANES = 8
TT = D_MODEL // LANES
assert TT == SUBLANES

A_HEADS = 4
A_HEAD_DIM = 128
D_A = A_HEADS * A_HEAD_DIM
CHUNK = 128
D_B = D_MODEL - D_A
B_CONV = 3
D_RNN = D_MODEL
LRU_HEADS = 8
LRU_HEAD_DIM = D_RNN // LRU_HEADS
C_CONV = 4
LRU_C = 8.0
N_GROUPS = 4
EXPERTS_PER_GROUP = 8
N_EXPERTS = N_GROUPS * EXPERTS_PER_GROUP
D_EXPERT = 512
EPS = 1e-6

N_PAIRS = EXPERTS_PER_GROUP * (EXPERTS_PER_GROUP - 1) // 2
N_CLASSES = N_GROUPS * N_PAIRS
CLS_PAD = 128
assert N_CLASSES <= CLS_PAD
GROUP_LANE = 0
EXPERT_LANE = 8
ROUTER_ROWS = 48
VMEM_LIMIT = 56 * 1024 * 1024

BF16 = jnp.bfloat16
F32 = jnp.float32

_PAIRS = [(lo, hi) for lo in range(EXPERTS_PER_GROUP)
          for hi in range(lo + 1, EXPERTS_PER_GROUP)]
CLASS_LO = np.array([g * EXPERTS_PER_GROUP + lo
                     for g in range(N_GROUPS) for lo, _ in _PAIRS], np.int32)
CLASS_HI = np.array([g * EXPERTS_PER_GROUP + hi
                     for g in range(N_GROUPS) for _, hi in _PAIRS], np.int32)


def _tiles(n_tok):
    tm = 512 if n_tok % 512 == 0 else 256
    td = 1024 if n_tok % 1024 == 0 else tm
    blk = 256 if n_tok >= 8192 else 128
    return tm, td, blk


DMA_UNROLL = 16
STAGE_COLS = 512
DEST_TILE = 4096


def _load_tt(ref, nrows):
    return jnp.concatenate(
        [ref[pl.ds(s, nrows, stride=TT), :] for s in range(TT)], axis=1)


def _store_tt(ref, val):
    nrows = val.shape[0]
    for s in range(TT):
        ref[pl.ds(s, nrows, stride=TT), :] = val[:, s * LANES:(s + 1) * LANES]


def _rms(x, g):
    ms = jnp.mean(x * x, axis=-1, keepdims=True)
    return x * lax.rsqrt(ms + EPS) * g


def _dot(a, b):
    return jnp.dot(a, b, preferred_element_type=F32)


def _shift_rows(cur, prev, k):
    rolled = pltpu.roll(cur, k, axis=0)
    row = lax.broadcasted_iota(jnp.int32, prev.shape, 0)
    head = jnp.where(row < k, pltpu.roll(prev, k, axis=0), rolled[0:SUBLANES, :])
    return jnp.concatenate([head, rolled[SUBLANES:, :]], axis=0)


def _stage_bf16(w_hbm, w_bf, stage, sem):
    cols = w_hbm.shape[1]
    cw = min(cols, stage.shape[1])
    for c0 in range(0, cols, cw):
        cp = pltpu.make_async_copy(w_hbm.at[:, pl.ds(c0, cw)],
                                   stage.at[:, pl.ds(0, cw)], sem)
        cp.start()
        cp.wait()
        w_bf[:, c0:c0 + cw] = stage[:, 0:cw].astype(BF16)


def _gather_combine(dcur_ref, dnext_ref, x_ref, y_hbm, yg_ref, sems, tm):
    i = pl.program_id(0)
    slot = i % 2

    def issue(dref, sl):
        def body(jj, _):
            for u in range(DMA_UNROLL):
                j = jj * DMA_UNROLL + u
                d = dref[0, 0, j]
                pltpu.make_async_copy(
                    y_hbm.at[pl.ds(pl.multiple_of(d * TT, TT), TT), :],
                    yg_ref.at[sl, pl.ds(pl.multiple_of(j * TT, TT), TT), :],
                    sems.at[sl]).start()
            return 0

        lax.fori_loop(0, tm // DMA_UNROLL, body, 0)

    @pl.when(i == 0)
    def _():
        issue(dcur_ref, 0)

    @pl.when(i + 1 < pl.num_programs(0))
    def _():
        issue(dnext_ref, 1 - slot)

    pltpu.make_async_copy(y_hbm.at[pl.ds(0, tm * TT), :], yg_ref.at[slot],
                          sems.at[slot]).wait()
    return _load_tt(x_ref, tm) + _load_tt(yg_ref.at[slot], tm)


def _gather_combine_specs(n_tiles, tm):
    in_specs = [
        pl.BlockSpec((1, 1, tm), lambda i: (i, 0, 0), memory_space=pltpu.SMEM),
        pl.BlockSpec((1, 1, tm), lambda i: (jnp.minimum(i + 1, n_tiles - 1), 0, 0),
                     memory_space=pltpu.SMEM),
        pl.BlockSpec((tm * TT, LANES), lambda i: (i, 0)),
        pl.BlockSpec(memory_space=pl.ANY),
    ]
    scratch = [
        pltpu.VMEM((2, tm * TT, LANES), F32),
        pltpu.SemaphoreType.DMA((2,)),
    ]
    return in_specs, scratch


def _mixer_ab_kernel(x_ref, g_ref, win_hbm, lng_ref, ws_ref, wsb_ref, cw_ref,
                     wout_hbm, o_ref, carry_ref, win_ref, wout_ref, stage_ref,
                     wsem, *, tm):
    t = pl.program_id(1)

    @pl.when((pl.program_id(0) == 0) & (t == 0))
    def _():
        _stage_bf16(win_hbm, win_ref, stage_ref, wsem)
        _stage_bf16(wout_hbm, wout_ref, stage_ref, wsem)

    x = x_ref[0]
    h = _rms(x, g_ref[...]).astype(BF16)

    u = jax.nn.gelu(_dot(h, win_ref[:, 0:D_A]))
    v = jax.nn.gelu(_dot(h, win_ref[:, D_A:2 * D_A]))
    mu = jnp.mean(v, axis=-1, keepdims=True)
    vc = v - mu
    var = jnp.mean(vc * vc, axis=-1, keepdims=True)
    vn = (vc * lax.rsqrt(var + EPS) * lng_ref[...]).astype(BF16)

    r_i = lax.broadcasted_iota(jnp.int32, (CHUNK, CHUNK), 0)
    c_i = lax.broadcasted_iota(jnp.int32, (CHUNK, CHUNK), 1)
    causal = r_i >= c_i
    head_cols = []
    for hh in range(A_HEADS):
        wsh = jnp.where(causal, ws_ref[hh], 0.0).astype(BF16)
        rows = []
        for c in range(tm // CHUNK):
            blk = vn[c * CHUNK:(c + 1) * CHUNK,
                     hh * A_HEAD_DIM:(hh + 1) * A_HEAD_DIM]
            rows.append(_dot(wsh, blk))
        head_cols.append(jnp.concatenate(rows, axis=0))
    mixed = jnp.concatenate(head_cols, axis=1) + wsb_ref[...]
    y_a = (u * mixed).astype(BF16)

    gate_b = _dot(h, win_ref[:, 2 * D_A:2 * D_A + D_B])
    gate_c = _dot(h, win_ref[:, 2 * D_A + D_B:2 * D_A + 2 * D_B])
    xb = _dot(h, win_ref[:, 2 * D_A + 2 * D_B:2 * D_A + 3 * D_B])
    cx = gate_c * xb

    @pl.when(t == 0)
    def _():
        carry_ref[...] = jnp.zeros_like(carry_ref)

    prev = carry_ref[...]
    conv = (_shift_rows(cx, prev, 2) * cw_ref[0:1, :]
            + _shift_rows(cx, prev, 1) * cw_ref[1:2, :]
            + cx * cw_ref[2:3, :])
    carry_ref[...] = cx[tm - SUBLANES:tm, :]
    y_b = (gate_b * conv).astype(BF16)

    out = x + _dot(y_a, wout_ref[0:D_A, :]) + _dot(y_b, wout_ref[D_A:D_MODEL, :])
    _store_tt(o_ref, out)


def _mixer_ab(x, g, w_in, ln_g, ws, ws_b, conv_w, w_out, tm):
    bsz, t_len, d = x.shape
    nt = t_len // tm
    wsb_full = jnp.tile(jnp.repeat(ws_b.T, A_HEAD_DIM, axis=1), (tm // CHUNK, 1))
    full = lambda shape: pl.BlockSpec(shape, lambda b, t: (0,) * len(shape))
    hbm = pl.BlockSpec(memory_space=pl.ANY)
    return pl.pallas_call(
        functools.partial(_mixer_ab_kernel, tm=tm),
        grid=(bsz, nt),
        in_specs=[
            pl.BlockSpec((1, tm, d), lambda b, t: (b, t, 0)),
            full((1, d)),
            hbm,
            full((1, D_A)),
            full(ws.shape),
            full((tm, D_A)),
            full(conv_w.shape),
            hbm,
        ],
        out_specs=pl.BlockSpec((tm * TT, LANES), lambda b, t: (b * nt + t, 0)),
        out_shape=jax.ShapeDtypeStruct((bsz * t_len * TT, LANES), F32),
        scratch_shapes=[
            pltpu.VMEM((SUBLANES, D_B), F32),
            pltpu.VMEM(w_in.shape, BF16),
            pltpu.VMEM(w_out.shape, BF16),
            pltpu.VMEM((d, STAGE_COLS), F32),
            pltpu.SemaphoreType.DMA(()),
        ],
        compiler_params=pltpu.CompilerParams(
            dimension_semantics=("arbitrary", "arbitrary"),
            vmem_limit_bytes=VMEM_LIMIT),
        name="mixer_ab",
    )(x, g.reshape(1, d), w_in, ln_g.reshape(1, D_A), ws, wsb_full, conv_w, w_out)


def _mixer_rglru_kernel(dcur_ref, dnext_ref, x_ref, y_hbm,
                        g_ref, win_hbm, cw_ref, cb_ref, wa_hbm, ba_ref,
                        wx_hbm, bx_ref, lam_ref, wout_hbm, o_ref,
                        yg_ref, gsems,
                        carry_ref, hstate_ref, a_scr, b_scr, h_scr,
                        win_ref, wa_ref, wx_ref, wout_ref, stage_ref, wsem,
                        *, tm, nt, seg, pitch):
    t = pl.program_id(0) % nt

    @pl.when(pl.program_id(0) == 0)
    def _():
        _stage_bf16(win_hbm, win_ref, stage_ref, wsem)
        _stage_bf16(wa_hbm, wa_ref, stage_ref, wsem)
        _stage_bf16(wx_hbm, wx_ref, stage_ref, wsem)
        _stage_bf16(wout_hbm, wout_ref, stage_ref, wsem)

    x = _gather_combine(dcur_ref, dnext_ref, x_ref, y_hbm, yg_ref, gsems, tm)
    h = _rms(x, g_ref[...]).astype(BF16)
    gate = jax.nn.gelu(_dot(h, win_ref[:, 0:D_RNN]))
    xr0 = _dot(h, win_ref[:, D_RNN:2 * D_RNN])

    @pl.when(t == 0)
    def _():
        carry_ref[...] = jnp.zeros_like(carry_ref)
        hstate_ref[...] = jnp.zeros_like(hstate_ref)

    prev = carry_ref[...]
    xr = (_shift_rows(xr0, prev, 3) * cw_ref[0:1, :]
          + _shift_rows(xr0, prev, 2) * cw_ref[1:2, :]
          + _shift_rows(xr0, prev, 1) * cw_ref[2:3, :]
          + xr0 * cw_ref[3:4, :]) + cb_ref[...]
    carry_ref[...] = xr0[tm - SUBLANES:tm, :]

    xrb = xr.astype(BF16)
    r_cols, i_cols = [], []
    for hh in range(LRU_HEADS):
        blk = xrb[:, hh * LRU_HEAD_DIM:(hh + 1) * LRU_HEAD_DIM]
        rows = slice(hh * LRU_HEAD_DIM, (hh + 1) * LRU_HEAD_DIM)
        r_cols.append(_dot(blk, wa_ref[rows, :]))
        i_cols.append(_dot(blk, wx_ref[rows, :]))
    r = jax.nn.sigmoid(jnp.concatenate(r_cols, axis=1) + ba_ref[...])
    i = jax.nn.sigmoid(jnp.concatenate(i_cols, axis=1) + bx_ref[...])

    lam = lam_ref[...]
    log_sig = jnp.minimum(lam, 0.0) - jnp.log1p(jnp.exp(-jnp.abs(lam)))
    log_a = (LRU_C * r) * log_sig
    a = jnp.exp(log_a)
    b = jnp.sqrt(-jnp.tanh(log_a) * (a * a + 1.0)) * (i * xr)

    n_lg = D_RNN // LANES

    def put(scr, val):
        for c in range(n_lg):
            for s in range(SUBLANES):
                scr[c, s * pitch:s * pitch + seg, :] = (
                    val[s * seg:(s + 1) * seg, c * LANES:(c + 1) * LANES])

    def ld(scr, j):
        return jnp.concatenate(
            [scr[c, pl.ds(j, SUBLANES, stride=pitch), :] for c in range(n_lg)],
            axis=1)

    def st(scr, j, val):
        for c in range(n_lg):
            scr[c, pl.ds(j, SUBLANES, stride=pitch), :] = (
                val[:, c * LANES:(c + 1) * LANES])

    put(a_scr, a)
    put(b_scr, b)

    def seg_step(j, carry):
        hl, pl_ = carry
        aj = ld(a_scr, j)
        hl = aj * hl + ld(b_scr, j)
        pl_ = aj * pl_
        st(h_scr, j, hl)
        st(a_scr, j, pl_)
        return hl, pl_

    zeros = jnp.zeros((SUBLANES, D_RNN), F32)
    h_end, p_end = lax.fori_loop(0, seg, seg_step, (zeros, zeros + 1.0))

    def seg_rows(scr, s):
        return jnp.concatenate(
            [scr[c, s * pitch:s * pitch + seg, :] for c in range(n_lg)], axis=1)

    c = hstate_ref[...]
    segs = []
    for s in range(SUBLANES):
        segs.append(seg_rows(h_scr, s) + seg_rows(a_scr, s) * c)
        c = h_end[s:s + 1, :] + p_end[s:s + 1, :] * c
    hstate_ref[...] = c
    hseq = jnp.concatenate(segs, axis=0)
    out = x + _dot((gate * hseq).astype(BF16), wout_ref[...])
    _store_tt(o_ref, out)


def _mixer_rglru(x_tt, moe_out, bsz, t_len, g, w_in, conv_w, conv_b, w_a, b_a,
                 w_x, b_x, lam, w_out, tm):
    y, dest_tiles = moe_out
    nt = t_len // tm
    n_tiles = bsz * nt
    seg = tm // SUBLANES
    pitch = seg + SUBLANES
    d = D_MODEL
    full = lambda shape: pl.BlockSpec(shape, lambda i: (0,) * len(shape))
    hbm = pl.BlockSpec(memory_space=pl.ANY)
    row = lambda v: v.reshape(1, -1)
    gc_specs, gc_scratch = _gather_combine_specs(n_tiles, tm)
    return pl.pallas_call(
        functools.partial(_mixer_rglru_kernel, tm=tm, nt=nt, seg=seg, pitch=pitch),
        grid=(n_tiles,),
        in_specs=gc_specs + [
            full((1, d)),
            hbm,
            full(conv_w.shape),
            full((1, D_RNN)),
            hbm,
            full((1, D_RNN)),
            hbm,
            full((1, D_RNN)),
            full((1, D_RNN)),
            hbm,
        ],
        out_specs=pl.BlockSpec((tm * TT, LANES), lambda i: (i, 0)),
        out_shape=jax.ShapeDtypeStruct(x_tt.shape, F32),
        scratch_shapes=gc_scratch + [
            pltpu.VMEM((SUBLANES, D_RNN), F32),
            pltpu.VMEM((1, D_RNN), F32),
            pltpu.VMEM((D_RNN // LANES, SUBLANES * pitch, LANES), F32),
            pltpu.VMEM((D_RNN // LANES, SUBLANES * pitch, LANES), F32),
            pltpu.VMEM((D_RNN // LANES, SUBLANES * pitch, LANES), F32),
            pltpu.VMEM((d, 2 * D_RNN), BF16),
            pltpu.VMEM((D_RNN, LRU_HEAD_DIM), BF16),
            pltpu.VMEM((D_RNN, LRU_HEAD_DIM), BF16),
            pltpu.VMEM((D_RNN, d), BF16),
            pltpu.VMEM((d, STAGE_COLS), F32),
            pltpu.SemaphoreType.DMA(()),
        ],
        compiler_params=pltpu.CompilerParams(
            dimension_semantics=("arbitrary",),
            vmem_limit_bytes=VMEM_LIMIT),
        name="mixer_rglru",
    )(dest_tiles, dest_tiles, x_tt, y, row(g), w_in, conv_w, row(conv_b),
      w_a.reshape(D_RNN, LRU_HEAD_DIM), row(b_a), w_x.reshape(D_RNN, LRU_HEAD_DIM),
      row(b_x), row(lam), w_out)


def _router_kernel(x_ref, g_ref, wr_ref, br_ref, ids_ref, cnt_ref, carry_ref,
                   *, tm):
    i = pl.program_id(0)
    x = _load_tt(x_ref, tm)
    h = _rms(x, g_ref[...]).astype(BF16)
    lt = lax.dot_general(wr_ref[...], h, (((1,), (1,)), ((), ())),
                         preferred_element_type=F32) + br_ref[...]
    gl = lt[GROUP_LANE:GROUP_LANE + N_GROUPS, :]
    gmax = jnp.max(gl, axis=0, keepdims=True)
    g_iota = lax.broadcasted_iota(jnp.int32, gl.shape, 0)
    g_idx = jnp.min(jnp.where(gl == gmax, g_iota, N_GROUPS), axis=0, keepdims=True)

    esel = lt[EXPERT_LANE:EXPERT_LANE + EXPERTS_PER_GROUP, :]
    for gidx in range(1, N_GROUPS):
        lo = EXPERT_LANE + gidx * EXPERTS_PER_GROUP
        esel = jnp.where(g_idx == gidx, lt[lo:lo + EXPERTS_PER_GROUP, :], esel)
    e_iota = lax.broadcasted_iota(jnp.int32, esel.shape, 0)
    top1 = jnp.max(esel, axis=0, keepdims=True)
    i1 = jnp.min(jnp.where(esel == top1, e_iota, EXPERTS_PER_GROUP), axis=0, keepdims=True)
    rest = jnp.where(e_iota == i1, -jnp.inf, esel)
    top2 = jnp.max(rest, axis=0, keepdims=True)
    i2 = jnp.min(jnp.where(rest == top2, e_iota, EXPERTS_PER_GROUP), axis=0, keepdims=True)

    lo_e = jnp.minimum(i1, i2)
    hi_e = jnp.maximum(i1, i2)
    pair = jnp.right_shift(lo_e * (2 * EXPERTS_PER_GROUP - 1 - lo_e), 1) + (hi_e - lo_e - 1)
    cls = g_idx * N_PAIRS + pair

    c_iota = lax.broadcasted_iota(jnp.int32, (CLS_PAD, tm), 0)
    hit = c_iota == cls
    onehot = jnp.where(hit, 1.0, 0.0)
    s_i = lax.broadcasted_iota(jnp.int32, (tm, tm), 0)
    t_i = lax.broadcasted_iota(jnp.int32, (tm, tm), 1)
    before = jnp.where(s_i < t_i, 1.0, 0.0).astype(BF16)

    @pl.when(i == 0)
    def _():
        carry_ref[...] = jnp.zeros_like(carry_ref)

    prefix = _dot(onehot.astype(BF16), before) + carry_ref[:, 0:1]
    rank = jnp.sum(jnp.where(hit, prefix, 0.0), axis=0, keepdims=True)
    carry_ref[...] = carry_ref[...] + jnp.sum(onehot, axis=1, keepdims=True)

    ids_ref[0:1, :] = cls
    ids_ref[1:2, :] = rank.astype(jnp.int32)
    ids_ref[2:8, :] = jnp.zeros((6, tm), jnp.int32)
    cnt_ref[...] = carry_ref[...].astype(jnp.int32)


def _router_weights(w_rg, b_rg, w_re, b_re):
    wr = jnp.zeros((ROUTER_ROWS, D_MODEL), F32)
    wr = wr.at[GROUP_LANE:GROUP_LANE + N_GROUPS].set(w_rg.T)
    wr = wr.at[EXPERT_LANE:EXPERT_LANE + N_EXPERTS].set(w_re.T)
    br = jnp.zeros((ROUTER_ROWS,), F32)
    br = br.at[GROUP_LANE:GROUP_LANE + N_GROUPS].set(b_rg)
    br = br.at[EXPERT_LANE:EXPERT_LANE + N_EXPERTS].set(b_re)
    return wr, br


def _router(x_tt, g, wr, br, tm):
    n_tok = x_tt.shape[0] // TT
    n_tiles = n_tok // tm
    full = lambda shape: pl.BlockSpec(shape, lambda i: (0,) * len(shape))
    return pl.pallas_call(
        functools.partial(_router_kernel, tm=tm),
        grid=(n_tiles,),
        in_specs=[
            pl.BlockSpec((tm * TT, LANES), lambda i: (i, 0)),
            full((1, D_MODEL)),
            full((ROUTER_ROWS, D_MODEL)),
            full((ROUTER_ROWS, 1)),
        ],
        out_specs=[
            pl.BlockSpec((8, tm), lambda i: (0, i)),
            full((CLS_PAD, LANES)),
        ],
        out_shape=[
            jax.ShapeDtypeStruct((8, n_tok), jnp.int32),
            jax.ShapeDtypeStruct((CLS_PAD, LANES), jnp.int32),
        ],
        scratch_shapes=[pltpu.VMEM((CLS_PAD, LANES), F32)],
        compiler_params=pltpu.CompilerParams(
            dimension_semantics=("arbitrary",),
            vmem_limit_bytes=VMEM_LIMIT),
        name="moe_router",
    )(x_tt, g.reshape(1, D_MODEL), wr.astype(BF16), br.reshape(ROUTER_ROWS, 1))


def _dest_kernel(ids_ref, pstart_ref, dest_ref):
    ids = ids_ref[...]
    c_iota = lax.broadcasted_iota(jnp.int32, (CLS_PAD, ids.shape[1]), 0)
    pstart = pstart_ref[:, 0:1]
    hit = c_iota == ids[0:1, :]
    base = jnp.sum(jnp.where(hit, pstart, 0), axis=0, keepdims=True)
    dest_ref[0:1, :] = base + ids[1:2, :]
    dest_ref[1:8, :] = jnp.zeros((7, ids.shape[1]), jnp.int32)


def _dest(ids, pad_starts, t):
    n_tok = ids.shape[1]
    pstart = jnp.broadcast_to(pad_starts[:, None], (CLS_PAD, LANES))
    return pl.pallas_call(
        _dest_kernel,
        grid=(n_tok // t,),
        in_specs=[
            pl.BlockSpec((8, t), lambda i: (0, i)),
            pl.BlockSpec((CLS_PAD, LANES), lambda i: (0, 0)),
        ],
        out_specs=pl.BlockSpec((8, t), lambda i: (0, i)),
        out_shape=jax.ShapeDtypeStruct((8, n_tok), jnp.int32),
        name="moe_dest",
    )(ids, pstart)


def _dispatch_kernel(pstart_ref, pend_ref, dest_ref, x_ref, zero_ref, xs_hbm,
                     sem, zsem, *, tm, blk, n_blocks):
    i = pl.program_id(0)

    @pl.when(i == 0)
    def _():
        def zcopy(c):
            start = pl.multiple_of((pend_ref[c] - blk) * TT, blk * TT)
            return pltpu.make_async_copy(
                zero_ref, xs_hbm.at[pl.ds(start, blk * TT), :], zsem)

        def zstart(c, _):
            @pl.when(pend_ref[c] > pstart_ref[c])
            def _():
                zcopy(c).start()
            return 0

        def zwait(c, _):
            @pl.when(pend_ref[c] > pstart_ref[c])
            def _():
                zcopy(c).wait()
            return 0

        def tcopy(b):
            return pltpu.make_async_copy(
                zero_ref,
                xs_hbm.at[pl.ds(pl.multiple_of(b * (blk * TT), blk * TT), blk * TT), :],
                zsem)

        def tstart(b, _):
            tcopy(b).start()
            return 0

        def twait(b, _):
            tcopy(b).wait()
            return 0

        n_used = pend_ref[N_CLASSES - 1] // blk
        lax.fori_loop(0, N_CLASSES, zstart, 0)
        lax.fori_loop(n_used, n_blocks, tstart, 0)
        lax.fori_loop(0, N_CLASSES, zwait, 0)
        lax.fori_loop(n_used, n_blocks, twait, 0)

    def issue(jj, _):
        for u in range(DMA_UNROLL):
            j = jj * DMA_UNROLL + u
            d = dest_ref[0, 0, j]
            pltpu.make_async_copy(
                x_ref.at[pl.ds(pl.multiple_of(j * TT, TT), TT), :],
                xs_hbm.at[pl.ds(pl.multiple_of(d * TT, TT), TT), :], sem).start()
        return 0

    lax.fori_loop(0, tm // DMA_UNROLL, issue, 0)
    pltpu.make_async_copy(x_ref, xs_hbm.at[pl.ds(0, tm * TT), :], sem).wait()


def _dispatch(x_tt, dest_tiles, pad_starts, pad_ends, n_rows, tm, blk):
    n_tok = x_tt.shape[0] // TT
    n_tiles = n_tok // tm
    zeros = jnp.zeros((blk * TT, LANES), F32)
    return pl.pallas_call(
        functools.partial(_dispatch_kernel, tm=tm, blk=blk, n_blocks=n_rows // blk),
        grid_spec=pltpu.PrefetchScalarGridSpec(
            num_scalar_prefetch=2,
            grid=(n_tiles,),
            in_specs=[
                pl.BlockSpec((1, 1, tm), lambda i, ps, pe: (i, 0, 0),
                             memory_space=pltpu.SMEM),
                pl.BlockSpec((tm * TT, LANES), lambda i, ps, pe: (i, 0)),
                pl.BlockSpec((blk * TT, LANES), lambda i, ps, pe: (0, 0)),
            ],
            out_specs=pl.BlockSpec(memory_space=pl.ANY),
            scratch_shapes=[pltpu.SemaphoreType.DMA(()), pltpu.SemaphoreType.DMA(())],
        ),
        out_shape=jax.ShapeDtypeStruct((n_rows * TT, LANES), F32),
        compiler_params=pltpu.CompilerParams(
            dimension_semantics=("arbitrary",), vmem_limit_bytes=VMEM_LIMIT),
        name="moe_dispatch",
    )(pad_starts, pad_ends, dest_tiles, x_tt, zeros)


def _expert_kernel(ea_ref, eb_ref, nu_ref, xs_ref, g_ref, wr_ref, br_ref,
                   wga_ref, wua_ref, wda_ref, wgb_ref, wub_ref, wdb_ref, y_ref,
                   wgu_bf, wd_bf, *, blk):
    i = pl.program_id(0)
    prev = jnp.maximum(i - 1, 0)
    ea = ea_ref[i]
    eb = eb_ref[i]

    @pl.when((i == 0) | (ea != ea_ref[prev]))
    def _():
        wgu_bf[:, 0:D_EXPERT] = wga_ref[0, 0].astype(BF16)
        wgu_bf[:, D_EXPERT:2 * D_EXPERT] = wua_ref[0, 0].astype(BF16)
        wd_bf[0:D_EXPERT, :] = wda_ref[0, 0].astype(BF16)

    @pl.when((i == 0) | (eb != eb_ref[prev]))
    def _():
        wgu_bf[:, 2 * D_EXPERT:3 * D_EXPERT] = wgb_ref[0, 0].astype(BF16)
        wgu_bf[:, 3 * D_EXPERT:4 * D_EXPERT] = wub_ref[0, 0].astype(BF16)
        wd_bf[D_EXPERT:2 * D_EXPERT, :] = wdb_ref[0, 0].astype(BF16)

    @pl.when(i < nu_ref[0])
    def _():
        x = _load_tt(xs_ref, blk)
        h = _rms(x, g_ref[...]).astype(BF16)

        logits = _dot(h, wr_ref[...]) + br_ref[...]
        lane = lax.broadcasted_iota(jnp.int32, logits.shape, 1)
        is_group = (lane >= GROUP_LANE) & (lane < GROUP_LANE + N_GROUPS)
        pick = lambda l: jnp.sum(jnp.where(lane == l, logits, 0.0), axis=-1, keepdims=True)
        gmax = jnp.max(jnp.where(is_group, logits, -jnp.inf), axis=-1, keepdims=True)
        gsum = jnp.sum(jnp.where(is_group, jnp.exp(logits - gmax), 0.0),
                       axis=-1, keepdims=True)
        g_prob = jnp.exp(pick(GROUP_LANE + ea // EXPERTS_PER_GROUP) - gmax) / gsum
        la = pick(EXPERT_LANE + ea)
        lb = pick(EXPERT_LANE + eb)
        m = jnp.maximum(la, lb)
        pa = jnp.exp(la - m)
        pb = jnp.exp(lb - m)
        scale = g_prob / (pa + pb)
        gate_a = pa * scale
        gate_b = pb * scale

        z = _dot(h, wgu_bf[...])
        hid_a = jax.nn.silu(z[:, 0:D_EXPERT]) * z[:, D_EXPERT:2 * D_EXPERT]
        hid_b = jax.nn.silu(z[:, 2 * D_EXPERT:3 * D_EXPERT]) * z[:, 3 * D_EXPERT:4 * D_EXPERT]
        hid = jnp.concatenate([gate_a * hid_a, gate_b * hid_b], axis=1).astype(BF16)
        _store_tt(y_ref, _dot(hid, wd_bf[...]))

    @pl.when(i >= nu_ref[0])
    def _():
        y_ref[...] = jnp.zeros_like(y_ref)


def _experts(xs, g, wr, br, w_gate, w_up, w_down, layer, block_ea, block_eb,
             n_used, blk):
    n_blocks = xs.shape[0] // (blk * TT)
    last = lambda i, ea, eb, nu: (jnp.minimum(i, nu[0] - 1), 0)
    const = lambda i, ea, eb, nu: (0, 0)
    w_a = lambda i, ea, eb, nu: (layer, ea[i], 0, 0)
    w_b = lambda i, ea, eb, nu: (layer, eb[i], 0, 0)
    wr2 = jnp.zeros((D_MODEL, LANES), F32).at[:, 0:ROUTER_ROWS].set(wr.T)
    br2 = jnp.zeros((1, LANES), F32).at[0, 0:ROUTER_ROWS].set(br)
    return pl.pallas_call(
        functools.partial(_expert_kernel, blk=blk),
        grid_spec=pltpu.PrefetchScalarGridSpec(
            num_scalar_prefetch=3,
            grid=(n_blocks,),
            in_specs=[
                pl.BlockSpec((blk * TT, LANES), last),
                pl.BlockSpec((1, D_MODEL), const),
                pl.BlockSpec((D_MODEL, LANES), const),
                pl.BlockSpec((1, LANES), const),
                pl.BlockSpec((1, 1, D_MODEL, D_EXPERT), w_a),
                pl.BlockSpec((1, 1, D_MODEL, D_EXPERT), w_a),
                pl.BlockSpec((1, 1, D_EXPERT, D_MODEL), w_a),
                pl.BlockSpec((1, 1, D_MODEL, D_EXPERT), w_b),
                pl.BlockSpec((1, 1, D_MODEL, D_EXPERT), w_b),
                pl.BlockSpec((1, 1, D_EXPERT, D_MODEL), w_b),
            ],
            out_specs=pl.BlockSpec((blk * TT, LANES), lambda i, ea, eb, nu: (i, 0)),
            scratch_shapes=[
                pltpu.VMEM((D_MODEL, 4 * D_EXPERT), BF16),
                pltpu.VMEM((2 * D_EXPERT, D_MODEL), BF16),
            ],
        ),
        out_shape=jax.ShapeDtypeStruct(xs.shape, F32),
        compiler_params=pltpu.CompilerParams(
            dimension_semantics=("arbitrary",),
            vmem_limit_bytes=VMEM_LIMIT),
        name="moe_experts",
    )(block_ea, block_eb, n_used, xs, g.reshape(1, D_MODEL), wr2.astype(BF16), br2,
      w_gate, w_up, w_down, w_gate, w_up, w_down)


def _final_kernel(dcur_ref, dnext_ref, x_ref, y_hbm, gf_ref, o_ref, yg_ref, gsems,
                  *, tm):
    out = _gather_combine(dcur_ref, dnext_ref, x_ref, y_hbm, yg_ref, gsems, tm)
    o_ref[...] = _rms(out, gf_ref[...])


def _final(x_tt, moe_out, g_final, tm):
    y, dest_tiles = moe_out
    n_tok = x_tt.shape[0] // TT
    n_tiles = n_tok // tm
    gc_specs, gc_scratch = _gather_combine_specs(n_tiles, tm)
    return pl.pallas_call(
        functools.partial(_final_kernel, tm=tm),
        grid=(n_tiles,),
        in_specs=gc_specs + [pl.BlockSpec((1, D_MODEL), lambda i: (0, 0))],
        out_specs=pl.BlockSpec((tm, D_MODEL), lambda i: (i, 0)),
        out_shape=jax.ShapeDtypeStruct((n_tok, D_MODEL), F32),
        scratch_shapes=gc_scratch,
        compiler_params=pltpu.CompilerParams(
            dimension_semantics=("arbitrary",),
            vmem_limit_bytes=VMEM_LIMIT),
        name="moe_combine_final",
    )(dest_tiles, dest_tiles, x_tt, y, g_final.reshape(1, D_MODEL))


def _moe(x_tt, g, w_rg, b_rg, w_re, b_re, w_gate, w_up, w_down, layer):
    n_tok = x_tt.shape[0] // TT
    tm, td, blk = _tiles(n_tok)
    n_blocks = n_tok // blk + N_CLASSES
    wr, br = _router_weights(w_rg, b_rg, w_re, b_re)
    ids, cnt = _router(x_tt, g, wr, br, tm)

    counts = cnt[:, 0]
    padded = (counts + blk - 1) // blk * blk
    pad_ends = jnp.cumsum(padded).astype(jnp.int32)
    pad_starts = pad_ends - padded
    block_start = jnp.arange(n_blocks, dtype=jnp.int32) * blk
    block_cls = jnp.minimum(
        jnp.sum(pad_ends[None, :N_CLASSES] <= block_start[:, None], axis=1),
        N_CLASSES - 1)
    block_ea = jnp.asarray(CLASS_LO)[block_cls]
    block_eb = jnp.asarray(CLASS_HI)[block_cls]
    n_used = (pad_ends[N_CLASSES - 1:N_CLASSES] // blk).astype(jnp.int32)

    dest = _dest(ids, pad_starts, DEST_TILE if n_tok % DEST_TILE == 0 else tm)
    per_tile = lambda t: dest[0].reshape(n_tok // t, 1, t)

    xs = _dispatch(x_tt, per_tile(td), pad_starts, pad_ends, n_blocks * blk, td, blk)
    y = _experts(xs, g, wr, br, w_gate, w_up, w_down, layer, block_ea, block_eb,
                 n_used, blk)
    return y, per_tile(tm)


def kernel(x, norm_mix_g, norm_ffn_g, norm_final_g, ab_w_in, a_ln_g, a_ws, a_ws_b, b_conv_w, ab_w_out, c_w_in, c_conv_w, c_conv_b, c_w_a, c_b_a, c_w_x, c_b_x, c_lambda, c_w_out, moe_w_rg, moe_b_rg, moe_w_re, moe_b_re, moe_w_gate, moe_w_up, moe_w_down):
    bsz, t_len, d = x.shape
    assert d == D_MODEL
    tm, _, _ = _tiles(bsz * t_len)
    assert t_len % tm == 0 and tm % CHUNK == 0

    def moe(x_tt, layer):
        return _moe(x_tt, norm_ffn_g[layer], moe_w_rg[layer], moe_b_rg[layer],
                    moe_w_re[layer], moe_b_re[layer], moe_w_gate, moe_w_up,
                    moe_w_down, layer)

    x1 = _mixer_ab(x, norm_mix_g[0], ab_w_in[0], a_ln_g[0], a_ws[0], a_ws_b[0],
                   b_conv_w[0], ab_w_out[0], tm)
    x3 = _mixer_rglru(x1, moe(x1, 0), bsz, t_len, norm_mix_g[1], c_w_in[0],
                      c_conv_w[0], c_conv_b[0], c_w_a[0], c_b_a[0], c_w_x[0],
                      c_b_x[0], c_lambda[0], c_w_out[0], tm)
    out = _final(x3, moe(x3, 1), norm_final_g, tm)
    return out.reshape(bsz, t_len, d)
```

```python
import functools

import numpy as np
import jax
import jax.numpy as jnp
from jax import lax
from jax.experimental import pallas as pl
from jax.experimental.pallas import tpu as pltpu

D_MODEL = 1024
LANES = 128
SUBLANES = 8
TT = D_MODEL // LANES
assert TT == SUBLANES

A_HEADS = 4
A_HEAD_DIM = 128
D_A = A_HEADS * A_HEAD_DIM
CHUNK = 128
D_B = D_MODEL - D_A
B_CONV = 3
D_RNN = D_MODEL
LRU_HEADS = 8
LRU_HEAD_DIM = D_RNN // LRU_HEADS
C_CONV = 4
LRU_C = 8.0
N_GROUPS = 4
EXPERTS_PER_GROUP = 8
N_EXPERTS = N_GROUPS * EXPERTS_PER_GROUP
D_EXPERT = 512
EPS = 1e-6

N_PAIRS = EXPERTS_PER_GROUP * (EXPERTS_PER_GROUP - 1) // 2
N_CLASSES = N_GROUPS * N_PAIRS
CLS_PAD = 128
assert N_CLASSES <= CLS_PAD
GROUP_LANE = 0
EXPERT_LANE = 8
ROUTER_ROWS = 48
VMEM_LIMIT = 56 * 1024 * 1024

BF16 = jnp.bfloat16
F32 = jnp.float32

_PAIRS = [(lo, hi) for lo in range(EXPERTS_PER_GROUP)
          for hi in range(lo + 1, EXPERTS_PER_GROUP)]
CLASS_LO = np.array([g * EXPERTS_PER_GROUP + lo
                     for g in range(N_GROUPS) for lo, _ in _PAIRS], np.int32)
CLASS_HI = np.array([g * EXPERTS_PER_GROUP + hi
                     for g in range(N_GROUPS) for _, hi in _PAIRS], np.int32)


def _tiles(n_tok):
    tm = 512 if n_tok % 512 == 0 else 256
    td = 1024 if n_tok % 1024 == 0 else tm
    blk = 256 if n_tok >= 8192 else 128
    return tm, td, blk


DMA_UNROLL = 16
STAGE_COLS = 512
DEST_TILE = 4096


def _load_tt(ref, nrows):
    return jnp.concatenate(
        [ref[pl.ds(s, nrows, stride=TT), :] for s in range(TT)], axis=1)


def _store_tt(ref, val):
    nrows = val.shape[0]
    for s in range(TT):
        ref[pl.ds(s, nrows, stride=TT), :] = val[:, s * LANES:(s + 1) * LANES]


def _rms(x, g):
    ms = jnp.mean(x * x, axis=-1, keepdims=True)
    return x * lax.rsqrt(ms + EPS) * g


def _dot(a, b):
    return jnp.dot(a, b, preferred_element_type=F32)


def _shift_rows(cur, prev, k):
    rolled = pltpu.roll(cur, k, axis=0)
    row = lax.broadcasted_iota(jnp.int32, prev.shape, 0)
    head = jnp.where(row < k, pltpu.roll(prev, k, axis=0), rolled[0:SUBLANES, :])
    return jnp.concatenate([head, rolled[SUBLANES:, :]], axis=0)


def _stage_bf16(w_hbm, w_bf, stage, sem):
    cols = w_hbm.shape[1]
    cw = min(cols, stage.shape[1])
    for c0 in range(0, cols, cw):
        cp = pltpu.make_async_copy(w_hbm.at[:, pl.ds(c0, cw)],
                                   stage.at[:, pl.ds(0, cw)], sem)
        cp.start()
        cp.wait()
        w_bf[:, c0:c0 + cw] = stage[:, 0:cw].astype(BF16)


def _gather_combine(dcur_ref, dnext_ref, x_ref, y_hbm, yg_ref, sems, tm):
    i = pl.program_id(0)
    slot = i % 2

    def issue(dref, sl):
        def body(jj, _):
            for u in range(DMA_UNROLL):
                j = jj * DMA_UNROLL + u
                d = dref[0, 0, j]
                pltpu.make_async_copy(
                    y_hbm.at[pl.ds(pl.multiple_of(d * TT, TT), TT), :],
                    yg_ref.at[sl, pl.ds(pl.multiple_of(j * TT, TT), TT), :],
                    sems.at[sl]).start()
            return 0

        lax.fori_loop(0, tm // DMA_UNROLL, body, 0)

    @pl.when(i == 0)
    def _():
        issue(dcur_ref, 0)

    @pl.when(i + 1 < pl.num_programs(0))
    def _():
        issue(dnext_ref, 1 - slot)

    pltpu.make_async_copy(y_hbm.at[pl.ds(0, tm * TT), :], yg_ref.at[slot],
                          sems.at[slot]).wait()
    return _load_tt(x_ref, tm) + _load_tt(yg_ref.at[slot], tm)


def _gather_combine_specs(n_tiles, tm):
    in_specs = [
        pl.BlockSpec((1, 1, tm), lambda i: (i, 0, 0), memory_space=pltpu.SMEM),
        pl.BlockSpec((1, 1, tm), lambda i: (jnp.minimum(i + 1, n_tiles - 1), 0, 0),
                     memory_space=pltpu.SMEM),
        pl.BlockSpec((tm * TT, LANES), lambda i: (i, 0)),
        pl.BlockSpec(memory_space=pl.ANY),
    ]
    scratch = [
        pltpu.VMEM((2, tm * TT, LANES), F32),
        pltpu.SemaphoreType.DMA((2,)),
    ]
    return in_specs, scratch


def _mixer_ab_kernel(x_ref, g_ref, win_hbm, lng_ref, ws_ref, wsb_ref, cw_ref,
                     wout_hbm, o_ref, carry_ref, win_ref, wout_ref, stage_ref,
                     wsem, *, tm):
    t = pl.program_id(1)

    @pl.when((pl.program_id(0) == 0) & (t == 0))
    def _():
        _stage_bf16(win_hbm, win_ref, stage_ref, wsem)
        _stage_bf16(wout_hbm, wout_ref, stage_ref, wsem)

    x = x_ref[0]
    h = _rms(x, g_ref[...]).astype(BF16)

    u = jax.nn.gelu(_dot(h, win_ref[:, 0:D_A]))
    v = jax.nn.gelu(_dot(h, win_ref[:, D_A:2 * D_A]))
    mu = jnp.mean(v, axis=-1, keepdims=True)
    vc = v - mu
    var = jnp.mean(vc * vc, axis=-1, keepdims=True)
    vn = (vc * lax.rsqrt(var + EPS) * lng_ref[...]).astype(BF16)

    r_i = lax.broadcasted_iota(jnp.int32, (CHUNK, CHUNK), 0)
    c_i = lax.broadcasted_iota(jnp.int32, (CHUNK, CHUNK), 1)
    causal = r_i >= c_i
    head_cols = []
    for hh in range(A_HEADS):
        wsh = jnp.where(causal, ws_ref[hh], 0.0).astype(BF16)
        rows = []
        for c in range(tm // CHUNK):
            blk = vn[c * CHUNK:(c + 1) * CHUNK,
                     hh * A_HEAD_DIM:(hh + 1) * A_HEAD_DIM]
            rows.append(_dot(wsh, blk))
        head_cols.append(jnp.concatenate(rows, axis=0))
    mixed = jnp.concatenate(head_cols, axis=1) + wsb_ref[...]
    y_a = (u * mixed).astype(BF16)

    gate_b = _dot(h, win_ref[:, 2 * D_A:2 * D_A + D_B])
    gate_c = _dot(h, win_ref[:, 2 * D_A + D_B:2 * D_A + 2 * D_B])
    xb = _dot(h, win_ref[:, 2 * D_A + 2 * D_B:2 * D_A + 3 * D_B])
    cx = gate_c * xb

    @pl.when(t == 0)
    def _():
        carry_ref[...] = jnp.zeros_like(carry_ref)

    prev = carry_ref[...]
    conv = (_shift_rows(cx, prev, 2) * cw_ref[0:1, :]
            + _shift_rows(cx, prev, 1) * cw_ref[1:2, :]
            + cx * cw_ref[2:3, :])
    carry_ref[...] = cx[tm - SUBLANES:tm, :]
    y_b = (gate_b * conv).astype(BF16)

    out = x + _dot(y_a, wout_ref[0:D_A, :]) + _dot(y_b, wout_ref[D_A:D_MODEL, :])
    _store_tt(o_ref, out)


def _mixer_ab(x, g, w_in, ln_g, ws, ws_b, conv_w, w_out, tm):
    bsz, t_len, d = x.shape
    nt = t_len // tm
    wsb_full = jnp.tile(jnp.repeat(ws_b.T, A_HEAD_DIM, axis=1), (tm // CHUNK, 1))
    full = lambda shape: pl.BlockSpec(shape, lambda b, t: (0,) * len(shape))
    hbm = pl.BlockSpec(memory_space=pl.ANY)
    return pl.pallas_call(
        functools.partial(_mixer_ab_kernel, tm=tm),
        grid=(bsz, nt),
        in_specs=[
            pl.BlockSpec((1, tm, d), lambda b, t: (b, t, 0)),
            full((1, d)),
            hbm,
            full((1, D_A)),
            full(ws.shape),
            full((tm, D_A)),
            full(conv_w.shape),
            hbm,
        ],
        out_specs=pl.BlockSpec((tm * TT, LANES), lambda b, t: (b * nt + t, 0)),
        out_shape=jax.ShapeDtypeStruct((bsz * t_len * TT, LANES), F32),
        scratch_shapes=[
            pltpu.VMEM((SUBLANES, D_B), F32),
            pltpu.VMEM(w_in.shape, BF16),
            pltpu.VMEM(w_out.shape, BF16),
            pltpu.VMEM((d, STAGE_COLS), F32),
            pltpu.SemaphoreType.DMA(()),
        ],
        compiler_params=pltpu.CompilerParams(
            dimension_semantics=("arbitrary", "arbitrary"),
            vmem_limit_bytes=VMEM_LIMIT),
        name="mixer_ab",
    )(x, g.reshape(1, d), w_in, ln_g.reshape(1, D_A), ws, wsb_full, conv_w, w_out)


def _mixer_rglru_kernel(dcur_ref, dnext_ref, x_ref, y_hbm,
                        g_ref, win_hbm, cw_ref, cb_ref, wa_hbm, ba_ref,
                        wx_hbm, bx_ref, lam_ref, wout_hbm, o_ref,
                        yg_ref, gsems,
                        carry_ref, hstate_ref, a_scr, b_scr, h_scr,
                        win_ref, wa_ref, wx_ref, wout_ref, stage_ref, wsem,
                        *, tm, nt, seg, pitch):
    t = pl.program_id(0) % nt

    @pl.when(pl.program_id(0) == 0)
    def _():
        _stage_bf16(win_hbm, win_ref, stage_ref, wsem)
        _stage_bf16(wa_hbm, wa_ref, stage_ref, wsem)
        _stage_bf16(wx_hbm, wx_ref, stage_ref, wsem)
        _stage_bf16(wout_hbm, wout_ref, stage_ref, wsem)

    x = _gather_combine(dcur_ref, dnext_ref, x_ref, y_hbm, yg_ref, gsems, tm)
    h = _rms(x, g_ref[...]).astype(BF16)
    gate = jax.nn.gelu(_dot(h, win_ref[:, 0:D_RNN]))
    xr0 = _dot(h, win_ref[:, D_RNN:2 * D_RNN])

    @pl.when(t == 0)
    def _():
        carry_ref[...] = jnp.zeros_like(carry_ref)
        hstate_ref[...] = jnp.zeros_like(hstate_ref)

    prev = carry_ref[...]
    xr = (_shift_rows(xr0, prev, 3) * cw_ref[0:1, :]
          + _shift_rows(xr0, prev, 2) * cw_ref[1:2, :]
          + _shift_rows(xr0, prev, 1) * cw_ref[2:3, :]
          + xr0 * cw_ref[3:4, :]) + cb_ref[...]
    carry_ref[...] = xr0[tm - SUBLANES:tm, :]

    xrb = xr.astype(BF16)
    r_cols, i_cols = [], []
    for hh in range(LRU_HEADS):
        blk = xrb[:, hh * LRU_HEAD_DIM:(hh + 1) * LRU_HEAD_DIM]
        rows = slice(hh * LRU_HEAD_DIM, (hh + 1) * LRU_HEAD_DIM)
        r_cols.append(_dot(blk, wa_ref[rows, :]))
        i_cols.append(_dot(blk, wx_ref[rows, :]))
    sigmoid = lambda v: 0.5 * jnp.tanh(0.5 * v) + 0.5
    r = sigmoid(jnp.concatenate(r_cols, axis=1) + ba_ref[...])
    i = sigmoid(jnp.concatenate(i_cols, axis=1) + bx_ref[...])

    lam = lam_ref[...]
    log_sig = jnp.minimum(lam, 0.0) - jnp.log1p(jnp.exp(-jnp.abs(lam)))
    log_a = (LRU_C * r) * log_sig
    a = jnp.exp(log_a)
    z = -jnp.tanh(log_a) * (a * a + 1.0)
    b = jnp.where(z > 0.0, z * lax.rsqrt(z), 0.0) * (i * xr)

    n_lg = D_RNN // LANES

    def put(scr, val):
        for c in range(n_lg):
            for s in range(SUBLANES):
                scr[c, s * pitch:s * pitch + seg, :] = (
                    val[s * seg:(s + 1) * seg, c * LANES:(c + 1) * LANES])

    def ld(scr, j):
        return jnp.concatenate(
            [scr[c, pl.ds(j, SUBLANES, stride=pitch), :] for c in range(n_lg)],
            axis=1)

    def st(scr, j, val):
        for c in range(n_lg):
            scr[c, pl.ds(j, SUBLANES, stride=pitch), :] = (
                val[:, c * LANES:(c + 1) * LANES])

    put(a_scr, a)
    put(b_scr, b)

    def seg_step(j, carry):
        hl, pl_ = carry
        aj = ld(a_scr, j)
        hl = aj * hl + ld(b_scr, j)
        pl_ = aj * pl_
        st(h_scr, j, hl)
        st(a_scr, j, pl_)
        return hl, pl_

    zeros = jnp.zeros((SUBLANES, D_RNN), F32)
    h_end, p_end = lax.fori_loop(0, seg, seg_step, (zeros, zeros + 1.0))

    def seg_rows(scr, s):
        return jnp.concatenate(
            [scr[c, s * pitch:s * pitch + seg, :] for c in range(n_lg)], axis=1)

    c = hstate_ref[...]
    segs = []
    for s in range(SUBLANES):
        segs.append(seg_rows(h_scr, s) + seg_rows(a_scr, s) * c)
        c = h_end[s:s + 1, :] + p_end[s:s + 1, :] * c
    hstate_ref[...] = c
    hseq = jnp.concatenate(segs, axis=0)
    out = x + _dot((gate * hseq).astype(BF16), wout_ref[...])
    _store_tt(o_ref, out)


def _mixer_rglru(x_tt, moe_out, bsz, t_len, g, w_in, conv_w, conv_b, w_a, b_a,
                 w_x, b_x, lam, w_out, tm):
    y, dest_tiles = moe_out
    nt = t_len // tm
    n_tiles = bsz * nt
    seg = tm // SUBLANES
    pitch = seg + SUBLANES
    d = D_MODEL
    full = lambda shape: pl.BlockSpec(shape, lambda i: (0,) * len(shape))
    hbm = pl.BlockSpec(memory_space=pl.ANY)
    row = lambda v: v.reshape(1, -1)
    gc_specs, gc_scratch = _gather_combine_specs(n_tiles, tm)
    return pl.pallas_call(
        functools.partial(_mixer_rglru_kernel, tm=tm, nt=nt, seg=seg, pitch=pitch),
        grid=(n_tiles,),
        in_specs=gc_specs + [
            full((1, d)),
            hbm,
            full(conv_w.shape),
            full((1, D_RNN)),
            hbm,
            full((1, D_RNN)),
            hbm,
            full((1, D_RNN)),
            full((1, D_RNN)),
            hbm,
        ],
        out_specs=pl.BlockSpec((tm * TT, LANES), lambda i: (i, 0)),
        out_shape=jax.ShapeDtypeStruct(x_tt.shape, F32),
        scratch_shapes=gc_scratch + [
            pltpu.VMEM((SUBLANES, D_RNN), F32),
            pltpu.VMEM((1, D_RNN), F32),
            pltpu.VMEM((D_RNN // LANES, SUBLANES * pitch, LANES), F32),
            pltpu.VMEM((D_RNN // LANES, SUBLANES * pitch, LANES), F32),
            pltpu.VMEM((D_RNN // LANES, SUBLANES * pitch, LANES), F32),
            pltpu.VMEM((d, 2 * D_RNN), BF16),
            pltpu.VMEM((D_RNN, LRU_HEAD_DIM), BF16),
            pltpu.VMEM((D_RNN, LRU_HEAD_DIM), BF16),
            pltpu.VMEM((D_RNN, d), BF16),
            pltpu.VMEM((d, STAGE_COLS), F32),
            pltpu.SemaphoreType.DMA(()),
        ],
        compiler_params=pltpu.CompilerParams(
            dimension_semantics=("arbitrary",),
            vmem_limit_bytes=VMEM_LIMIT),
        name="mixer_rglru",
    )(dest_tiles, dest_tiles, x_tt, y, row(g), w_in, conv_w, row(conv_b),
      w_a.reshape(D_RNN, LRU_HEAD_DIM), row(b_a), w_x.reshape(D_RNN, LRU_HEAD_DIM),
      row(b_x), row(lam), w_out)


def _router_kernel(x_ref, g_ref, wr_ref, br_ref, ids_ref, cnt_ref, carry_ref,
                   *, tm):
    i = pl.program_id(0)
    x = _load_tt(x_ref, tm)
    h = _rms(x, g_ref[...]).astype(BF16)
    lt = lax.dot_general(wr_ref[...], h, (((1,), (1,)), ((), ())),
                         preferred_element_type=F32) + br_ref[...]
    gl = lt[GROUP_LANE:GROUP_LANE + N_GROUPS, :]
    gmax = jnp.max(gl, axis=0, keepdims=True)
    g_iota = lax.broadcasted_iota(jnp.int32, gl.shape, 0)
    g_idx = jnp.min(jnp.where(gl == gmax, g_iota, N_GROUPS), axis=0, keepdims=True)

    esel = lt[EXPERT_LANE:EXPERT_LANE + EXPERTS_PER_GROUP, :]
    for gidx in range(1, N_GROUPS):
        lo = EXPERT_LANE + gidx * EXPERTS_PER_GROUP
        esel = jnp.where(g_idx == gidx, lt[lo:lo + EXPERTS_PER_GROUP, :], esel)
    e_iota = lax.broadcasted_iota(jnp.int32, esel.shape, 0)
    top1 = jnp.max(esel, axis=0, keepdims=True)
    i1 = jnp.min(jnp.where(esel == top1, e_iota, EXPERTS_PER_GROUP), axis=0, keepdims=True)
    rest = jnp.where(e_iota == i1, -jnp.inf, esel)
    top2 = jnp.max(rest, axis=0, keepdims=True)
    i2 = jnp.min(jnp.where(rest == top2, e_iota, EXPERTS_PER_GROUP), axis=0, keepdims=True)

    lo_e = jnp.minimum(i1, i2)
    hi_e = jnp.maximum(i1, i2)
    pair = jnp.right_shift(lo_e * (2 * EXPERTS_PER_GROUP - 1 - lo_e), 1) + (hi_e - lo_e - 1)
    cls = g_idx * N_PAIRS + pair

    c_iota = lax.broadcasted_iota(jnp.int32, (CLS_PAD, tm), 0)
    hit = c_iota == cls
    onehot = jnp.where(hit, 1.0, 0.0)
    s_i = lax.broadcasted_iota(jnp.int32, (tm, tm), 0)
    t_i = lax.broadcasted_iota(jnp.int32, (tm, tm), 1)
    before = jnp.where(s_i < t_i, 1.0, 0.0).astype(BF16)

    @pl.when(i == 0)
    def _():
        carry_ref[...] = jnp.zeros_like(carry_ref)

    prefix = _dot(onehot.astype(BF16), before) + carry_ref[:, 0:1]
    rank = jnp.sum(jnp.where(hit, prefix, 0.0), axis=0, keepdims=True)
    carry_ref[...] = carry_ref[...] + jnp.sum(onehot, axis=1, keepdims=True)

    ids_ref[0:1, :] = cls
    ids_ref[1:2, :] = rank.astype(jnp.int32)
    ids_ref[2:8, :] = jnp.zeros((6, tm), jnp.int32)
    cnt_ref[...] = carry_ref[...].astype(jnp.int32)


def _router_weights(w_rg, b_rg, w_re, b_re):
    wr = jnp.zeros((ROUTER_ROWS, D_MODEL), F32)
    wr = wr.at[GROUP_LANE:GROUP_LANE + N_GROUPS].set(w_rg.T)
    wr = wr.at[EXPERT_LANE:EXPERT_LANE + N_EXPERTS].set(w_re.T)
    br = jnp.zeros((ROUTER_ROWS,), F32)
    br = br.at[GROUP_LANE:GROUP_LANE + N_GROUPS].set(b_rg)
    br = br.at[EXPERT_LANE:EXPERT_LANE + N_EXPERTS].set(b_re)
    return wr, br


def _router(x_tt, g, wr, br, tm):
    n_tok = x_tt.shape[0] // TT
    n_tiles = n_tok // tm
    full = lambda shape: pl.BlockSpec(shape, lambda i: (0,) * len(shape))
    return pl.pallas_call(
        functools.partial(_router_kernel, tm=tm),
        grid=(n_tiles,),
        in_specs=[
            pl.BlockSpec((tm * TT, LANES), lambda i: (i, 0)),
            full((1, D_MODEL)),
            full((ROUTER_ROWS, D_MODEL)),
            full((ROUTER_ROWS, 1)),
        ],
        out_specs=[
            pl.BlockSpec((8, tm), lambda i: (0, i)),
            full((CLS_PAD, LANES)),
        ],
        out_shape=[
            jax.ShapeDtypeStruct((8, n_tok), jnp.int32),
            jax.ShapeDtypeStruct((CLS_PAD, LANES), jnp.int32),
        ],
        scratch_shapes=[pltpu.VMEM((CLS_PAD, LANES), F32)],
        compiler_params=pltpu.CompilerParams(
            dimension_semantics=("arbitrary",),
            vmem_limit_bytes=VMEM_LIMIT),
        name="moe_router",
    )(x_tt, g.reshape(1, D_MODEL), wr.astype(BF16), br.reshape(ROUTER_ROWS, 1))


def _dest_kernel(ids_ref, pstart_ref, dest_ref):
    ids = ids_ref[...]
    c_iota = lax.broadcasted_iota(jnp.int32, (CLS_PAD, ids.shape[1]), 0)
    pstart = pstart_ref[:, 0:1]
    hit = c_iota == ids[0:1, :]
    base = jnp.sum(jnp.where(hit, pstart, 0), axis=0, keepdims=True)
    dest_ref[0:1, :] = base + ids[1:2, :]
    dest_ref[1:8, :] = jnp.zeros((7, ids.shape[1]), jnp.int32)


def _dest(ids, pad_starts, t):
    n_tok = ids.shape[1]
    pstart = jnp.broadcast_to(pad_starts[:, None], (CLS_PAD, LANES))
    return pl.pallas_call(
        _dest_kernel,
        grid=(n_tok // t,),
        in_specs=[
            pl.BlockSpec((8, t), lambda i: (0, i)),
            pl.BlockSpec((CLS_PAD, LANES), lambda i: (0, 0)),
        ],
        out_specs=pl.BlockSpec((8, t), lambda i: (0, i)),
        out_shape=jax.ShapeDtypeStruct((8, n_tok), jnp.int32),
        name="moe_dest",
    )(ids, pstart)


def _dispatch_kernel(pstart_ref, pend_ref, dest_ref, x_ref, zero_ref, xs_hbm,
                     sem, zsem, *, tm, blk, n_blocks):
    i = pl.program_id(0)

    @pl.when(i == 0)
    def _():
        def zcopy(c):
            start = pl.multiple_of((pend_ref[c] - blk) * TT, blk * TT)
            return pltpu.make_async_copy(
                zero_ref, xs_hbm.at[pl.ds(start, blk * TT), :], zsem)

        def zstart(c, _):
            @pl.when(pend_ref[c] > pstart_ref[c])
            def _():
                zcopy(c).start()
            return 0

        def zwait(c, _):
            @pl.when(pend_ref[c] > pstart_ref[c])
            def _():
                zcopy(c).wait()
            return 0

        def tcopy(b):
            return pltpu.make_async_copy(
                zero_ref,
                xs_hbm.at[pl.ds(pl.multiple_of(b * (blk * TT), blk * TT), blk * TT), :],
                zsem)

        def tstart(b, _):
            tcopy(b).start()
            return 0

        def twait(b, _):
            tcopy(b).wait()
            return 0

        n_used = pend_ref[N_CLASSES - 1] // blk
        lax.fori_loop(0, N_CLASSES, zstart, 0)
        lax.fori_loop(n_used, n_blocks, tstart, 0)
        lax.fori_loop(0, N_CLASSES, zwait, 0)
        lax.fori_loop(n_used, n_blocks, twait, 0)

    def issue(jj, _):
        for u in range(DMA_UNROLL):
            j = jj * DMA_UNROLL + u
            d = dest_ref[0, 0, j]
            pltpu.make_async_copy(
                x_ref.at[pl.ds(pl.multiple_of(j * TT, TT), TT), :],
                xs_hbm.at[pl.ds(pl.multiple_of(d * TT, TT), TT), :], sem).start()
        return 0

    lax.fori_loop(0, tm // DMA_UNROLL, issue, 0)
    pltpu.make_async_copy(x_ref, xs_hbm.at[pl.ds(0, tm * TT), :], sem).wait()


def _dispatch(x_tt, dest_tiles, pad_starts, pad_ends, n_rows, tm, blk):
    n_tok = x_tt.shape[0] // TT
    n_tiles = n_tok // tm
    zeros = jnp.zeros((blk * TT, LANES), F32)
    return pl.pallas_call(
        functools.partial(_dispatch_kernel, tm=tm, blk=blk, n_blocks=n_rows // blk),
        grid_spec=pltpu.PrefetchScalarGridSpec(
            num_scalar_prefetch=2,
            grid=(n_tiles,),
            in_specs=[
                pl.BlockSpec((1, 1, tm), lambda i, ps, pe: (i, 0, 0),
                             memory_space=pltpu.SMEM),
                pl.BlockSpec((tm * TT, LANES), lambda i, ps, pe: (i, 0)),
                pl.BlockSpec((blk * TT, LANES), lambda i, ps, pe: (0, 0)),
            ],
            out_specs=pl.BlockSpec(memory_space=pl.ANY),
            scratch_shapes=[pltpu.SemaphoreType.DMA(()), pltpu.SemaphoreType.DMA(())],
        ),
        out_shape=jax.ShapeDtypeStruct((n_rows * TT, LANES), F32),
        compiler_params=pltpu.CompilerParams(
            dimension_semantics=("arbitrary",), vmem_limit_bytes=VMEM_LIMIT),
        name="moe_dispatch",
    )(pad_starts, pad_ends, dest_tiles, x_tt, zeros)


def _expert_kernel(ea_ref, eb_ref, nu_ref, xs_ref, g_ref, wr_ref, br_ref,
                   wga_ref, wua_ref, wda_ref, wgb_ref, wub_ref, wdb_ref, y_ref,
                   *, blk):
    i = pl.program_id(0)
    ea = ea_ref[i]
    eb = eb_ref[i]

    @pl.when(i < nu_ref[0])
    def _():
        x = _load_tt(xs_ref, blk)
        h = _rms(x, g_ref[...]).astype(BF16)

        logits = _dot(h, wr_ref[...]) + br_ref[...]
        lane = lax.broadcasted_iota(jnp.int32, logits.shape, 1)
        is_group = (lane >= GROUP_LANE) & (lane < GROUP_LANE + N_GROUPS)
        pick = lambda l: jnp.sum(jnp.where(lane == l, logits, 0.0), axis=-1, keepdims=True)
        gmax = jnp.max(jnp.where(is_group, logits, -jnp.inf), axis=-1, keepdims=True)
        gsum = jnp.sum(jnp.where(is_group, jnp.exp(logits - gmax), 0.0),
                       axis=-1, keepdims=True)
        g_prob = jnp.exp(pick(GROUP_LANE + ea // EXPERTS_PER_GROUP) - gmax) / gsum
        la = pick(EXPERT_LANE + ea)
        lb = pick(EXPERT_LANE + eb)
        m = jnp.maximum(la, lb)
        pa = jnp.exp(la - m)
        pb = jnp.exp(lb - m)
        scale = g_prob / (pa + pb)
        gate_a = pa * scale
        gate_b = pb * scale

        hid_a = (jax.nn.silu(_dot(h, wga_ref[0, 0])) * _dot(h, wua_ref[0, 0])).astype(BF16)
        hid_b = (jax.nn.silu(_dot(h, wgb_ref[0, 0])) * _dot(h, wub_ref[0, 0])).astype(BF16)
        _store_tt(y_ref, gate_a * _dot(hid_a, wda_ref[0, 0])
                  + gate_b * _dot(hid_b, wdb_ref[0, 0]))

    @pl.when(i >= nu_ref[0])
    def _():
        y_ref[...] = jnp.zeros_like(y_ref)


def _experts(xs, g, wr, br, w_gate, w_up, w_down, layer, block_ea, block_eb,
             n_used, blk):
    n_blocks = xs.shape[0] // (blk * TT)
    last = lambda i, ea, eb, nu: (jnp.minimum(i, nu[0] - 1), 0)
    const = lambda i, ea, eb, nu: (0, 0)
    w_a = lambda i, ea, eb, nu: (layer, ea[i], 0, 0)
    w_b = lambda i, ea, eb, nu: (layer, eb[i], 0, 0)
    wr2 = jnp.zeros((D_MODEL, LANES), F32).at[:, 0:ROUTER_ROWS].set(wr.T)
    br2 = jnp.zeros((1, LANES), F32).at[0, 0:ROUTER_ROWS].set(br)
    return pl.pallas_call(
        functools.partial(_expert_kernel, blk=blk),
        grid_spec=pltpu.PrefetchScalarGridSpec(
            num_scalar_prefetch=3,
            grid=(n_blocks,),
            in_specs=[
                pl.BlockSpec((blk * TT, LANES), last),
                pl.BlockSpec((1, D_MODEL), const),
                pl.BlockSpec((D_MODEL, LANES), const),
                pl.BlockSpec((1, LANES), const),
                pl.BlockSpec((1, 1, D_MODEL, D_EXPERT), w_a),
                pl.BlockSpec((1, 1, D_MODEL, D_EXPERT), w_a),
                pl.BlockSpec((1, 1, D_EXPERT, D_MODEL), w_a),
                pl.BlockSpec((1, 1, D_MODEL, D_EXPERT), w_b),
                pl.BlockSpec((1, 1, D_MODEL, D_EXPERT), w_b),
                pl.BlockSpec((1, 1, D_EXPERT, D_MODEL), w_b),
            ],
            out_specs=pl.BlockSpec((blk * TT, LANES), lambda i, ea, eb, nu: (i, 0)),
        ),
        out_shape=jax.ShapeDtypeStruct(xs.shape, F32),
        compiler_params=pltpu.CompilerParams(
            dimension_semantics=("arbitrary",),
            vmem_limit_bytes=VMEM_LIMIT),
        name="moe_experts",
    )(block_ea, block_eb, n_used, xs, g.reshape(1, D_MODEL), wr2.astype(BF16), br2,
      w_gate, w_up, w_down, w_gate, w_up, w_down)


def _final_kernel(dcur_ref, dnext_ref, x_ref, y_hbm, gf_ref, o_ref, yg_ref, gsems,
                  *, tm):
    out = _gather_combine(dcur_ref, dnext_ref, x_ref, y_hbm, yg_ref, gsems, tm)
    o_ref[...] = _rms(out, gf_ref[...])


def _final(x_tt, moe_out, g_final, tm):
    y, dest_tiles = moe_out
    n_tok = x_tt.shape[0] // TT
    n_tiles = n_tok // tm
    gc_specs, gc_scratch = _gather_combine_specs(n_tiles, tm)
    return pl.pallas_call(
        functools.partial(_final_kernel, tm=tm),
        grid=(n_tiles,),
        in_specs=gc_specs + [pl.BlockSpec((1, D_MODEL), lambda i: (0, 0))],
        out_specs=pl.BlockSpec((tm, D_MODEL), lambda i: (i, 0)),
        out_shape=jax.ShapeDtypeStruct((n_tok, D_MODEL), F32),
        scratch_shapes=gc_scratch,
        compiler_params=pltpu.CompilerParams(
            dimension_semantics=("arbitrary",),
            vmem_limit_bytes=VMEM_LIMIT),
        name="moe_combine_final",
    )(dest_tiles, dest_tiles, x_tt, y, g_final.reshape(1, D_MODEL))


def _moe(x_tt, g, w_rg, b_rg, w_re, b_re, w_gate, w_up, w_down, layer):
    n_tok = x_tt.shape[0] // TT
    tm, td, blk = _tiles(n_tok)
    n_blocks = n_tok // blk + N_CLASSES
    wr, br = _router_weights(w_rg, b_rg, w_re, b_re)
    ids, cnt = _router(x_tt, g, wr, br, tm)

    counts = cnt[:, 0]
    padded = (counts + blk - 1) // blk * blk
    pad_ends = jnp.cumsum(padded).astype(jnp.int32)
    pad_starts = pad_ends - padded
    block_start = jnp.arange(n_blocks, dtype=jnp.int32) * blk
    block_cls = jnp.minimum(
        jnp.sum(pad_ends[None, :N_CLASSES] <= block_start[:, None], axis=1),
        N_CLASSES - 1)
    block_ea = jnp.asarray(CLASS_LO)[block_cls]
    block_eb = jnp.asarray(CLASS_HI)[block_cls]
    n_used = (pad_ends[N_CLASSES - 1:N_CLASSES] // blk).astype(jnp.int32)

    dest = _dest(ids, pad_starts, DEST_TILE if n_tok % DEST_TILE == 0 else tm)
    per_tile = lambda t: dest[0].reshape(n_tok // t, 1, t)

    xs = _dispatch(x_tt, per_tile(td), pad_starts, pad_ends, n_blocks * blk, td, blk)
    y = _experts(xs, g, wr, br, w_gate, w_up, w_down, layer, block_ea, block_eb,
                 n_used, blk)
    return y, per_tile(tm)


def kernel(x, norm_mix_g, norm_ffn_g, norm_final_g, ab_w_in, a_ln_g, a_ws, a_ws_b, b_conv_w, ab_w_out, c_w_in, c_conv_w, c_conv_b, c_w_a, c_b_a, c_w_x, c_b_x, c_lambda, c_w_out, moe_w_rg, moe_b_rg, moe_w_re, moe_b_re, moe_w_gate, moe_w_up, moe_w_down):
    bsz, t_len, d = x.shape
    assert d == D_MODEL
    tm, _, _ = _tiles(bsz * t_len)
    assert t_len % tm == 0 and tm % CHUNK == 0

    w_gate, w_up, w_down = (w.astype(BF16) for w in (moe_w_gate, moe_w_up, moe_w_down))

    def moe(x_tt, layer):
        return _moe(x_tt, norm_ffn_g[layer], moe_w_rg[layer], moe_b_rg[layer],
                    moe_w_re[layer], moe_b_re[layer], w_gate, w_up, w_down, layer)

    x1 = _mixer_ab(x, norm_mix_g[0], ab_w_in[0], a_ln_g[0], a_ws[0], a_ws_b[0],
                   b_conv_w[0], ab_w_out[0], tm)
    x3 = _mixer_rglru(x1, moe(x1, 0), bsz, t_len, norm_mix_g[1], c_w_in[0],
                      c_conv_w[0], c_conv_b[0], c_w_a[0], c_b_a[0], c_w_x[0],
                      c_b_x[0], c_lambda[0], c_w_out[0], tm)
    out = _final(x3, moe(x3, 1), norm_final_g, tm)
    return out.reshape(bsz, t_len, d)
```

```python
import functools

import numpy as np
import jax
import jax.numpy as jnp
from jax import lax
from jax.experimental import pallas as pl
from jax.experimental.pallas import tpu as pltpu

D_MODEL = 1024
LANES = 128
SUBLANES = 8
TT = D_MODEL // LANES
assert TT == SUBLANES

A_HEADS = 4
A_HEAD_DIM = 128
D_A = A_HEADS * A_HEAD_DIM
CHUNK = 128
D_B = D_MODEL - D_A
B_CONV = 3
D_RNN = D_MODEL
LRU_HEADS = 8
LRU_HEAD_DIM = D_RNN // LRU_HEADS
C_CONV = 4
LRU_C = 8.0
N_GROUPS = 4
EXPERTS_PER_GROUP = 8
N_EXPERTS = N_GROUPS * EXPERTS_PER_GROUP
D_EXPERT = 512
EPS = 1e-6

N_PAIRS = EXPERTS_PER_GROUP * (EXPERTS_PER_GROUP - 1) // 2
N_CLASSES = N_GROUPS * N_PAIRS
CLS_PAD = 128
assert N_CLASSES <= CLS_PAD
GROUP_LANE = 0
EXPERT_LANE = 8
ROUTER_ROWS = 48
VMEM_LIMIT = 56 * 1024 * 1024

BF16 = jnp.bfloat16
F32 = jnp.float32

_PAIRS = [(lo, hi) for lo in range(EXPERTS_PER_GROUP)
          for hi in range(lo + 1, EXPERTS_PER_GROUP)]
CLASS_LO = np.array([g * EXPERTS_PER_GROUP + lo
                     for g in range(N_GROUPS) for lo, _ in _PAIRS], np.int32)
CLASS_HI = np.array([g * EXPERTS_PER_GROUP + hi
                     for g in range(N_GROUPS) for _, hi in _PAIRS], np.int32)


def _tiles(n_tok):
    tm = 512 if n_tok % 512 == 0 else 256
    td = n_tok // N_EXPERTS
    assert td * N_EXPERTS == n_tok and td % DMA_UNROLL == 0
    blk = 256 if n_tok >= 8192 else 128
    return tm, td, blk


DMA_UNROLL = 16
STAGE_COLS = 512
DEST_TILE = 4096


def _load_tt(ref, nrows):
    return jnp.concatenate(
        [ref[pl.ds(s, nrows, stride=TT), :] for s in range(TT)], axis=1)


def _store_tt(ref, val):
    nrows = val.shape[0]
    for s in range(TT):
        ref[pl.ds(s, nrows, stride=TT), :] = val[:, s * LANES:(s + 1) * LANES]


def _rms(x, g):
    ms = jnp.mean(x * x, axis=-1, keepdims=True)
    return x * lax.rsqrt(ms + EPS) * g


def _dot(a, b):
    return jnp.dot(a, b, preferred_element_type=F32)


def _shift_rows(cur, prev, k):
    rolled = pltpu.roll(cur, k, axis=0)
    row = lax.broadcasted_iota(jnp.int32, prev.shape, 0)
    head = jnp.where(row < k, pltpu.roll(prev, k, axis=0), rolled[0:SUBLANES, :])
    return jnp.concatenate([head, rolled[SUBLANES:, :]], axis=0)


def _stage_bf16(w_hbm, w_bf, stage, sem):
    cols = w_hbm.shape[1]
    cw = min(cols, stage.shape[1])
    for c0 in range(0, cols, cw):
        cp = pltpu.make_async_copy(w_hbm.at[:, pl.ds(c0, cw)],
                                   stage.at[:, pl.ds(0, cw)], sem)
        cp.start()
        cp.wait()
        w_bf[:, c0:c0 + cw] = stage[:, 0:cw].astype(BF16)


def _gather_combine(dcur_ref, dnext_ref, x_ref, y_hbm, yg_ref, sems, tm):
    i = pl.program_id(0)
    slot = i % 2

    def issue(dref, sl):
        def body(jj, _):
            for u in range(DMA_UNROLL):
                j = jj * DMA_UNROLL + u
                d = dref[0, 0, j]
                pltpu.make_async_copy(
                    y_hbm.at[pl.ds(pl.multiple_of(d * TT, TT), TT), :],
                    yg_ref.at[sl, pl.ds(pl.multiple_of(j * TT, TT), TT), :],
                    sems.at[sl]).start()
            return 0

        lax.fori_loop(0, tm // DMA_UNROLL, body, 0)

    @pl.when(i == 0)
    def _():
        issue(dcur_ref, 0)

    @pl.when(i + 1 < pl.num_programs(0))
    def _():
        issue(dnext_ref, 1 - slot)

    pltpu.make_async_copy(y_hbm.at[pl.ds(0, tm * TT), :], yg_ref.at[slot],
                          sems.at[slot]).wait()
    return _load_tt(x_ref, tm) + _load_tt(yg_ref.at[slot], tm)


def _gather_combine_specs(n_tiles, tm):
    in_specs = [
        pl.BlockSpec((1, 1, tm), lambda i: (i, 0, 0), memory_space=pltpu.SMEM),
        pl.BlockSpec((1, 1, tm), lambda i: (jnp.minimum(i + 1, n_tiles - 1), 0, 0),
                     memory_space=pltpu.SMEM),
        pl.BlockSpec((tm * TT, LANES), lambda i: (i, 0)),
        pl.BlockSpec(memory_space=pl.ANY),
    ]
    scratch = [
        pltpu.VMEM((2, tm * TT, LANES), F32),
        pltpu.SemaphoreType.DMA((2,)),
    ]
    return in_specs, scratch


def _mixer_ab_kernel(x_ref, g_ref, win_hbm, lng_ref, ws_ref, wsb_ref, cw_ref,
                     wout_hbm, o_ref, carry_ref, win_ref, wout_ref, stage_ref,
                     wsem, *, tm):
    t = pl.program_id(1)

    @pl.when((pl.program_id(0) == 0) & (t == 0))
    def _():
        _stage_bf16(win_hbm, win_ref, stage_ref, wsem)
        _stage_bf16(wout_hbm, wout_ref, stage_ref, wsem)

    x = x_ref[0]
    h = _rms(x, g_ref[...]).astype(BF16)

    u = jax.nn.gelu(_dot(h, win_ref[:, 0:D_A]))
    v = jax.nn.gelu(_dot(h, win_ref[:, D_A:2 * D_A]))
    mu = jnp.mean(v, axis=-1, keepdims=True)
    vc = v - mu
    var = jnp.mean(vc * vc, axis=-1, keepdims=True)
    vn = (vc * lax.rsqrt(var + EPS) * lng_ref[...]).astype(BF16)

    r_i = lax.broadcasted_iota(jnp.int32, (CHUNK, CHUNK), 0)
    c_i = lax.broadcasted_iota(jnp.int32, (CHUNK, CHUNK), 1)
    causal = r_i >= c_i
    head_cols = []
    for hh in range(A_HEADS):
        wsh = jnp.where(causal, ws_ref[hh], 0.0).astype(BF16)
        rows = []
        for c in range(tm // CHUNK):
            blk = vn[c * CHUNK:(c + 1) * CHUNK,
                     hh * A_HEAD_DIM:(hh + 1) * A_HEAD_DIM]
            rows.append(_dot(wsh, blk))
        head_cols.append(jnp.concatenate(rows, axis=0))
    mixed = jnp.concatenate(head_cols, axis=1) + wsb_ref[...]
    y_a = (u * mixed).astype(BF16)

    gate_b = _dot(h, win_ref[:, 2 * D_A:2 * D_A + D_B])
    gate_c = _dot(h, win_ref[:, 2 * D_A + D_B:2 * D_A + 2 * D_B])
    xb = _dot(h, win_ref[:, 2 * D_A + 2 * D_B:2 * D_A + 3 * D_B])
    cx = gate_c * xb

    @pl.when(t == 0)
    def _():
        carry_ref[...] = jnp.zeros_like(carry_ref)

    prev = carry_ref[...]
    conv = (_shift_rows(cx, prev, 2) * cw_ref[0:1, :]
            + _shift_rows(cx, prev, 1) * cw_ref[1:2, :]
            + cx * cw_ref[2:3, :])
    carry_ref[...] = cx[tm - SUBLANES:tm, :]
    y_b = (gate_b * conv).astype(BF16)

    out = x + _dot(y_a, wout_ref[0:D_A, :]) + _dot(y_b, wout_ref[D_A:D_MODEL, :])
    _store_tt(o_ref, out)


def _mixer_ab(x, g, w_in, ln_g, ws, ws_b, conv_w, w_out, tm):
    bsz, t_len, d = x.shape
    nt = t_len // tm
    wsb_full = jnp.tile(jnp.repeat(ws_b.T, A_HEAD_DIM, axis=1), (tm // CHUNK, 1))
    full = lambda shape: pl.BlockSpec(shape, lambda b, t: (0,) * len(shape))
    hbm = pl.BlockSpec(memory_space=pl.ANY)
    return pl.pallas_call(
        functools.partial(_mixer_ab_kernel, tm=tm),
        grid=(bsz, nt),
        in_specs=[
            pl.BlockSpec((1, tm, d), lambda b, t: (b, t, 0)),
            full((1, d)),
            hbm,
            full((1, D_A)),
            full(ws.shape),
            full((tm, D_A)),
            full(conv_w.shape),
            hbm,
        ],
        out_specs=pl.BlockSpec((tm * TT, LANES), lambda b, t: (b * nt + t, 0)),
        out_shape=jax.ShapeDtypeStruct((bsz * t_len * TT, LANES), F32),
        scratch_shapes=[
            pltpu.VMEM((SUBLANES, D_B), F32),
            pltpu.VMEM(w_in.shape, BF16),
            pltpu.VMEM(w_out.shape, BF16),
            pltpu.VMEM((d, STAGE_COLS), F32),
            pltpu.SemaphoreType.DMA(()),
        ],
        compiler_params=pltpu.CompilerParams(
            dimension_semantics=("arbitrary", "arbitrary"),
            vmem_limit_bytes=VMEM_LIMIT),
        name="mixer_ab",
    )(x, g.reshape(1, d), w_in, ln_g.reshape(1, D_A), ws, wsb_full, conv_w, w_out)


def _mixer_rglru_kernel(dcur_ref, dnext_ref, x_ref, y_hbm,
                        g_ref, win_hbm, cw_ref, cb_ref, wa_hbm, ba_ref,
                        wx_hbm, bx_ref, lam_ref, wout_hbm, o_ref,
                        yg_ref, gsems,
                        carry_ref, hstate_ref, a_scr, b_scr, h_scr,
                        win_ref, wa_ref, wx_ref, wout_ref, stage_ref, wsem,
                        *, tm, nt, seg, pitch):
    t = pl.program_id(0) % nt

    @pl.when(pl.program_id(0) == 0)
    def _():
        _stage_bf16(win_hbm, win_ref, stage_ref, wsem)
        _stage_bf16(wa_hbm, wa_ref, stage_ref, wsem)
        _stage_bf16(wx_hbm, wx_ref, stage_ref, wsem)
        _stage_bf16(wout_hbm, wout_ref, stage_ref, wsem)

    x = _gather_combine(dcur_ref, dnext_ref, x_ref, y_hbm, yg_ref, gsems, tm)
    h = _rms(x, g_ref[...]).astype(BF16)
    gate = jax.nn.gelu(_dot(h, win_ref[:, 0:D_RNN]))
    xr0 = _dot(h, win_ref[:, D_RNN:2 * D_RNN])

    @pl.when(t == 0)
    def _():
        carry_ref[...] = jnp.zeros_like(carry_ref)
        hstate_ref[...] = jnp.zeros_like(hstate_ref)

    prev = carry_ref[...]
    xr = (_shift_rows(xr0, prev, 3) * cw_ref[0:1, :]
          + _shift_rows(xr0, prev, 2) * cw_ref[1:2, :]
          + _shift_rows(xr0, prev, 1) * cw_ref[2:3, :]
          + xr0 * cw_ref[3:4, :]) + cb_ref[...]
    carry_ref[...] = xr0[tm - SUBLANES:tm, :]

    xrb = xr.astype(BF16)
    r_cols, i_cols = [], []
    for hh in range(LRU_HEADS):
        blk = xrb[:, hh * LRU_HEAD_DIM:(hh + 1) * LRU_HEAD_DIM]
        rows = slice(hh * LRU_HEAD_DIM, (hh + 1) * LRU_HEAD_DIM)
        r_cols.append(_dot(blk, wa_ref[rows, :]))
        i_cols.append(_dot(blk, wx_ref[rows, :]))
    sigmoid = lambda v: 0.5 * jnp.tanh(0.5 * v) + 0.5
    r = sigmoid(jnp.concatenate(r_cols, axis=1) + ba_ref[...])
    i = sigmoid(jnp.concatenate(i_cols, axis=1) + bx_ref[...])

    lam = lam_ref[...]
    log_sig = jnp.minimum(lam, 0.0) - jnp.log1p(jnp.exp(-jnp.abs(lam)))
    log_a = (LRU_C * r) * log_sig
    a = jnp.exp(log_a)
    z = -jnp.tanh(log_a) * (a * a + 1.0)
    b = jnp.where(z > 0.0, z * lax.rsqrt(z), 0.0) * (i * xr)

    n_lg = D_RNN // LANES

    def put(scr, val):
        for c in range(n_lg):
            for s in range(SUBLANES):
                scr[c, s * pitch:s * pitch + seg, :] = (
                    val[s * seg:(s + 1) * seg, c * LANES:(c + 1) * LANES])

    def ld(scr, j):
        return jnp.concatenate(
            [scr[c, pl.ds(j, SUBLANES, stride=pitch), :] for c in range(n_lg)],
            axis=1)

    def st(scr, j, val):
        for c in range(n_lg):
            scr[c, pl.ds(j, SUBLANES, stride=pitch), :] = (
                val[:, c * LANES:(c + 1) * LANES])

    put(a_scr, a)
    put(b_scr, b)

    def seg_step(j, carry):
        hl, pl_ = carry
        aj = ld(a_scr, j)
        hl = aj * hl + ld(b_scr, j)
        pl_ = aj * pl_
        st(h_scr, j, hl)
        st(a_scr, j, pl_)
        return hl, pl_

    zeros = jnp.zeros((SUBLANES, D_RNN), F32)
    h_end, p_end = lax.fori_loop(0, seg, seg_step, (zeros, zeros + 1.0))

    def seg_rows(scr, s):
        return jnp.concatenate(
            [scr[c, s * pitch:s * pitch + seg, :] for c in range(n_lg)], axis=1)

    c = hstate_ref[...]
    segs = []
    for s in range(SUBLANES):
        segs.append(seg_rows(h_scr, s) + seg_rows(a_scr, s) * c)
        c = h_end[s:s + 1, :] + p_end[s:s + 1, :] * c
    hstate_ref[...] = c
    hseq = jnp.concatenate(segs, axis=0)
    out = x + _dot((gate * hseq).astype(BF16), wout_ref[...])
    _store_tt(o_ref, out)


def _mixer_rglru(x_tt, moe_out, bsz, t_len, g, w_in, conv_w, conv_b, w_a, b_a,
                 w_x, b_x, lam, w_out, tm):
    y, dest_tiles = moe_out
    nt = t_len // tm
    n_tiles = bsz * nt
    seg = tm // SUBLANES
    pitch = seg + SUBLANES
    d = D_MODEL
    full = lambda shape: pl.BlockSpec(shape, lambda i: (0,) * len(shape))
    hbm = pl.BlockSpec(memory_space=pl.ANY)
    row = lambda v: v.reshape(1, -1)
    gc_specs, gc_scratch = _gather_combine_specs(n_tiles, tm)
    return pl.pallas_call(
        functools.partial(_mixer_rglru_kernel, tm=tm, nt=nt, seg=seg, pitch=pitch),
        grid=(n_tiles,),
        in_specs=gc_specs + [
            full((1, d)),
            hbm,
            full(conv_w.shape),
            full((1, D_RNN)),
            hbm,
            full((1, D_RNN)),
            hbm,
            full((1, D_RNN)),
            full((1, D_RNN)),
            hbm,
        ],
        out_specs=pl.BlockSpec((tm * TT, LANES), lambda i: (i, 0)),
        out_shape=jax.ShapeDtypeStruct(x_tt.shape, F32),
        scratch_shapes=gc_scratch + [
            pltpu.VMEM((SUBLANES, D_RNN), F32),
            pltpu.VMEM((1, D_RNN), F32),
            pltpu.VMEM((D_RNN // LANES, SUBLANES * pitch, LANES), F32),
            pltpu.VMEM((D_RNN // LANES, SUBLANES * pitch, LANES), F32),
            pltpu.VMEM((D_RNN // LANES, SUBLANES * pitch, LANES), F32),
            pltpu.VMEM((d, 2 * D_RNN), BF16),
            pltpu.VMEM((D_RNN, LRU_HEAD_DIM), BF16),
            pltpu.VMEM((D_RNN, LRU_HEAD_DIM), BF16),
            pltpu.VMEM((D_RNN, d), BF16),
            pltpu.VMEM((d, STAGE_COLS), F32),
            pltpu.SemaphoreType.DMA(()),
        ],
        compiler_params=pltpu.CompilerParams(
            dimension_semantics=("arbitrary",),
            vmem_limit_bytes=VMEM_LIMIT),
        name="mixer_rglru",
    )(dest_tiles, dest_tiles, x_tt, y, row(g), w_in, conv_w, row(conv_b),
      w_a.reshape(D_RNN, LRU_HEAD_DIM), row(b_a), w_x.reshape(D_RNN, LRU_HEAD_DIM),
      row(b_x), row(lam), w_out)


def _router_kernel(x_ref, g_ref, wr_ref, br_ref, ids_ref, cnt_ref, carry_ref,
                   before_ref, *, tm):
    i = pl.program_id(0)
    x = _load_tt(x_ref, tm)
    h = _rms(x, g_ref[...]).astype(BF16)
    lt = lax.dot_general(wr_ref[...], h, (((1,), (1,)), ((), ())),
                         preferred_element_type=F32) + br_ref[...]
    gl = lt[GROUP_LANE:GROUP_LANE + N_GROUPS, :]
    gmax = jnp.max(gl, axis=0, keepdims=True)
    g_iota = lax.broadcasted_iota(jnp.int32, gl.shape, 0)
    g_idx = jnp.min(jnp.where(gl == gmax, g_iota, N_GROUPS), axis=0, keepdims=True)

    esel = lt[EXPERT_LANE:EXPERT_LANE + EXPERTS_PER_GROUP, :]
    for gidx in range(1, N_GROUPS):
        lo = EXPERT_LANE + gidx * EXPERTS_PER_GROUP
        esel = jnp.where(g_idx == gidx, lt[lo:lo + EXPERTS_PER_GROUP, :], esel)
    e_iota = lax.broadcasted_iota(jnp.int32, esel.shape, 0)
    top1 = jnp.max(esel, axis=0, keepdims=True)
    i1 = jnp.min(jnp.where(esel == top1, e_iota, EXPERTS_PER_GROUP), axis=0, keepdims=True)
    rest = jnp.where(e_iota == i1, -jnp.inf, esel)
    top2 = jnp.max(rest, axis=0, keepdims=True)
    i2 = jnp.min(jnp.where(rest == top2, e_iota, EXPERTS_PER_GROUP), axis=0, keepdims=True)

    lo_e = jnp.minimum(i1, i2)
    hi_e = jnp.maximum(i1, i2)
    pair = jnp.right_shift(lo_e * (2 * EXPERTS_PER_GROUP - 1 - lo_e), 1) + (hi_e - lo_e - 1)
    cls = g_idx * N_PAIRS + pair

    c_iota = lax.broadcasted_iota(jnp.int32, (CLS_PAD, tm), 0)
    hit = c_iota == cls
    onehot = jnp.where(hit, 1.0, 0.0)

    @pl.when(i == 0)
    def _():
        carry_ref[...] = jnp.zeros_like(carry_ref)
        s_i = lax.broadcasted_iota(jnp.int32, (tm, tm), 0)
        t_i = lax.broadcasted_iota(jnp.int32, (tm, tm), 1)
        before_ref[...] = jnp.where(s_i < t_i, 1.0, 0.0).astype(BF16)

    prefix = _dot(onehot.astype(BF16), before_ref[...]) + carry_ref[:, 0:1]
    rank = jnp.sum(jnp.where(hit, prefix, 0.0), axis=0, keepdims=True)
    carry_ref[...] = carry_ref[...] + jnp.sum(onehot, axis=1, keepdims=True)

    ids_ref[0:1, :] = cls
    ids_ref[1:2, :] = rank.astype(jnp.int32)
    ids_ref[2:8, :] = jnp.zeros((6, tm), jnp.int32)
    cnt_ref[...] = carry_ref[...].astype(jnp.int32)


def _router_weights(w_rg, b_rg, w_re, b_re):
    wr = jnp.zeros((ROUTER_ROWS, D_MODEL), F32)
    wr = wr.at[GROUP_LANE:GROUP_LANE + N_GROUPS].set(w_rg.T)
    wr = wr.at[EXPERT_LANE:EXPERT_LANE + N_EXPERTS].set(w_re.T)
    br = jnp.zeros((ROUTER_ROWS,), F32)
    br = br.at[GROUP_LANE:GROUP_LANE + N_GROUPS].set(b_rg)
    br = br.at[EXPERT_LANE:EXPERT_LANE + N_EXPERTS].set(b_re)
    return wr, br


def _router(x_tt, g, wr, br, tm):
    n_tok = x_tt.shape[0] // TT
    n_tiles = n_tok // tm
    full = lambda shape: pl.BlockSpec(shape, lambda i: (0,) * len(shape))
    return pl.pallas_call(
        functools.partial(_router_kernel, tm=tm),
        grid=(n_tiles,),
        in_specs=[
            pl.BlockSpec((tm * TT, LANES), lambda i: (i, 0)),
            full((1, D_MODEL)),
            full((ROUTER_ROWS, D_MODEL)),
            full((ROUTER_ROWS, 1)),
        ],
        out_specs=[
            pl.BlockSpec((8, tm), lambda i: (0, i)),
            full((CLS_PAD, LANES)),
        ],
        out_shape=[
            jax.ShapeDtypeStruct((8, n_tok), jnp.int32),
            jax.ShapeDtypeStruct((CLS_PAD, LANES), jnp.int32),
        ],
        scratch_shapes=[pltpu.VMEM((CLS_PAD, LANES), F32), pltpu.VMEM((tm, tm), BF16)],
        compiler_params=pltpu.CompilerParams(
            dimension_semantics=("arbitrary",),
            vmem_limit_bytes=VMEM_LIMIT),
        name="moe_router",
    )(x_tt, g.reshape(1, D_MODEL), wr.astype(BF16), br.reshape(ROUTER_ROWS, 1))


def _dest_kernel(ids_ref, pstart_ref, dest_ref):
    ids = ids_ref[...]
    c_iota = lax.broadcasted_iota(jnp.int32, (CLS_PAD, ids.shape[1]), 0)
    pstart = pstart_ref[:, 0:1]
    hit = c_iota == ids[0:1, :]
    base = jnp.sum(jnp.where(hit, pstart, 0), axis=0, keepdims=True)
    dest_ref[0:1, :] = base + ids[1:2, :]
    dest_ref[1:8, :] = jnp.zeros((7, ids.shape[1]), jnp.int32)


def _dest(ids, pad_starts, t):
    n_tok = ids.shape[1]
    pstart = jnp.broadcast_to(pad_starts[:, None], (CLS_PAD, LANES))
    return pl.pallas_call(
        _dest_kernel,
        grid=(n_tok // t,),
        in_specs=[
            pl.BlockSpec((8, t), lambda i: (0, i)),
            pl.BlockSpec((CLS_PAD, LANES), lambda i: (0, 0)),
        ],
        out_specs=pl.BlockSpec((8, t), lambda i: (0, i)),
        out_shape=jax.ShapeDtypeStruct((8, n_tok), jnp.int32),
        name="moe_dest",
    )(ids, pstart)


def _dispatch_kernel(pstart_ref, pend_ref, dest_ref, x_ref, zero_ref,
                     wg_ref, wu_ref, wd_ref, xs_hbm, wg_out, wu_out, wd_out,
                     sem, zsem, *, tm, blk, n_blocks):
    i = pl.program_id(0)

    @pl.when(i == 0)
    def _():
        def zcopy(c):
            start = pl.multiple_of((pend_ref[c] - blk) * TT, blk * TT)
            return pltpu.make_async_copy(
                zero_ref, xs_hbm.at[pl.ds(start, blk * TT), :], zsem)

        def zstart(c, _):
            @pl.when(pend_ref[c] > pstart_ref[c])
            def _():
                zcopy(c).start()
            return 0

        def zwait(c, _):
            @pl.when(pend_ref[c] > pstart_ref[c])
            def _():
                zcopy(c).wait()
            return 0

        def tcopy(b):
            return pltpu.make_async_copy(
                zero_ref,
                xs_hbm.at[pl.ds(pl.multiple_of(b * (blk * TT), blk * TT), blk * TT), :],
                zsem)

        def tstart(b, _):
            tcopy(b).start()
            return 0

        def twait(b, _):
            tcopy(b).wait()
            return 0

        n_used = pend_ref[N_CLASSES - 1] // blk
        lax.fori_loop(0, N_CLASSES, zstart, 0)
        lax.fori_loop(n_used, n_blocks, tstart, 0)
        lax.fori_loop(0, N_CLASSES, zwait, 0)
        lax.fori_loop(n_used, n_blocks, twait, 0)

    def issue(jj, _):
        for u in range(DMA_UNROLL):
            j = jj * DMA_UNROLL + u
            d = dest_ref[0, 0, j]
            pltpu.make_async_copy(
                x_ref.at[pl.ds(pl.multiple_of(j * TT, TT), TT), :],
                xs_hbm.at[pl.ds(pl.multiple_of(d * TT, TT), TT), :], sem).start()
        return 0

    lax.fori_loop(0, tm // DMA_UNROLL, issue, 0)
    wg_out[0] = wg_ref[0, 0].astype(BF16)
    wu_out[0] = wu_ref[0, 0].astype(BF16)
    wd_out[0] = wd_ref[0, 0].astype(BF16)
    pltpu.make_async_copy(x_ref, xs_hbm.at[pl.ds(0, tm * TT), :], sem).wait()


def _dispatch(x_tt, dest_tiles, pad_starts, pad_ends, w_gate, w_up, w_down, layer,
              n_rows, tm, blk):
    n_tok = x_tt.shape[0] // TT
    n_tiles = n_tok // tm
    assert n_tiles == N_EXPERTS
    zeros = jnp.zeros((blk * TT, LANES), F32)
    w_in = lambda i, ps, pe: (layer, i, 0, 0)
    w_o = lambda i, ps, pe: (i, 0, 0)
    return pl.pallas_call(
        functools.partial(_dispatch_kernel, tm=tm, blk=blk, n_blocks=n_rows // blk),
        grid_spec=pltpu.PrefetchScalarGridSpec(
            num_scalar_prefetch=2,
            grid=(n_tiles,),
            in_specs=[
                pl.BlockSpec((1, 1, tm), lambda i, ps, pe: (i, 0, 0),
                             memory_space=pltpu.SMEM),
                pl.BlockSpec((tm * TT, LANES), lambda i, ps, pe: (i, 0)),
                pl.BlockSpec((blk * TT, LANES), lambda i, ps, pe: (0, 0)),
                pl.BlockSpec((1, 1, D_MODEL, D_EXPERT), w_in),
                pl.BlockSpec((1, 1, D_MODEL, D_EXPERT), w_in),
                pl.BlockSpec((1, 1, D_EXPERT, D_MODEL), w_in),
            ],
            out_specs=[
                pl.BlockSpec(memory_space=pl.ANY),
                pl.BlockSpec((1, D_MODEL, D_EXPERT), w_o),
                pl.BlockSpec((1, D_MODEL, D_EXPERT), w_o),
                pl.BlockSpec((1, D_EXPERT, D_MODEL), w_o),
            ],
            scratch_shapes=[pltpu.SemaphoreType.DMA(()), pltpu.SemaphoreType.DMA(())],
        ),
        out_shape=[
            jax.ShapeDtypeStruct((n_rows * TT, LANES), F32),
            jax.ShapeDtypeStruct((N_EXPERTS, D_MODEL, D_EXPERT), BF16),
            jax.ShapeDtypeStruct((N_EXPERTS, D_MODEL, D_EXPERT), BF16),
            jax.ShapeDtypeStruct((N_EXPERTS, D_EXPERT, D_MODEL), BF16),
        ],
        compiler_params=pltpu.CompilerParams(
            dimension_semantics=("arbitrary",), vmem_limit_bytes=VMEM_LIMIT),
        name="moe_dispatch",
    )(pad_starts, pad_ends, dest_tiles, x_tt, zeros, w_gate, w_up, w_down)


def _expert_kernel(ea_ref, eb_ref, nu_ref, xs_ref, g_ref, wr_ref, br_ref,
                   wga_ref, wua_ref, wda_ref, wgb_ref, wub_ref, wdb_ref, y_ref,
                   *, blk):
    i = pl.program_id(0)
    ea = ea_ref[i]
    eb = eb_ref[i]

    @pl.when(i < nu_ref[0])
    def _():
        x = _load_tt(xs_ref, blk)
        h = _rms(x, g_ref[...]).astype(BF16)

        logits = _dot(h, wr_ref[...]) + br_ref[...]
        lane = lax.broadcasted_iota(jnp.int32, logits.shape, 1)
        is_group = (lane >= GROUP_LANE) & (lane < GROUP_LANE + N_GROUPS)
        pick = lambda l: jnp.sum(jnp.where(lane == l, logits, 0.0), axis=-1, keepdims=True)
        gmax = jnp.max(jnp.where(is_group, logits, -jnp.inf), axis=-1, keepdims=True)
        gsum = jnp.sum(jnp.where(is_group, jnp.exp(logits - gmax), 0.0),
                       axis=-1, keepdims=True)
        g_prob = jnp.exp(pick(GROUP_LANE + ea // EXPERTS_PER_GROUP) - gmax) / gsum
        la = pick(EXPERT_LANE + ea)
        lb = pick(EXPERT_LANE + eb)
        m = jnp.maximum(la, lb)
        pa = jnp.exp(la - m)
        pb = jnp.exp(lb - m)
        scale = g_prob / (pa + pb)
        gate_a = pa * scale
        gate_b = pb * scale

        hid_a = (jax.nn.silu(_dot(h, wga_ref[0])) * _dot(h, wua_ref[0])).astype(BF16)
        hid_b = (jax.nn.silu(_dot(h, wgb_ref[0])) * _dot(h, wub_ref[0])).astype(BF16)
        _store_tt(y_ref, gate_a * _dot(hid_a, wda_ref[0])
                  + gate_b * _dot(hid_b, wdb_ref[0]))

    @pl.when(i >= nu_ref[0])
    def _():
        y_ref[...] = jnp.zeros_like(y_ref)


def _experts(xs, g, wr, br, w_gate, w_up, w_down, block_ea, block_eb, n_used, blk):
    n_blocks = xs.shape[0] // (blk * TT)
    last = lambda i, ea, eb, nu: (jnp.minimum(i, nu[0] - 1), 0)
    const = lambda i, ea, eb, nu: (0, 0)
    w_a = lambda i, ea, eb, nu: (ea[i], 0, 0)
    w_b = lambda i, ea, eb, nu: (eb[i], 0, 0)
    wr2 = jnp.zeros((D_MODEL, LANES), F32).at[:, 0:ROUTER_ROWS].set(wr.T)
    br2 = jnp.zeros((1, LANES), F32).at[0, 0:ROUTER_ROWS].set(br)
    return pl.pallas_call(
        functools.partial(_expert_kernel, blk=blk),
        grid_spec=pltpu.PrefetchScalarGridSpec(
            num_scalar_prefetch=3,
            grid=(n_blocks,),
            in_specs=[
                pl.BlockSpec((blk * TT, LANES), last),
                pl.BlockSpec((1, D_MODEL), const),
                pl.BlockSpec((D_MODEL, LANES), const),
                pl.BlockSpec((1, LANES), const),
                pl.BlockSpec((1, D_MODEL, D_EXPERT), w_a),
                pl.BlockSpec((1, D_MODEL, D_EXPERT), w_a),
                pl.BlockSpec((1, D_EXPERT, D_MODEL), w_a),
                pl.BlockSpec((1, D_MODEL, D_EXPERT), w_b),
                pl.BlockSpec((1, D_MODEL, D_EXPERT), w_b),
                pl.BlockSpec((1, D_EXPERT, D_MODEL), w_b),
            ],
            out_specs=pl.BlockSpec((blk * TT, LANES), lambda i, ea, eb, nu: (i, 0)),
        ),
        out_shape=jax.ShapeDtypeStruct(xs.shape, F32),
        compiler_params=pltpu.CompilerParams(
            dimension_semantics=("arbitrary",),
            vmem_limit_bytes=VMEM_LIMIT),
        name="moe_experts",
    )(block_ea, block_eb, n_used, xs, g.reshape(1, D_MODEL), wr2.astype(BF16), br2,
      w_gate, w_up, w_down, w_gate, w_up, w_down)


def _final_kernel(dcur_ref, dnext_ref, x_ref, y_hbm, gf_ref, o_ref, yg_ref, gsems,
                  *, tm):
    out = _gather_combine(dcur_ref, dnext_ref, x_ref, y_hbm, yg_ref, gsems, tm)
    o_ref[...] = _rms(out, gf_ref[...])


def _final(x_tt, moe_out, g_final, tm):
    y, dest_tiles = moe_out
    n_tok = x_tt.shape[0] // TT
    n_tiles = n_tok // tm
    gc_specs, gc_scratch = _gather_combine_specs(n_tiles, tm)
    return pl.pallas_call(
        functools.partial(_final_kernel, tm=tm),
        grid=(n_tiles,),
        in_specs=gc_specs + [pl.BlockSpec((1, D_MODEL), lambda i: (0, 0))],
        out_specs=pl.BlockSpec((tm, D_MODEL), lambda i: (i, 0)),
        out_shape=jax.ShapeDtypeStruct((n_tok, D_MODEL), F32),
        scratch_shapes=gc_scratch,
        compiler_params=pltpu.CompilerParams(
            dimension_semantics=("arbitrary",),
            vmem_limit_bytes=VMEM_LIMIT),
        name="moe_combine_final",
    )(dest_tiles, dest_tiles, x_tt, y, g_final.reshape(1, D_MODEL))


def _moe(x_tt, g, w_rg, b_rg, w_re, b_re, w_gate, w_up, w_down, layer):
    n_tok = x_tt.shape[0] // TT
    tm, td, blk = _tiles(n_tok)
    n_blocks = n_tok // blk + N_CLASSES
    wr, br = _router_weights(w_rg, b_rg, w_re, b_re)
    ids, cnt = _router(x_tt, g, wr, br, tm)

    counts = cnt[:, 0]
    padded = (counts + blk - 1) // blk * blk
    pad_ends = jnp.cumsum(padded).astype(jnp.int32)
    pad_starts = pad_ends - padded
    block_start = jnp.arange(n_blocks, dtype=jnp.int32) * blk
    block_cls = jnp.minimum(
        jnp.sum(pad_ends[None, :N_CLASSES] <= block_start[:, None], axis=1),
        N_CLASSES - 1)
    block_ea = jnp.asarray(CLASS_LO)[block_cls]
    block_eb = jnp.asarray(CLASS_HI)[block_cls]
    n_used = (pad_ends[N_CLASSES - 1:N_CLASSES] // blk).astype(jnp.int32)

    dest = _dest(ids, pad_starts, DEST_TILE if n_tok % DEST_TILE == 0 else tm)
    per_tile = lambda t: dest[0].reshape(n_tok // t, 1, t)

    xs, wg_bf, wu_bf, wd_bf = _dispatch(x_tt, per_tile(td), pad_starts, pad_ends,
                                        w_gate, w_up, w_down, layer,
                                        n_blocks * blk, td, blk)
    y = _experts(xs, g, wr, br, wg_bf, wu_bf, wd_bf, block_ea, block_eb, n_used, blk)
    return y, per_tile(tm)


def kernel(x, norm_mix_g, norm_ffn_g, norm_final_g, ab_w_in, a_ln_g, a_ws, a_ws_b, b_conv_w, ab_w_out, c_w_in, c_conv_w, c_conv_b, c_w_a, c_b_a, c_w_x, c_b_x, c_lambda, c_w_out, moe_w_rg, moe_b_rg, moe_w_re, moe_b_re, moe_w_gate, moe_w_up, moe_w_down):
    bsz, t_len, d = x.shape
    assert d == D_MODEL
    tm, _, _ = _tiles(bsz * t_len)
    assert t_len % tm == 0 and tm % CHUNK == 0

    def moe(x_tt, layer):
        return _moe(x_tt, norm_ffn_g[layer], moe_w_rg[layer], moe_b_rg[layer],
                    moe_w_re[layer], moe_b_re[layer], moe_w_gate, moe_w_up,
                    moe_w_down, layer)

    x1 = _mixer_ab(x, norm_mix_g[0], ab_w_in[0], a_ln_g[0], a_ws[0], a_ws_b[0],
                   b_conv_w[0], ab_w_out[0], tm)
    x3 = _mixer_rglru(x1, moe(x1, 0), bsz, t_len, norm_mix_g[1], c_w_in[0],
                      c_conv_w[0], c_conv_b[0], c_w_a[0], c_b_a[0], c_w_x[0],
                      c_b_x[0], c_lambda[0], c_w_out[0], tm)
    out = _final(x3, moe(x3, 1), norm_final_g, tm)
    return out.reshape(bsz, t_len, d)
```

```python
import functools

import numpy as np
import jax
import jax.numpy as jnp
from jax import lax
from jax.experimental import pallas as pl
from jax.experimental.pallas import tpu as pltpu

D_MODEL = 1024
LANES = 128
SUBLANES = 8
TT = D_MODEL // LANES
assert TT == SUBLANES

A_HEADS = 4
A_HEAD_DIM = 128
D_A = A_HEADS * A_HEAD_DIM
CHUNK = 128
D_B = D_MODEL - D_A
B_CONV = 3
D_RNN = D_MODEL
LRU_HEADS = 8
LRU_HEAD_DIM = D_RNN // LRU_HEADS
C_CONV = 4
LRU_C = 8.0
N_GROUPS = 4
EXPERTS_PER_GROUP = 8
N_EXPERTS = N_GROUPS * EXPERTS_PER_GROUP
D_EXPERT = 512
EPS = 1e-6

N_PAIRS = EXPERTS_PER_GROUP * (EXPERTS_PER_GROUP - 1) // 2
N_CLASSES = N_GROUPS * N_PAIRS
CLS_PAD = 128
assert N_CLASSES <= CLS_PAD
GROUP_LANE = 0
EXPERT_LANE = 8
ROUTER_ROWS = 48
VMEM_LIMIT = 56 * 1024 * 1024

BF16 = jnp.bfloat16
F32 = jnp.float32

_PAIRS = [(lo, hi) for lo in range(EXPERTS_PER_GROUP)
          for hi in range(lo + 1, EXPERTS_PER_GROUP)]
CLASS_LO = np.array([g * EXPERTS_PER_GROUP + lo
                     for g in range(N_GROUPS) for lo, _ in _PAIRS], np.int32)
CLASS_HI = np.array([g * EXPERTS_PER_GROUP + hi
                     for g in range(N_GROUPS) for _, hi in _PAIRS], np.int32)


def _tiles(n_tok):
    tm = 512 if n_tok % 512 == 0 else 256
    td = n_tok // N_EXPERTS
    assert td * N_EXPERTS == n_tok and td % DMA_UNROLL == 0
    blk = 256 if n_tok >= 8192 else 128
    return tm, td, blk


DMA_UNROLL = 16
STAGE_COLS = 512
DEST_TILE = 4096


def _load_tt(ref, nrows):
    return jnp.concatenate(
        [ref[pl.ds(s, nrows, stride=TT), :] for s in range(TT)], axis=1)


def _store_tt(ref, val):
    nrows = val.shape[0]
    for s in range(TT):
        ref[pl.ds(s, nrows, stride=TT), :] = val[:, s * LANES:(s + 1) * LANES]


def _rms(x, g):
    ms = jnp.mean(x * x, axis=-1, keepdims=True)
    return x * lax.rsqrt(ms + EPS) * g


def _dot(a, b):
    return jnp.dot(a, b, preferred_element_type=F32)


def _shift_rows(cur, prev, k):
    rolled = pltpu.roll(cur, k, axis=0)
    row = lax.broadcasted_iota(jnp.int32, prev.shape, 0)
    head = jnp.where(row < k, pltpu.roll(prev, k, axis=0), rolled[0:SUBLANES, :])
    return jnp.concatenate([head, rolled[SUBLANES:, :]], axis=0)


def _stage_bf16(w_hbm, w_bf, stage, sem):
    cols = w_hbm.shape[1]
    cw = min(cols, stage.shape[1])
    for c0 in range(0, cols, cw):
        cp = pltpu.make_async_copy(w_hbm.at[:, pl.ds(c0, cw)],
                                   stage.at[:, pl.ds(0, cw)], sem)
        cp.start()
        cp.wait()
        w_bf[:, c0:c0 + cw] = stage[:, 0:cw].astype(BF16)


def _gather_combine(dcur_ref, dnext_ref, x_ref, y_hbm, yg_ref, sems, tm):
    i = pl.program_id(0)
    slot = i % 2

    def issue(dref, sl):
        def body(jj, _):
            for u in range(DMA_UNROLL):
                j = jj * DMA_UNROLL + u
                d = dref[0, 0, j]
                pltpu.make_async_copy(
                    y_hbm.at[pl.ds(pl.multiple_of(d * TT, TT), TT), :],
                    yg_ref.at[sl, pl.ds(pl.multiple_of(j * TT, TT), TT), :],
                    sems.at[sl]).start()
            return 0

        lax.fori_loop(0, tm // DMA_UNROLL, body, 0)

    @pl.when(i == 0)
    def _():
        issue(dcur_ref, 0)

    @pl.when(i + 1 < pl.num_programs(0))
    def _():
        issue(dnext_ref, 1 - slot)

    pltpu.make_async_copy(y_hbm.at[pl.ds(0, tm * TT), :], yg_ref.at[slot],
                          sems.at[slot]).wait()
    return _load_tt(x_ref, tm) + _load_tt(yg_ref.at[slot], tm)


def _gather_combine_specs(n_tiles, tm):
    in_specs = [
        pl.BlockSpec((1, 1, tm), lambda i: (i, 0, 0), memory_space=pltpu.SMEM),
        pl.BlockSpec((1, 1, tm), lambda i: (jnp.minimum(i + 1, n_tiles - 1), 0, 0),
                     memory_space=pltpu.SMEM),
        pl.BlockSpec((tm * TT, LANES), lambda i: (i, 0)),
        pl.BlockSpec(memory_space=pl.ANY),
    ]
    scratch = [
        pltpu.VMEM((2, tm * TT, LANES), F32),
        pltpu.SemaphoreType.DMA((2,)),
    ]
    return in_specs, scratch


def _mixer_ab_kernel(x_ref, g_ref, win_hbm, lng_ref, ws_ref, wsb_ref, cw_ref,
                     wout_hbm, gf_ref, wr_ref, br_ref, o_ref, ids_ref, cnt_ref,
                     carry_ref, win_ref, wout_ref, stage_ref, wsem,
                     rcarry_ref, before_ref, *, tm):
    t = pl.program_id(1)
    first = (pl.program_id(0) == 0) & (t == 0)

    @pl.when(first)
    def _():
        _stage_bf16(win_hbm, win_ref, stage_ref, wsem)
        _stage_bf16(wout_hbm, wout_ref, stage_ref, wsem)

    x = x_ref[0]
    h = _rms(x, g_ref[...]).astype(BF16)

    u = jax.nn.gelu(_dot(h, win_ref[:, 0:D_A]))
    v = jax.nn.gelu(_dot(h, win_ref[:, D_A:2 * D_A]))
    mu = jnp.mean(v, axis=-1, keepdims=True)
    vc = v - mu
    var = jnp.mean(vc * vc, axis=-1, keepdims=True)
    vn = (vc * lax.rsqrt(var + EPS) * lng_ref[...]).astype(BF16)

    r_i = lax.broadcasted_iota(jnp.int32, (CHUNK, CHUNK), 0)
    c_i = lax.broadcasted_iota(jnp.int32, (CHUNK, CHUNK), 1)
    causal = r_i >= c_i
    head_cols = []
    for hh in range(A_HEADS):
        wsh = jnp.where(causal, ws_ref[hh], 0.0).astype(BF16)
        rows = []
        for c in range(tm // CHUNK):
            blk = vn[c * CHUNK:(c + 1) * CHUNK,
                     hh * A_HEAD_DIM:(hh + 1) * A_HEAD_DIM]
            rows.append(_dot(wsh, blk))
        head_cols.append(jnp.concatenate(rows, axis=0))
    mixed = jnp.concatenate(head_cols, axis=1) + wsb_ref[...]
    y_a = (u * mixed).astype(BF16)

    gate_b = _dot(h, win_ref[:, 2 * D_A:2 * D_A + D_B])
    gate_c = _dot(h, win_ref[:, 2 * D_A + D_B:2 * D_A + 2 * D_B])
    xb = _dot(h, win_ref[:, 2 * D_A + 2 * D_B:2 * D_A + 3 * D_B])
    cx = gate_c * xb

    @pl.when(t == 0)
    def _():
        carry_ref[...] = jnp.zeros_like(carry_ref)

    prev = carry_ref[...]
    conv = (_shift_rows(cx, prev, 2) * cw_ref[0:1, :]
            + _shift_rows(cx, prev, 1) * cw_ref[1:2, :]
            + cx * cw_ref[2:3, :])
    carry_ref[...] = cx[tm - SUBLANES:tm, :]
    y_b = (gate_b * conv).astype(BF16)

    out = x + _dot(y_a, wout_ref[0:D_A, :]) + _dot(y_b, wout_ref[D_A:D_MODEL, :])
    _store_tt(o_ref, out)
    _route(out, first, gf_ref, wr_ref, br_ref, ids_ref, cnt_ref, rcarry_ref,
           before_ref, tm)


def _mixer_ab(x, g, w_in, ln_g, ws, ws_b, conv_w, w_out, route_args, tm):
    bsz, t_len, d = x.shape
    nt = t_len // tm
    wsb_full = jnp.tile(jnp.repeat(ws_b.T, A_HEAD_DIM, axis=1), (tm // CHUNK, 1))
    full = lambda shape: pl.BlockSpec(shape, lambda b, t: (0,) * len(shape))
    hbm = pl.BlockSpec(memory_space=pl.ANY)
    r_in, r_out, r_shapes, r_scratch = _route_specs(lambda b, t: b * nt + t, tm,
                                                    bsz * t_len)
    return pl.pallas_call(
        functools.partial(_mixer_ab_kernel, tm=tm),
        grid=(bsz, nt),
        in_specs=[
            pl.BlockSpec((1, tm, d), lambda b, t: (b, t, 0)),
            full((1, d)),
            hbm,
            full((1, D_A)),
            full(ws.shape),
            full((tm, D_A)),
            full(conv_w.shape),
            hbm,
        ] + r_in,
        out_specs=[pl.BlockSpec((tm * TT, LANES), lambda b, t: (b * nt + t, 0))] + r_out,
        out_shape=[jax.ShapeDtypeStruct((bsz * t_len * TT, LANES), F32)] + r_shapes,
        scratch_shapes=[
            pltpu.VMEM((SUBLANES, D_B), F32),
            pltpu.VMEM(w_in.shape, BF16),
            pltpu.VMEM(w_out.shape, BF16),
            pltpu.VMEM((d, STAGE_COLS), F32),
            pltpu.SemaphoreType.DMA(()),
        ] + r_scratch,
        compiler_params=pltpu.CompilerParams(
            dimension_semantics=("arbitrary", "arbitrary"),
            vmem_limit_bytes=VMEM_LIMIT),
        name="mixer_ab",
    )(x, g.reshape(1, d), w_in, ln_g.reshape(1, D_A), ws, wsb_full, conv_w, w_out,
      *route_args)


def _mixer_rglru_kernel(dcur_ref, dnext_ref, x_ref, y_hbm,
                        g_ref, win_hbm, cw_ref, cb_ref, wa_hbm, ba_ref,
                        wx_hbm, bx_ref, lam_ref, wout_hbm, gf_ref, wr_ref, br_ref,
                        o_ref, ids_ref, cnt_ref,
                        yg_ref, gsems,
                        carry_ref, hstate_ref, a_scr, b_scr, h_scr,
                        win_ref, wa_ref, wx_ref, wout_ref, stage_ref, wsem,
                        rcarry_ref, before_ref,
                        *, tm, nt, seg, pitch):
    t = pl.program_id(0) % nt
    first = pl.program_id(0) == 0

    @pl.when(first)
    def _():
        _stage_bf16(win_hbm, win_ref, stage_ref, wsem)
        _stage_bf16(wa_hbm, wa_ref, stage_ref, wsem)
        _stage_bf16(wx_hbm, wx_ref, stage_ref, wsem)
        _stage_bf16(wout_hbm, wout_ref, stage_ref, wsem)

    x = _gather_combine(dcur_ref, dnext_ref, x_ref, y_hbm, yg_ref, gsems, tm)
    h = _rms(x, g_ref[...]).astype(BF16)
    gate = jax.nn.gelu(_dot(h, win_ref[:, 0:D_RNN]))
    xr0 = _dot(h, win_ref[:, D_RNN:2 * D_RNN])

    @pl.when(t == 0)
    def _():
        carry_ref[...] = jnp.zeros_like(carry_ref)
        hstate_ref[...] = jnp.zeros_like(hstate_ref)

    prev = carry_ref[...]
    xr = (_shift_rows(xr0, prev, 3) * cw_ref[0:1, :]
          + _shift_rows(xr0, prev, 2) * cw_ref[1:2, :]
          + _shift_rows(xr0, prev, 1) * cw_ref[2:3, :]
          + xr0 * cw_ref[3:4, :]) + cb_ref[...]
    carry_ref[...] = xr0[tm - SUBLANES:tm, :]

    xrb = xr.astype(BF16)
    r_cols, i_cols = [], []
    for hh in range(LRU_HEADS):
        blk = xrb[:, hh * LRU_HEAD_DIM:(hh + 1) * LRU_HEAD_DIM]
        rows = slice(hh * LRU_HEAD_DIM, (hh + 1) * LRU_HEAD_DIM)
        r_cols.append(_dot(blk, wa_ref[rows, :]))
        i_cols.append(_dot(blk, wx_ref[rows, :]))
    sigmoid = lambda v: 0.5 * jnp.tanh(0.5 * v) + 0.5
    r = sigmoid(jnp.concatenate(r_cols, axis=1) + ba_ref[...])
    i = sigmoid(jnp.concatenate(i_cols, axis=1) + bx_ref[...])

    lam = lam_ref[...]
    log_sig = jnp.minimum(lam, 0.0) - jnp.log1p(jnp.exp(-jnp.abs(lam)))
    log_a = (LRU_C * r) * log_sig
    a = jnp.exp(log_a)
    z = -jnp.tanh(log_a) * (a * a + 1.0)
    b = jnp.where(z > 0.0, z * lax.rsqrt(z), 0.0) * (i * xr)

    n_lg = D_RNN // LANES

    def put(scr, val):
        for c in range(n_lg):
            for s in range(SUBLANES):
                scr[c, s * pitch:s * pitch + seg, :] = (
                    val[s * seg:(s + 1) * seg, c * LANES:(c + 1) * LANES])

    def ld(scr, j):
        return jnp.concatenate(
            [scr[c, pl.ds(j, SUBLANES, stride=pitch), :] for c in range(n_lg)],
            axis=1)

    def st(scr, j, val):
        for c in range(n_lg):
            scr[c, pl.ds(j, SUBLANES, stride=pitch), :] = (
                val[:, c * LANES:(c + 1) * LANES])

    put(a_scr, a)
    put(b_scr, b)

    def seg_step(j, carry):
        hl, pl_ = carry
        aj = ld(a_scr, j)
        hl = aj * hl + ld(b_scr, j)
        pl_ = aj * pl_
        st(h_scr, j, hl)
        st(a_scr, j, pl_)
        return hl, pl_

    zeros = jnp.zeros((SUBLANES, D_RNN), F32)
    h_end, p_end = lax.fori_loop(0, seg, seg_step, (zeros, zeros + 1.0))

    def seg_rows(scr, s):
        return jnp.concatenate(
            [scr[c, s * pitch:s * pitch + seg, :] for c in range(n_lg)], axis=1)

    c = hstate_ref[...]
    segs = []
    for s in range(SUBLANES):
        segs.append(seg_rows(h_scr, s) + seg_rows(a_scr, s) * c)
        c = h_end[s:s + 1, :] + p_end[s:s + 1, :] * c
    hstate_ref[...] = c
    hseq = jnp.concatenate(segs, axis=0)
    out = x + _dot((gate * hseq).astype(BF16), wout_ref[...])
    _store_tt(o_ref, out)
    _route(out, first, gf_ref, wr_ref, br_ref, ids_ref, cnt_ref, rcarry_ref,
           before_ref, tm)


def _mixer_rglru(x_tt, moe_out, bsz, t_len, g, w_in, conv_w, conv_b, w_a, b_a,
                 w_x, b_x, lam, w_out, route_args, tm):
    y, dest_tiles = moe_out
    nt = t_len // tm
    n_tiles = bsz * nt
    seg = tm // SUBLANES
    pitch = seg + SUBLANES
    d = D_MODEL
    full = lambda shape: pl.BlockSpec(shape, lambda i: (0,) * len(shape))
    hbm = pl.BlockSpec(memory_space=pl.ANY)
    row = lambda v: v.reshape(1, -1)
    gc_specs, gc_scratch = _gather_combine_specs(n_tiles, tm)
    r_in, r_out, r_shapes, r_scratch = _route_specs(lambda i: i, tm, n_tiles * tm)
    return pl.pallas_call(
        functools.partial(_mixer_rglru_kernel, tm=tm, nt=nt, seg=seg, pitch=pitch),
        grid=(n_tiles,),
        in_specs=gc_specs + [
            full((1, d)),
            hbm,
            full(conv_w.shape),
            full((1, D_RNN)),
            hbm,
            full((1, D_RNN)),
            hbm,
            full((1, D_RNN)),
            full((1, D_RNN)),
            hbm,
        ] + r_in,
        out_specs=[pl.BlockSpec((tm * TT, LANES), lambda i: (i, 0))] + r_out,
        out_shape=[jax.ShapeDtypeStruct(x_tt.shape, F32)] + r_shapes,
        scratch_shapes=gc_scratch + [
            pltpu.VMEM((SUBLANES, D_RNN), F32),
            pltpu.VMEM((1, D_RNN), F32),
            pltpu.VMEM((D_RNN // LANES, SUBLANES * pitch, LANES), F32),
            pltpu.VMEM((D_RNN // LANES, SUBLANES * pitch, LANES), F32),
            pltpu.VMEM((D_RNN // LANES, SUBLANES * pitch, LANES), F32),
            pltpu.VMEM((d, 2 * D_RNN), BF16),
            pltpu.VMEM((D_RNN, LRU_HEAD_DIM), BF16),
            pltpu.VMEM((D_RNN, LRU_HEAD_DIM), BF16),
            pltpu.VMEM((D_RNN, d), BF16),
            pltpu.VMEM((d, STAGE_COLS), F32),
            pltpu.SemaphoreType.DMA(()),
        ] + r_scratch,
        compiler_params=pltpu.CompilerParams(
            dimension_semantics=("arbitrary",),
            vmem_limit_bytes=VMEM_LIMIT),
        name="mixer_rglru",
    )(dest_tiles, dest_tiles, x_tt, y, row(g), w_in, conv_w, row(conv_b),
      w_a.reshape(D_RNN, LRU_HEAD_DIM), row(b_a), w_x.reshape(D_RNN, LRU_HEAD_DIM),
      row(b_x), row(lam), w_out, *route_args)


def _route(x, first, gf_ref, wr_ref, br_ref, ids_ref, cnt_ref, carry_ref,
           before_ref, tm):
    h = _rms(x, gf_ref[...]).astype(BF16)
    lt = lax.dot_general(wr_ref[...], h, (((1,), (1,)), ((), ())),
                         preferred_element_type=F32) + br_ref[...]
    gl = lt[GROUP_LANE:GROUP_LANE + N_GROUPS, :]
    gmax = jnp.max(gl, axis=0, keepdims=True)
    g_iota = lax.broadcasted_iota(jnp.int32, gl.shape, 0)
    g_idx = jnp.min(jnp.where(gl == gmax, g_iota, N_GROUPS), axis=0, keepdims=True)

    esel = lt[EXPERT_LANE:EXPERT_LANE + EXPERTS_PER_GROUP, :]
    for gidx in range(1, N_GROUPS):
        lo = EXPERT_LANE + gidx * EXPERTS_PER_GROUP
        esel = jnp.where(g_idx == gidx, lt[lo:lo + EXPERTS_PER_GROUP, :], esel)
    e_iota = lax.broadcasted_iota(jnp.int32, esel.shape, 0)
    top1 = jnp.max(esel, axis=0, keepdims=True)
    i1 = jnp.min(jnp.where(esel == top1, e_iota, EXPERTS_PER_GROUP), axis=0, keepdims=True)
    rest = jnp.where(e_iota == i1, -jnp.inf, esel)
    top2 = jnp.max(rest, axis=0, keepdims=True)
    i2 = jnp.min(jnp.where(rest == top2, e_iota, EXPERTS_PER_GROUP), axis=0, keepdims=True)

    lo_e = jnp.minimum(i1, i2)
    hi_e = jnp.maximum(i1, i2)
    pair = jnp.right_shift(lo_e * (2 * EXPERTS_PER_GROUP - 1 - lo_e), 1) + (hi_e - lo_e - 1)
    cls = g_idx * N_PAIRS + pair

    c_iota = lax.broadcasted_iota(jnp.int32, (CLS_PAD, tm), 0)
    hit = c_iota == cls
    onehot = jnp.where(hit, 1.0, 0.0)

    @pl.when(first)
    def _():
        carry_ref[...] = jnp.zeros_like(carry_ref)
        s_i = lax.broadcasted_iota(jnp.int32, (tm, tm), 0)
        t_i = lax.broadcasted_iota(jnp.int32, (tm, tm), 1)
        before_ref[...] = jnp.where(s_i < t_i, 1.0, 0.0).astype(BF16)

    prefix = _dot(onehot.astype(BF16), before_ref[...]) + carry_ref[:, 0:1]
    rank = jnp.sum(jnp.where(hit, prefix, 0.0), axis=0, keepdims=True)
    carry_ref[...] = carry_ref[...] + jnp.sum(onehot, axis=1, keepdims=True)

    ids_ref[0:1, :] = cls
    ids_ref[1:2, :] = rank.astype(jnp.int32)
    ids_ref[2:8, :] = jnp.zeros((6, tm), jnp.int32)
    cnt_ref[...] = carry_ref[...].astype(jnp.int32)


def _router_weights(w_rg, b_rg, w_re, b_re):
    wr = jnp.zeros((ROUTER_ROWS, D_MODEL), F32)
    wr = wr.at[GROUP_LANE:GROUP_LANE + N_GROUPS].set(w_rg.T)
    wr = wr.at[EXPERT_LANE:EXPERT_LANE + N_EXPERTS].set(w_re.T)
    br = jnp.zeros((ROUTER_ROWS,), F32)
    br = br.at[GROUP_LANE:GROUP_LANE + N_GROUPS].set(b_rg)
    br = br.at[EXPERT_LANE:EXPERT_LANE + N_EXPERTS].set(b_re)
    return wr, br


def _route_specs(tile_idx, tm, n_tok):
    const = lambda shape: pl.BlockSpec(shape, lambda *_: (0,) * len(shape))
    in_specs = [const((1, D_MODEL)), const((ROUTER_ROWS, D_MODEL)),
                const((ROUTER_ROWS, 1))]
    out_specs = [pl.BlockSpec((8, tm), lambda *g: (0, tile_idx(*g))),
                 const((CLS_PAD, LANES))]
    out_shapes = [jax.ShapeDtypeStruct((8, n_tok), jnp.int32),
                  jax.ShapeDtypeStruct((CLS_PAD, LANES), jnp.int32)]
    scratch = [pltpu.VMEM((CLS_PAD, LANES), F32), pltpu.VMEM((tm, tm), BF16)]
    return in_specs, out_specs, out_shapes, scratch


def _route_args(g, wr, br):
    return g.reshape(1, D_MODEL), wr.astype(BF16), br.reshape(ROUTER_ROWS, 1)


def _dest_kernel(ids_ref, pstart_ref, dest_ref):
    ids = ids_ref[...]
    c_iota = lax.broadcasted_iota(jnp.int32, (CLS_PAD, ids.shape[1]), 0)
    pstart = pstart_ref[:, 0:1]
    hit = c_iota == ids[0:1, :]
    base = jnp.sum(jnp.where(hit, pstart, 0), axis=0, keepdims=True)
    dest_ref[0:1, :] = base + ids[1:2, :]
    dest_ref[1:8, :] = jnp.zeros((7, ids.shape[1]), jnp.int32)


def _dest(ids, pad_starts, t):
    n_tok = ids.shape[1]
    pstart = jnp.broadcast_to(pad_starts[:, None], (CLS_PAD, LANES))
    return pl.pallas_call(
        _dest_kernel,
        grid=(n_tok // t,),
        in_specs=[
            pl.BlockSpec((8, t), lambda i: (0, i)),
            pl.BlockSpec((CLS_PAD, LANES), lambda i: (0, 0)),
        ],
        out_specs=pl.BlockSpec((8, t), lambda i: (0, i)),
        out_shape=jax.ShapeDtypeStruct((8, n_tok), jnp.int32),
        name="moe_dest",
    )(ids, pstart)


def _dispatch_kernel(pstart_ref, pend_ref, dest_ref, x_ref, zero_ref,
                     wg_ref, wu_ref, wd_ref, xs_hbm, wg_out, wu_out, wd_out,
                     sem, zsem, *, tm, blk, n_blocks):
    i = pl.program_id(0)

    @pl.when(i == 0)
    def _():
        def zcopy(c):
            start = pl.multiple_of((pend_ref[c] - blk) * TT, blk * TT)
            return pltpu.make_async_copy(
                zero_ref, xs_hbm.at[pl.ds(start, blk * TT), :], zsem)

        def zstart(c, _):
            @pl.when(pend_ref[c] > pstart_ref[c])
            def _():
                zcopy(c).start()
            return 0

        def zwait(c, _):
            @pl.when(pend_ref[c] > pstart_ref[c])
            def _():
                zcopy(c).wait()
            return 0

        def tcopy(b):
            return pltpu.make_async_copy(
                zero_ref,
                xs_hbm.at[pl.ds(pl.multiple_of(b * (blk * TT), blk * TT), blk * TT), :],
                zsem)

        def tstart(b, _):
            tcopy(b).start()
            return 0

        def twait(b, _):
            tcopy(b).wait()
            return 0

        n_used = pend_ref[N_CLASSES - 1] // blk
        lax.fori_loop(0, N_CLASSES, zstart, 0)
        lax.fori_loop(n_used, n_blocks, tstart, 0)
        lax.fori_loop(0, N_CLASSES, zwait, 0)
        lax.fori_loop(n_used, n_blocks, twait, 0)

    def issue(jj, _):
        for u in range(DMA_UNROLL):
            j = jj * DMA_UNROLL + u
            d = dest_ref[0, 0, j]
            pltpu.make_async_copy(
                x_ref.at[pl.ds(pl.multiple_of(j * TT, TT), TT), :],
                xs_hbm.at[pl.ds(pl.multiple_of(d * TT, TT), TT), :], sem).start()
        return 0

    lax.fori_loop(0, tm // DMA_UNROLL, issue, 0)
    wg_out[0] = wg_ref[0, 0].astype(BF16)
    wu_out[0] = wu_ref[0, 0].astype(BF16)
    wd_out[0] = wd_ref[0, 0].astype(BF16)
    pltpu.make_async_copy(x_ref, xs_hbm.at[pl.ds(0, tm * TT), :], sem).wait()


def _dispatch(x_tt, dest_tiles, pad_starts, pad_ends, w_gate, w_up, w_down, layer,
              n_rows, tm, blk):
    n_tok = x_tt.shape[0] // TT
    n_tiles = n_tok // tm
    assert n_tiles == N_EXPERTS
    zeros = jnp.zeros((blk * TT, LANES), F32)
    w_in = lambda i, ps, pe: (layer, i, 0, 0)
    w_o = lambda i, ps, pe: (i, 0, 0)
    return pl.pallas_call(
        functools.partial(_dispatch_kernel, tm=tm, blk=blk, n_blocks=n_rows // blk),
        grid_spec=pltpu.PrefetchScalarGridSpec(
            num_scalar_prefetch=2,
            grid=(n_tiles,),
            in_specs=[
                pl.BlockSpec((1, 1, tm), lambda i, ps, pe: (i, 0, 0),
                             memory_space=pltpu.SMEM),
                pl.BlockSpec((tm * TT, LANES), lambda i, ps, pe: (i, 0)),
                pl.BlockSpec((blk * TT, LANES), lambda i, ps, pe: (0, 0)),
                pl.BlockSpec((1, 1, D_MODEL, D_EXPERT), w_in),
                pl.BlockSpec((1, 1, D_MODEL, D_EXPERT), w_in),
                pl.BlockSpec((1, 1, D_EXPERT, D_MODEL), w_in),
            ],
            out_specs=[
                pl.BlockSpec(memory_space=pl.ANY),
                pl.BlockSpec((1, D_MODEL, D_EXPERT), w_o),
                pl.BlockSpec((1, D_MODEL, D_EXPERT), w_o),
                pl.BlockSpec((1, D_EXPERT, D_MODEL), w_o),
            ],
            scratch_shapes=[pltpu.SemaphoreType.DMA(()), pltpu.SemaphoreType.DMA(())],
        ),
        out_shape=[
            jax.ShapeDtypeStruct((n_rows * TT, LANES), F32),
            jax.ShapeDtypeStruct((N_EXPERTS, D_MODEL, D_EXPERT), BF16),
            jax.ShapeDtypeStruct((N_EXPERTS, D_MODEL, D_EXPERT), BF16),
            jax.ShapeDtypeStruct((N_EXPERTS, D_EXPERT, D_MODEL), BF16),
        ],
        compiler_params=pltpu.CompilerParams(
            dimension_semantics=("arbitrary",), vmem_limit_bytes=VMEM_LIMIT),
        name="moe_dispatch",
    )(pad_starts, pad_ends, dest_tiles, x_tt, zeros, w_gate, w_up, w_down)


def _expert_kernel(ea_ref, eb_ref, nu_ref, xs_ref, g_ref, wr_ref, br_ref,
                   wga_ref, wua_ref, wda_ref, wgb_ref, wub_ref, wdb_ref, y_ref,
                   *, blk):
    i = pl.program_id(0)
    ea = ea_ref[i]
    eb = eb_ref[i]

    @pl.when(i < nu_ref[0])
    def _():
        x = _load_tt(xs_ref, blk)
        h = _rms(x, g_ref[...]).astype(BF16)

        logits = _dot(h, wr_ref[...]) + br_ref[...]
        lane = lax.broadcasted_iota(jnp.int32, logits.shape, 1)
        is_group = (lane >= GROUP_LANE) & (lane < GROUP_LANE + N_GROUPS)
        pick = lambda l: jnp.sum(jnp.where(lane == l, logits, 0.0), axis=-1, keepdims=True)
        gmax = jnp.max(jnp.where(is_group, logits, -jnp.inf), axis=-1, keepdims=True)
        gsum = jnp.sum(jnp.where(is_group, jnp.exp(logits - gmax), 0.0),
                       axis=-1, keepdims=True)
        g_prob = jnp.exp(pick(GROUP_LANE + ea // EXPERTS_PER_GROUP) - gmax) / gsum
        la = pick(EXPERT_LANE + ea)
        lb = pick(EXPERT_LANE + eb)
        m = jnp.maximum(la, lb)
        pa = jnp.exp(la - m)
        pb = jnp.exp(lb - m)
        scale = g_prob / (pa + pb)
        gate_a = pa * scale
        gate_b = pb * scale

        hid_a = (jax.nn.silu(_dot(h, wga_ref[0])) * _dot(h, wua_ref[0])).astype(BF16)
        hid_b = (jax.nn.silu(_dot(h, wgb_ref[0])) * _dot(h, wub_ref[0])).astype(BF16)
        _store_tt(y_ref, gate_a * _dot(hid_a, wda_ref[0])
                  + gate_b * _dot(hid_b, wdb_ref[0]))

    @pl.when(i >= nu_ref[0])
    def _():
        y_ref[...] = jnp.zeros_like(y_ref)


def _experts(xs, g, wr, br, w_gate, w_up, w_down, block_ea, block_eb, n_used, blk):
    n_blocks = xs.shape[0] // (blk * TT)
    last = lambda i, ea, eb, nu: (jnp.minimum(i, nu[0] - 1), 0)
    const = lambda i, ea, eb, nu: (0, 0)
    w_a = lambda i, ea, eb, nu: (ea[i], 0, 0)
    w_b = lambda i, ea, eb, nu: (eb[i], 0, 0)
    wr2 = jnp.zeros((D_MODEL, LANES), F32).at[:, 0:ROUTER_ROWS].set(wr.T)
    br2 = jnp.zeros((1, LANES), F32).at[0, 0:ROUTER_ROWS].set(br)
    return pl.pallas_call(
        functools.partial(_expert_kernel, blk=blk),
        grid_spec=pltpu.PrefetchScalarGridSpec(
            num_scalar_prefetch=3,
            grid=(n_blocks,),
            in_specs=[
                pl.BlockSpec((blk * TT, LANES), last),
                pl.BlockSpec((1, D_MODEL), const),
                pl.BlockSpec((D_MODEL, LANES), const),
                pl.BlockSpec((1, LANES), const),
                pl.BlockSpec((1, D_MODEL, D_EXPERT), w_a),
                pl.BlockSpec((1, D_MODEL, D_EXPERT), w_a),
                pl.BlockSpec((1, D_EXPERT, D_MODEL), w_a),
                pl.BlockSpec((1, D_MODEL, D_EXPERT), w_b),
                pl.BlockSpec((1, D_MODEL, D_EXPERT), w_b),
                pl.BlockSpec((1, D_EXPERT, D_MODEL), w_b),
            ],
            out_specs=pl.BlockSpec((blk * TT, LANES), lambda i, ea, eb, nu: (i, 0)),
        ),
        out_shape=jax.ShapeDtypeStruct(xs.shape, F32),
        compiler_params=pltpu.CompilerParams(
            dimension_semantics=("arbitrary",),
            vmem_limit_bytes=VMEM_LIMIT),
        name="moe_experts",
    )(block_ea, block_eb, n_used, xs, g.reshape(1, D_MODEL), wr2.astype(BF16), br2,
      w_gate, w_up, w_down, w_gate, w_up, w_down)


def _final_kernel(dcur_ref, dnext_ref, x_ref, y_hbm, gf_ref, o_ref, yg_ref, gsems,
                  *, tm):
    out = _gather_combine(dcur_ref, dnext_ref, x_ref, y_hbm, yg_ref, gsems, tm)
    o_ref[...] = _rms(out, gf_ref[...])


def _final(x_tt, moe_out, g_final, tm):
    y, dest_tiles = moe_out
    n_tok = x_tt.shape[0] // TT
    n_tiles = n_tok // tm
    gc_specs, gc_scratch = _gather_combine_specs(n_tiles, tm)
    return pl.pallas_call(
        functools.partial(_final_kernel, tm=tm),
        grid=(n_tiles,),
        in_specs=gc_specs + [pl.BlockSpec((1, D_MODEL), lambda i: (0, 0))],
        out_specs=pl.BlockSpec((tm, D_MODEL), lambda i: (i, 0)),
        out_shape=jax.ShapeDtypeStruct((n_tok, D_MODEL), F32),
        scratch_shapes=gc_scratch,
        compiler_params=pltpu.CompilerParams(
            dimension_semantics=("arbitrary",),
            vmem_limit_bytes=VMEM_LIMIT),
        name="moe_combine_final",
    )(dest_tiles, dest_tiles, x_tt, y, g_final.reshape(1, D_MODEL))


def _moe(x_tt, ids, cnt, g, wr, br, w_gate, w_up, w_down, layer):
    n_tok = x_tt.shape[0] // TT
    tm, td, blk = _tiles(n_tok)
    n_blocks = n_tok // blk + N_CLASSES

    counts = cnt[:, 0]
    padded = (counts + blk - 1) // blk * blk
    pad_ends = jnp.cumsum(padded).astype(jnp.int32)
    pad_starts = pad_ends - padded
    block_start = jnp.arange(n_blocks, dtype=jnp.int32) * blk
    block_cls = jnp.minimum(
        jnp.sum(pad_ends[None, :N_CLASSES] <= block_start[:, None], axis=1),
        N_CLASSES - 1)
    block_ea = jnp.asarray(CLASS_LO)[block_cls]
    block_eb = jnp.asarray(CLASS_HI)[block_cls]
    n_used = (pad_ends[N_CLASSES - 1:N_CLASSES] // blk).astype(jnp.int32)

    dest = _dest(ids, pad_starts, DEST_TILE if n_tok % DEST_TILE == 0 else tm)
    per_tile = lambda t: dest[0].reshape(n_tok // t, 1, t)

    xs, wg_bf, wu_bf, wd_bf = _dispatch(x_tt, per_tile(td), pad_starts, pad_ends,
                                        w_gate, w_up, w_down, layer,
                                        n_blocks * blk, td, blk)
    y = _experts(xs, g, wr, br, wg_bf, wu_bf, wd_bf, block_ea, block_eb, n_used, blk)
    return y, per_tile(tm)


def kernel(x, norm_mix_g, norm_ffn_g, norm_final_g, ab_w_in, a_ln_g, a_ws, a_ws_b, b_conv_w, ab_w_out, c_w_in, c_conv_w, c_conv_b, c_w_a, c_b_a, c_w_x, c_b_x, c_lambda, c_w_out, moe_w_rg, moe_b_rg, moe_w_re, moe_b_re, moe_w_gate, moe_w_up, moe_w_down):
    bsz, t_len, d = x.shape
    assert d == D_MODEL
    tm, _, _ = _tiles(bsz * t_len)
    assert t_len % tm == 0 and tm % CHUNK == 0

    routers = [_router_weights(moe_w_rg[l], moe_b_rg[l], moe_w_re[l], moe_b_re[l])
               for l in range(2)]
    route_args = [_route_args(norm_ffn_g[l], *routers[l]) for l in range(2)]

    def moe(x_tt, ids, cnt, layer):
        return _moe(x_tt, ids, cnt, norm_ffn_g[layer], *routers[layer],
                    moe_w_gate, moe_w_up, moe_w_down, layer)

    x1, ids, cnt = _mixer_ab(x, norm_mix_g[0], ab_w_in[0], a_ln_g[0], a_ws[0],
                             a_ws_b[0], b_conv_w[0], ab_w_out[0], route_args[0], tm)
    x3, ids, cnt = _mixer_rglru(x1, moe(x1, ids, cnt, 0), bsz, t_len, norm_mix_g[1],
                                c_w_in[0], c_conv_w[0], c_conv_b[0], c_w_a[0],
                                c_b_a[0], c_w_x[0], c_b_x[0], c_lambda[0],
                                c_w_out[0], route_args[1], tm)
    out = _final(x3, moe(x3, ids, cnt, 1), norm_final_g, tm)
    return out.reshape(bsz, t_len, d)
```

```python
import functools

import numpy as np
import jax
import jax.numpy as jnp
from jax import lax
from jax.experimental import pallas as pl
from jax.experimental.pallas import tpu as pltpu

D_MODEL = 1024
LANES = 128
SUBLANES = 8
TT = D_MODEL // LANES
assert TT == SUBLANES

A_HEADS = 4
A_HEAD_DIM = 128
D_A = A_HEADS * A_HEAD_DIM
CHUNK = 128
D_B = D_MODEL - D_A
B_CONV = 3
D_RNN = D_MODEL
LRU_HEADS = 8
LRU_HEAD_DIM = D_RNN // LRU_HEADS
C_CONV = 4
LRU_C = 8.0
N_GROUPS = 4
EXPERTS_PER_GROUP = 8
N_EXPERTS = N_GROUPS * EXPERTS_PER_GROUP
D_EXPERT = 512
EPS = 1e-6

N_PAIRS = EXPERTS_PER_GROUP * (EXPERTS_PER_GROUP - 1) // 2
N_CLASSES = N_GROUPS * N_PAIRS
CLS_PAD = 128
assert N_CLASSES <= CLS_PAD
GROUP_LANE = 0
EXPERT_LANE = 8
ROUTER_ROWS = 48
VMEM_LIMIT = 56 * 1024 * 1024

BF16 = jnp.bfloat16
F32 = jnp.float32

_PAIRS = [(lo, hi) for lo in range(EXPERTS_PER_GROUP)
          for hi in range(lo + 1, EXPERTS_PER_GROUP)]
CLASS_LO = np.array([g * EXPERTS_PER_GROUP + lo
                     for g in range(N_GROUPS) for lo, _ in _PAIRS], np.int32)
CLASS_HI = np.array([g * EXPERTS_PER_GROUP + hi
                     for g in range(N_GROUPS) for _, hi in _PAIRS], np.int32)


def _tiles(n_tok):
    tm = 512 if n_tok % 512 == 0 else 256
    td = n_tok // N_EXPERTS
    assert td * N_EXPERTS == n_tok and td % DMA_UNROLL == 0
    blk = 256 if n_tok >= 8192 else 128
    return tm, td, blk


DMA_UNROLL = 16
STAGE_COLS = 512
DEST_TILE = 4096
AB_TILE = 1024


def _load_tt(ref, nrows):
    return jnp.concatenate(
        [ref[pl.ds(s, nrows, stride=TT), :] for s in range(TT)], axis=1)


def _store_tt(ref, val):
    nrows = val.shape[0]
    for s in range(TT):
        ref[pl.ds(s, nrows, stride=TT), :] = val[:, s * LANES:(s + 1) * LANES]


def _rms(x, g):
    ms = jnp.mean(x * x, axis=-1, keepdims=True)
    return x * lax.rsqrt(ms + EPS) * g


def _dot(a, b):
    return jnp.dot(a, b, preferred_element_type=F32)


def _shift_rows(cur, prev, k):
    rolled = pltpu.roll(cur, k, axis=0)
    row = lax.broadcasted_iota(jnp.int32, prev.shape, 0)
    head = jnp.where(row < k, pltpu.roll(prev, k, axis=0), rolled[0:SUBLANES, :])
    return jnp.concatenate([head, rolled[SUBLANES:, :]], axis=0)


def _stage_bf16(w_hbm, w_bf, stage, sem):
    cols = w_hbm.shape[1]
    cw = min(cols, stage.shape[1])
    for c0 in range(0, cols, cw):
        cp = pltpu.make_async_copy(w_hbm.at[:, pl.ds(c0, cw)],
                                   stage.at[:, pl.ds(0, cw)], sem)
        cp.start()
        cp.wait()
        w_bf[:, c0:c0 + cw] = stage[:, 0:cw].astype(BF16)


def _gather_combine(dcur_ref, dnext_ref, x_ref, y_hbm, yg_ref, sems, tm):
    i = pl.program_id(0)
    slot = i % 2

    def issue(dref, sl):
        def body(jj, _):
            for u in range(DMA_UNROLL):
                j = jj * DMA_UNROLL + u
                d = dref[0, 0, j]
                pltpu.make_async_copy(
                    y_hbm.at[pl.ds(pl.multiple_of(d * TT, TT), TT), :],
                    yg_ref.at[sl, pl.ds(pl.multiple_of(j * TT, TT), TT), :],
                    sems.at[sl]).start()
            return 0

        lax.fori_loop(0, tm // DMA_UNROLL, body, 0)

    @pl.when(i == 0)
    def _():
        issue(dcur_ref, 0)

    @pl.when(i + 1 < pl.num_programs(0))
    def _():
        issue(dnext_ref, 1 - slot)

    pltpu.make_async_copy(y_hbm.at[pl.ds(0, tm * TT), :], yg_ref.at[slot],
                          sems.at[slot]).wait()
    return _load_tt(x_ref, tm) + _load_tt(yg_ref.at[slot], tm)


def _gather_combine_specs(n_tiles, tm):
    in_specs = [
        pl.BlockSpec((1, 1, tm), lambda i: (i, 0, 0), memory_space=pltpu.SMEM),
        pl.BlockSpec((1, 1, tm), lambda i: (jnp.minimum(i + 1, n_tiles - 1), 0, 0),
                     memory_space=pltpu.SMEM),
        pl.BlockSpec((tm * TT, LANES), lambda i: (i, 0)),
        pl.BlockSpec(memory_space=pl.ANY),
    ]
    scratch = [
        pltpu.VMEM((2, tm * TT, LANES), F32),
        pltpu.SemaphoreType.DMA((2,)),
    ]
    return in_specs, scratch


def _mixer_ab_kernel(x_ref, g_ref, win_hbm, lng_ref, ws_ref, wsb_ref, cw_ref,
                     wout_hbm, gf_ref, wr_ref, br_ref, o_ref, ids_ref, cnt_ref,
                     carry_ref, win_ref, wout_ref, stage_ref, wsem,
                     rcarry_ref, before_ref, *, tm):
    t = pl.program_id(1)
    first = (pl.program_id(0) == 0) & (t == 0)

    @pl.when(first)
    def _():
        _stage_bf16(win_hbm, win_ref, stage_ref, wsem)
        _stage_bf16(wout_hbm, wout_ref, stage_ref, wsem)

    x = x_ref[0]
    h = _rms(x, g_ref[...]).astype(BF16)

    u = jax.nn.gelu(_dot(h, win_ref[:, 0:D_A]))
    v = jax.nn.gelu(_dot(h, win_ref[:, D_A:2 * D_A]))
    mu = jnp.mean(v, axis=-1, keepdims=True)
    vc = v - mu
    var = jnp.mean(vc * vc, axis=-1, keepdims=True)
    vn = (vc * lax.rsqrt(var + EPS) * lng_ref[...]).astype(BF16)

    r_i = lax.broadcasted_iota(jnp.int32, (CHUNK, CHUNK), 0)
    c_i = lax.broadcasted_iota(jnp.int32, (CHUNK, CHUNK), 1)
    causal = r_i >= c_i
    head_cols = []
    for hh in range(A_HEADS):
        wsh = jnp.where(causal, ws_ref[hh], 0.0).astype(BF16)
        rows = []
        for c in range(tm // CHUNK):
            blk = vn[c * CHUNK:(c + 1) * CHUNK,
                     hh * A_HEAD_DIM:(hh + 1) * A_HEAD_DIM]
            rows.append(_dot(wsh, blk))
        head_cols.append(jnp.concatenate(rows, axis=0))
    mixed = jnp.concatenate(head_cols, axis=1) + wsb_ref[...]
    y_a = (u * mixed).astype(BF16)

    gate_b = _dot(h, win_ref[:, 2 * D_A:2 * D_A + D_B])
    gate_c = _dot(h, win_ref[:, 2 * D_A + D_B:2 * D_A + 2 * D_B])
    xb = _dot(h, win_ref[:, 2 * D_A + 2 * D_B:2 * D_A + 3 * D_B])
    cx = gate_c * xb

    @pl.when(t == 0)
    def _():
        carry_ref[...] = jnp.zeros_like(carry_ref)

    prev = carry_ref[...]
    conv = (_shift_rows(cx, prev, 2) * cw_ref[0:1, :]
            + _shift_rows(cx, prev, 1) * cw_ref[1:2, :]
            + cx * cw_ref[2:3, :])
    carry_ref[...] = cx[tm - SUBLANES:tm, :]
    y_b = (gate_b * conv).astype(BF16)

    out = x + _dot(y_a, wout_ref[0:D_A, :]) + _dot(y_b, wout_ref[D_A:D_MODEL, :])
    _store_tt(o_ref, out)
    _route(out, first, gf_ref, wr_ref, br_ref, ids_ref, cnt_ref, rcarry_ref,
           before_ref, tm)


def _mixer_ab(x, g, w_in, ln_g, ws, ws_b, conv_w, w_out, route_args, tm):
    bsz, t_len, d = x.shape
    nt = t_len // tm
    wsb_full = jnp.tile(jnp.repeat(ws_b.T, A_HEAD_DIM, axis=1), (tm // CHUNK, 1))
    full = lambda shape: pl.BlockSpec(shape, lambda b, t: (0,) * len(shape))
    hbm = pl.BlockSpec(memory_space=pl.ANY)
    r_in, r_out, r_shapes, r_scratch = _route_specs(lambda b, t: b * nt + t, tm,
                                                    bsz * t_len)
    return pl.pallas_call(
        functools.partial(_mixer_ab_kernel, tm=tm),
        grid=(bsz, nt),
        in_specs=[
            pl.BlockSpec((1, tm, d), lambda b, t: (b, t, 0)),
            full((1, d)),
            hbm,
            full((1, D_A)),
            full(ws.shape),
            full((tm, D_A)),
            full(conv_w.shape),
            hbm,
        ] + r_in,
        out_specs=[pl.BlockSpec((tm * TT, LANES), lambda b, t: (b * nt + t, 0))] + r_out,
        out_shape=[jax.ShapeDtypeStruct((bsz * t_len * TT, LANES), F32)] + r_shapes,
        scratch_shapes=[
            pltpu.VMEM((SUBLANES, D_B), F32),
            pltpu.VMEM(w_in.shape, BF16),
            pltpu.VMEM(w_out.shape, BF16),
            pltpu.VMEM((d, STAGE_COLS), F32),
            pltpu.SemaphoreType.DMA(()),
        ] + r_scratch,
        compiler_params=pltpu.CompilerParams(
            dimension_semantics=("arbitrary", "arbitrary"),
            vmem_limit_bytes=VMEM_LIMIT),
        name="mixer_ab",
    )(x, g.reshape(1, d), w_in, ln_g.reshape(1, D_A), ws, wsb_full, conv_w, w_out,
      *route_args)


def _mixer_rglru_kernel(dcur_ref, dnext_ref, x_ref, y_hbm,
                        g_ref, win_hbm, cw_ref, cb_ref, wa_hbm, ba_ref,
                        wx_hbm, bx_ref, lam_ref, wout_hbm, gf_ref, wr_ref, br_ref,
                        o_ref, ids_ref, cnt_ref,
                        yg_ref, gsems,
                        carry_ref, hstate_ref, a_scr, b_scr, h_scr,
                        win_ref, wa_ref, wx_ref, wout_ref, stage_ref, wsem,
                        rcarry_ref, before_ref,
                        *, tm, nt, seg, pitch):
    t = pl.program_id(0) % nt
    first = pl.program_id(0) == 0

    @pl.when(first)
    def _():
        _stage_bf16(win_hbm, win_ref, stage_ref, wsem)
        _stage_bf16(wa_hbm, wa_ref, stage_ref, wsem)
        _stage_bf16(wx_hbm, wx_ref, stage_ref, wsem)
        _stage_bf16(wout_hbm, wout_ref, stage_ref, wsem)

    x = _gather_combine(dcur_ref, dnext_ref, x_ref, y_hbm, yg_ref, gsems, tm)
    h = _rms(x, g_ref[...]).astype(BF16)
    gate = jax.nn.gelu(_dot(h, win_ref[:, 0:D_RNN]))
    xr0 = _dot(h, win_ref[:, D_RNN:2 * D_RNN])

    @pl.when(t == 0)
    def _():
        carry_ref[...] = jnp.zeros_like(carry_ref)
        hstate_ref[...] = jnp.zeros_like(hstate_ref)

    prev = carry_ref[...]
    xr = (_shift_rows(xr0, prev, 3) * cw_ref[0:1, :]
          + _shift_rows(xr0, prev, 2) * cw_ref[1:2, :]
          + _shift_rows(xr0, prev, 1) * cw_ref[2:3, :]
          + xr0 * cw_ref[3:4, :]) + cb_ref[...]
    carry_ref[...] = xr0[tm - SUBLANES:tm, :]

    xrb = xr.astype(BF16)
    r_cols, i_cols = [], []
    for hh in range(LRU_HEADS):
        blk = xrb[:, hh * LRU_HEAD_DIM:(hh + 1) * LRU_HEAD_DIM]
        rows = slice(hh * LRU_HEAD_DIM, (hh + 1) * LRU_HEAD_DIM)
        r_cols.append(_dot(blk, wa_ref[rows, :]))
        i_cols.append(_dot(blk, wx_ref[rows, :]))
    sigmoid = lambda v: 0.5 * jnp.tanh(0.5 * v) + 0.5
    r = sigmoid(jnp.concatenate(r_cols, axis=1) + ba_ref[...])
    i = sigmoid(jnp.concatenate(i_cols, axis=1) + bx_ref[...])

    lam = lam_ref[...]
    log_sig = jnp.minimum(lam, 0.0) - jnp.log1p(jnp.exp(-jnp.abs(lam)))
    log_a = (LRU_C * r) * log_sig
    a = jnp.exp(log_a)
    z = -jnp.tanh(log_a) * (a * a + 1.0)
    b = jnp.where(z > 0.0, z * lax.rsqrt(z), 0.0) * (i * xr)

    n_lg = D_RNN // LANES

    def put(scr, val):
        for c in range(n_lg):
            for s in range(SUBLANES):
                scr[c, s * pitch:s * pitch + seg, :] = (
                    val[s * seg:(s + 1) * seg, c * LANES:(c + 1) * LANES])

    def ld(scr, j):
        return jnp.concatenate(
            [scr[c, pl.ds(j, SUBLANES, stride=pitch), :] for c in range(n_lg)],
            axis=1)

    def st(scr, j, val):
        for c in range(n_lg):
            scr[c, pl.ds(j, SUBLANES, stride=pitch), :] = (
                val[:, c * LANES:(c + 1) * LANES])

    put(a_scr, a)
    put(b_scr, b)

    def seg_step(j, carry):
        hl, pl_ = carry
        aj = ld(a_scr, j)
        hl = aj * hl + ld(b_scr, j)
        pl_ = aj * pl_
        st(h_scr, j, hl)
        st(a_scr, j, pl_)
        return hl, pl_

    zeros = jnp.zeros((SUBLANES, D_RNN), F32)
    h_end, p_end = lax.fori_loop(0, seg, seg_step, (zeros, zeros + 1.0))

    def seg_rows(scr, s):
        return jnp.concatenate(
            [scr[c, s * pitch:s * pitch + seg, :] for c in range(n_lg)], axis=1)

    c = hstate_ref[...]
    segs = []
    for s in range(SUBLANES):
        segs.append(seg_rows(h_scr, s) + seg_rows(a_scr, s) * c)
        c = h_end[s:s + 1, :] + p_end[s:s + 1, :] * c
    hstate_ref[...] = c
    hseq = jnp.concatenate(segs, axis=0)
    out = x + _dot((gate * hseq).astype(BF16), wout_ref[...])
    _store_tt(o_ref, out)
    _route(out, first, gf_ref, wr_ref, br_ref, ids_ref, cnt_ref, rcarry_ref,
           before_ref, tm)


def _mixer_rglru(x_tt, moe_out, bsz, t_len, g, w_in, conv_w, conv_b, w_a, b_a,
                 w_x, b_x, lam, w_out, route_args, tm):
    y, dest_tiles = moe_out
    nt = t_len // tm
    n_tiles = bsz * nt
    seg = tm // SUBLANES
    pitch = seg + SUBLANES
    d = D_MODEL
    full = lambda shape: pl.BlockSpec(shape, lambda i: (0,) * len(shape))
    hbm = pl.BlockSpec(memory_space=pl.ANY)
    row = lambda v: v.reshape(1, -1)
    gc_specs, gc_scratch = _gather_combine_specs(n_tiles, tm)
    r_in, r_out, r_shapes, r_scratch = _route_specs(lambda i: i, tm, n_tiles * tm)
    return pl.pallas_call(
        functools.partial(_mixer_rglru_kernel, tm=tm, nt=nt, seg=seg, pitch=pitch),
        grid=(n_tiles,),
        in_specs=gc_specs + [
            full((1, d)),
            hbm,
            full(conv_w.shape),
            full((1, D_RNN)),
            hbm,
            full((1, D_RNN)),
            hbm,
            full((1, D_RNN)),
            full((1, D_RNN)),
            hbm,
        ] + r_in,
        out_specs=[pl.BlockSpec((tm * TT, LANES), lambda i: (i, 0))] + r_out,
        out_shape=[jax.ShapeDtypeStruct(x_tt.shape, F32)] + r_shapes,
        scratch_shapes=gc_scratch + [
            pltpu.VMEM((SUBLANES, D_RNN), F32),
            pltpu.VMEM((1, D_RNN), F32),
            pltpu.VMEM((D_RNN // LANES, SUBLANES * pitch, LANES), F32),
            pltpu.VMEM((D_RNN // LANES, SUBLANES * pitch, LANES), F32),
            pltpu.VMEM((D_RNN // LANES, SUBLANES * pitch, LANES), F32),
            pltpu.VMEM((d, 2 * D_RNN), BF16),
            pltpu.VMEM((D_RNN, LRU_HEAD_DIM), BF16),
            pltpu.VMEM((D_RNN, LRU_HEAD_DIM), BF16),
            pltpu.VMEM((D_RNN, d), BF16),
            pltpu.VMEM((d, STAGE_COLS), F32),
            pltpu.SemaphoreType.DMA(()),
        ] + r_scratch,
        compiler_params=pltpu.CompilerParams(
            dimension_semantics=("arbitrary",),
            vmem_limit_bytes=VMEM_LIMIT),
        name="mixer_rglru",
    )(dest_tiles, dest_tiles, x_tt, y, row(g), w_in, conv_w, row(conv_b),
      w_a.reshape(D_RNN, LRU_HEAD_DIM), row(b_a), w_x.reshape(D_RNN, LRU_HEAD_DIM),
      row(b_x), row(lam), w_out, *route_args)


def _route(x, first, gf_ref, wr_ref, br_ref, ids_ref, cnt_ref, carry_ref,
           before_ref, tm):
    h = _rms(x, gf_ref[...]).astype(BF16)
    lt = lax.dot_general(wr_ref[...], h, (((1,), (1,)), ((), ())),
                         preferred_element_type=F32) + br_ref[...]
    gl = lt[GROUP_LANE:GROUP_LANE + N_GROUPS, :]
    gmax = jnp.max(gl, axis=0, keepdims=True)
    g_iota = lax.broadcasted_iota(jnp.int32, gl.shape, 0)
    g_idx = jnp.min(jnp.where(gl == gmax, g_iota, N_GROUPS), axis=0, keepdims=True)

    esel = lt[EXPERT_LANE:EXPERT_LANE + EXPERTS_PER_GROUP, :]
    for gidx in range(1, N_GROUPS):
        lo = EXPERT_LANE + gidx * EXPERTS_PER_GROUP
        esel = jnp.where(g_idx == gidx, lt[lo:lo + EXPERTS_PER_GROUP, :], esel)
    e_iota = lax.broadcasted_iota(jnp.int32, esel.shape, 0)
    top1 = jnp.max(esel, axis=0, keepdims=True)
    i1 = jnp.min(jnp.where(esel == top1, e_iota, EXPERTS_PER_GROUP), axis=0, keepdims=True)
    rest = jnp.where(e_iota == i1, -jnp.inf, esel)
    top2 = jnp.max(rest, axis=0, keepdims=True)
    i2 = jnp.min(jnp.where(rest == top2, e_iota, EXPERTS_PER_GROUP), axis=0, keepdims=True)

    lo_e = jnp.minimum(i1, i2)
    hi_e = jnp.maximum(i1, i2)
    pair = jnp.right_shift(lo_e * (2 * EXPERTS_PER_GROUP - 1 - lo_e), 1) + (hi_e - lo_e - 1)
    cls = g_idx * N_PAIRS + pair

    c_iota = lax.broadcasted_iota(jnp.int32, (CLS_PAD, tm), 0)
    hit = c_iota == cls
    onehot = jnp.where(hit, 1.0, 0.0)

    @pl.when(first)
    def _():
        carry_ref[...] = jnp.zeros_like(carry_ref)
        s_i = lax.broadcasted_iota(jnp.int32, (tm, tm), 0)
        t_i = lax.broadcasted_iota(jnp.int32, (tm, tm), 1)
        before_ref[...] = jnp.where(s_i < t_i, 1.0, 0.0).astype(BF16)

    prefix = _dot(onehot.astype(BF16), before_ref[...]) + carry_ref[:, 0:1]
    rank = jnp.sum(jnp.where(hit, prefix, 0.0), axis=0, keepdims=True)
    carry_ref[...] = carry_ref[...] + jnp.sum(onehot, axis=1, keepdims=True)

    ids_ref[0:1, :] = cls
    ids_ref[1:2, :] = rank.astype(jnp.int32)
    ids_ref[2:8, :] = jnp.zeros((6, tm), jnp.int32)
    cnt_ref[...] = carry_ref[...].astype(jnp.int32)


def _router_weights(w_rg, b_rg, w_re, b_re):
    wr = jnp.zeros((ROUTER_ROWS, D_MODEL), F32)
    wr = wr.at[GROUP_LANE:GROUP_LANE + N_GROUPS].set(w_rg.T)
    wr = wr.at[EXPERT_LANE:EXPERT_LANE + N_EXPERTS].set(w_re.T)
    br = jnp.zeros((ROUTER_ROWS,), F32)
    br = br.at[GROUP_LANE:GROUP_LANE + N_GROUPS].set(b_rg)
    br = br.at[EXPERT_LANE:EXPERT_LANE + N_EXPERTS].set(b_re)
    return wr, br


def _route_specs(tile_idx, tm, n_tok):
    const = lambda shape: pl.BlockSpec(shape, lambda *_: (0,) * len(shape))
    in_specs = [const((1, D_MODEL)), const((ROUTER_ROWS, D_MODEL)),
                const((ROUTER_ROWS, 1))]
    out_specs = [pl.BlockSpec((8, tm), lambda *g: (0, tile_idx(*g))),
                 const((CLS_PAD, LANES))]
    out_shapes = [jax.ShapeDtypeStruct((8, n_tok), jnp.int32),
                  jax.ShapeDtypeStruct((CLS_PAD, LANES), jnp.int32)]
    scratch = [pltpu.VMEM((CLS_PAD, LANES), F32), pltpu.VMEM((tm, tm), BF16)]
    return in_specs, out_specs, out_shapes, scratch


def _route_args(g, wr, br):
    return g.reshape(1, D_MODEL), wr.astype(BF16), br.reshape(ROUTER_ROWS, 1)


def _dest_kernel(ids_ref, pstart_ref, dest_ref):
    ids = ids_ref[...]
    c_iota = lax.broadcasted_iota(jnp.int32, (CLS_PAD, ids.shape[1]), 0)
    pstart = pstart_ref[:, 0:1]
    hit = c_iota == ids[0:1, :]
    base = jnp.sum(jnp.where(hit, pstart, 0), axis=0, keepdims=True)
    dest_ref[0:1, :] = base + ids[1:2, :]
    dest_ref[1:8, :] = jnp.zeros((7, ids.shape[1]), jnp.int32)


def _dest(ids, pad_starts, t):
    n_tok = ids.shape[1]
    pstart = jnp.broadcast_to(pad_starts[:, None], (CLS_PAD, LANES))
    return pl.pallas_call(
        _dest_kernel,
        grid=(n_tok // t,),
        in_specs=[
            pl.BlockSpec((8, t), lambda i: (0, i)),
            pl.BlockSpec((CLS_PAD, LANES), lambda i: (0, 0)),
        ],
        out_specs=pl.BlockSpec((8, t), lambda i: (0, i)),
        out_shape=jax.ShapeDtypeStruct((8, n_tok), jnp.int32),
        name="moe_dest",
    )(ids, pstart)


def _dispatch_kernel(pstart_ref, pend_ref, dest_ref, x_ref, zero_ref,
                     wg_ref, wu_ref, wd_ref, xs_hbm, wg_out, wu_out, wd_out,
                     sem, zsem, *, tm, blk, n_blocks):
    i = pl.program_id(0)

    @pl.when(i == 0)
    def _():
        def zcopy(c):
            start = pl.multiple_of((pend_ref[c] - blk) * TT, blk * TT)
            return pltpu.make_async_copy(
                zero_ref, xs_hbm.at[pl.ds(start, blk * TT), :], zsem)

        def zstart(c, _):
            @pl.when(pend_ref[c] > pstart_ref[c])
            def _():
                zcopy(c).start()
            return 0

        def zwait(c, _):
            @pl.when(pend_ref[c] > pstart_ref[c])
            def _():
                zcopy(c).wait()
            return 0

        def tcopy(b):
            return pltpu.make_async_copy(
                zero_ref,
                xs_hbm.at[pl.ds(pl.multiple_of(b * (blk * TT), blk * TT), blk * TT), :],
                zsem)

        def tstart(b, _):
            tcopy(b).start()
            return 0

        def twait(b, _):
            tcopy(b).wait()
            return 0

        n_used = pend_ref[N_CLASSES - 1] // blk
        lax.fori_loop(0, N_CLASSES, zstart, 0)
        lax.fori_loop(n_used, n_blocks, tstart, 0)
        lax.fori_loop(0, N_CLASSES, zwait, 0)
        lax.fori_loop(n_used, n_blocks, twait, 0)

    def issue(jj, _):
        for u in range(DMA_UNROLL):
            j = jj * DMA_UNROLL + u
            d = dest_ref[0, 0, j]
            pltpu.make_async_copy(
                x_ref.at[pl.ds(pl.multiple_of(j * TT, TT), TT), :],
                xs_hbm.at[pl.ds(pl.multiple_of(d * TT, TT), TT), :], sem).start()
        return 0

    lax.fori_loop(0, tm // DMA_UNROLL, issue, 0)
    wg_out[0] = wg_ref[0, 0].astype(BF16)
    wu_out[0] = wu_ref[0, 0].astype(BF16)
    wd_out[0] = wd_ref[0, 0].astype(BF16)
    pltpu.make_async_copy(x_ref, xs_hbm.at[pl.ds(0, tm * TT), :], sem).wait()


def _dispatch(x_tt, dest_tiles, pad_starts, pad_ends, w_gate, w_up, w_down, layer,
              n_rows, tm, blk):
    n_tok = x_tt.shape[0] // TT
    n_tiles = n_tok // tm
    assert n_tiles == N_EXPERTS
    zeros = jnp.zeros((blk * TT, LANES), F32)
    w_in = lambda i, ps, pe: (layer, i, 0, 0)
    w_o = lambda i, ps, pe: (i, 0, 0)
    return pl.pallas_call(
        functools.partial(_dispatch_kernel, tm=tm, blk=blk, n_blocks=n_rows // blk),
        grid_spec=pltpu.PrefetchScalarGridSpec(
            num_scalar_prefetch=2,
            grid=(n_tiles,),
            in_specs=[
                pl.BlockSpec((1, 1, tm), lambda i, ps, pe: (i, 0, 0),
                             memory_space=pltpu.SMEM),
                pl.BlockSpec((tm * TT, LANES), lambda i, ps, pe: (i, 0)),
                pl.BlockSpec((blk * TT, LANES), lambda i, ps, pe: (0, 0)),
                pl.BlockSpec((1, 1, D_MODEL, D_EXPERT), w_in),
                pl.BlockSpec((1, 1, D_MODEL, D_EXPERT), w_in),
                pl.BlockSpec((1, 1, D_EXPERT, D_MODEL), w_in),
            ],
            out_specs=[
                pl.BlockSpec(memory_space=pl.ANY),
                pl.BlockSpec((1, D_MODEL, D_EXPERT), w_o),
                pl.BlockSpec((1, D_MODEL, D_EXPERT), w_o),
                pl.BlockSpec((1, D_EXPERT, D_MODEL), w_o),
            ],
            scratch_shapes=[pltpu.SemaphoreType.DMA(()), pltpu.SemaphoreType.DMA(())],
        ),
        out_shape=[
            jax.ShapeDtypeStruct((n_rows * TT, LANES), F32),
            jax.ShapeDtypeStruct((N_EXPERTS, D_MODEL, D_EXPERT), BF16),
            jax.ShapeDtypeStruct((N_EXPERTS, D_MODEL, D_EXPERT), BF16),
            jax.ShapeDtypeStruct((N_EXPERTS, D_EXPERT, D_MODEL), BF16),
        ],
        compiler_params=pltpu.CompilerParams(
            dimension_semantics=("arbitrary",), vmem_limit_bytes=VMEM_LIMIT),
        name="moe_dispatch",
    )(pad_starts, pad_ends, dest_tiles, x_tt, zeros, w_gate, w_up, w_down)


def _expert_kernel(ea_ref, eb_ref, nu_ref, xs_ref, g_ref, wr_ref, br_ref,
                   wga_ref, wua_ref, wda_ref, wgb_ref, wub_ref, wdb_ref, y_ref,
                   *, blk):
    i = pl.program_id(0)
    ea = ea_ref[i]
    eb = eb_ref[i]

    @pl.when(i < nu_ref[0])
    def _():
        x = _load_tt(xs_ref, blk)
        h = _rms(x, g_ref[...]).astype(BF16)

        logits = _dot(h, wr_ref[...]) + br_ref[...]
        lane = lax.broadcasted_iota(jnp.int32, logits.shape, 1)
        is_group = (lane >= GROUP_LANE) & (lane < GROUP_LANE + N_GROUPS)
        pick = lambda l: jnp.sum(jnp.where(lane == l, logits, 0.0), axis=-1, keepdims=True)
        gmax = jnp.max(jnp.where(is_group, logits, -jnp.inf), axis=-1, keepdims=True)
        gsum = jnp.sum(jnp.where(is_group, jnp.exp(logits - gmax), 0.0),
                       axis=-1, keepdims=True)
        g_prob = jnp.exp(pick(GROUP_LANE + ea // EXPERTS_PER_GROUP) - gmax) / gsum
        la = pick(EXPERT_LANE + ea)
        lb = pick(EXPERT_LANE + eb)
        m = jnp.maximum(la, lb)
        pa = jnp.exp(la - m)
        pb = jnp.exp(lb - m)
        scale = g_prob / (pa + pb)
        gate_a = pa * scale
        gate_b = pb * scale

        hid_a = (jax.nn.silu(_dot(h, wga_ref[0])) * _dot(h, wua_ref[0])).astype(BF16)
        hid_b = (jax.nn.silu(_dot(h, wgb_ref[0])) * _dot(h, wub_ref[0])).astype(BF16)
        _store_tt(y_ref, gate_a * _dot(hid_a, wda_ref[0])
                  + gate_b * _dot(hid_b, wdb_ref[0]))

    @pl.when(i >= nu_ref[0])
    def _():
        y_ref[...] = jnp.zeros_like(y_ref)


def _experts(xs, g, wr, br, w_gate, w_up, w_down, block_ea, block_eb, n_used, blk):
    n_blocks = xs.shape[0] // (blk * TT)
    last = lambda i, ea, eb, nu: (jnp.minimum(i, nu[0] - 1), 0)
    const = lambda i, ea, eb, nu: (0, 0)
    w_a = lambda i, ea, eb, nu: (ea[i], 0, 0)
    w_b = lambda i, ea, eb, nu: (eb[i], 0, 0)
    wr2 = jnp.zeros((D_MODEL, LANES), F32).at[:, 0:ROUTER_ROWS].set(wr.T)
    br2 = jnp.zeros((1, LANES), F32).at[0, 0:ROUTER_ROWS].set(br)
    return pl.pallas_call(
        functools.partial(_expert_kernel, blk=blk),
        grid_spec=pltpu.PrefetchScalarGridSpec(
            num_scalar_prefetch=3,
            grid=(n_blocks,),
            in_specs=[
                pl.BlockSpec((blk * TT, LANES), last),
                pl.BlockSpec((1, D_MODEL), const),
                pl.BlockSpec((D_MODEL, LANES), const),
                pl.BlockSpec((1, LANES), const),
                pl.BlockSpec((1, D_MODEL, D_EXPERT), w_a),
                pl.BlockSpec((1, D_MODEL, D_EXPERT), w_a),
                pl.BlockSpec((1, D_EXPERT, D_MODEL), w_a),
                pl.BlockSpec((1, D_MODEL, D_EXPERT), w_b),
                pl.BlockSpec((1, D_MODEL, D_EXPERT), w_b),
                pl.BlockSpec((1, D_EXPERT, D_MODEL), w_b),
            ],
            out_specs=pl.BlockSpec((blk * TT, LANES), lambda i, ea, eb, nu: (i, 0)),
        ),
        out_shape=jax.ShapeDtypeStruct(xs.shape, F32),
        compiler_params=pltpu.CompilerParams(
            dimension_semantics=("arbitrary",),
            vmem_limit_bytes=VMEM_LIMIT),
        name="moe_experts",
    )(block_ea, block_eb, n_used, xs, g.reshape(1, D_MODEL), wr2.astype(BF16), br2,
      w_gate, w_up, w_down, w_gate, w_up, w_down)


def _final_kernel(dcur_ref, dnext_ref, x_ref, y_hbm, gf_ref, o_ref, yg_ref, gsems,
                  *, tm):
    out = _gather_combine(dcur_ref, dnext_ref, x_ref, y_hbm, yg_ref, gsems, tm)
    o_ref[...] = _rms(out, gf_ref[...])


def _final(x_tt, moe_out, g_final, tm):
    y, dest_tiles = moe_out
    n_tok = x_tt.shape[0] // TT
    n_tiles = n_tok // tm
    gc_specs, gc_scratch = _gather_combine_specs(n_tiles, tm)
    return pl.pallas_call(
        functools.partial(_final_kernel, tm=tm),
        grid=(n_tiles,),
        in_specs=gc_specs + [pl.BlockSpec((1, D_MODEL), lambda i: (0, 0))],
        out_specs=pl.BlockSpec((tm, D_MODEL), lambda i: (i, 0)),
        out_shape=jax.ShapeDtypeStruct((n_tok, D_MODEL), F32),
        scratch_shapes=gc_scratch,
        compiler_params=pltpu.CompilerParams(
            dimension_semantics=("arbitrary",),
            vmem_limit_bytes=VMEM_LIMIT),
        name="moe_combine_final",
    )(dest_tiles, dest_tiles, x_tt, y, g_final.reshape(1, D_MODEL))


def _moe(x_tt, ids, cnt, g, wr, br, w_gate, w_up, w_down, layer):
    n_tok = x_tt.shape[0] // TT
    tm, td, blk = _tiles(n_tok)
    n_blocks = n_tok // blk + N_CLASSES

    counts = cnt[:, 0]
    padded = (counts + blk - 1) // blk * blk
    pad_ends = jnp.cumsum(padded).astype(jnp.int32)
    pad_starts = pad_ends - padded
    block_start = jnp.arange(n_blocks, dtype=jnp.int32) * blk
    block_cls = jnp.minimum(
        jnp.sum(pad_ends[None, :N_CLASSES] <= block_start[:, None], axis=1),
        N_CLASSES - 1)
    block_ea = jnp.asarray(CLASS_LO)[block_cls]
    block_eb = jnp.asarray(CLASS_HI)[block_cls]
    n_used = (pad_ends[N_CLASSES - 1:N_CLASSES] // blk).astype(jnp.int32)

    dest = _dest(ids, pad_starts, DEST_TILE if n_tok % DEST_TILE == 0 else tm)
    per_tile = lambda t: dest[0].reshape(n_tok // t, 1, t)

    xs, wg_bf, wu_bf, wd_bf = _dispatch(x_tt, per_tile(td), pad_starts, pad_ends,
                                        w_gate, w_up, w_down, layer,
                                        n_blocks * blk, td, blk)
    y = _experts(xs, g, wr, br, wg_bf, wu_bf, wd_bf, block_ea, block_eb, n_used, blk)
    return y, per_tile(tm)


def kernel(x, norm_mix_g, norm_ffn_g, norm_final_g, ab_w_in, a_ln_g, a_ws, a_ws_b, b_conv_w, ab_w_out, c_w_in, c_conv_w, c_conv_b, c_w_a, c_b_a, c_w_x, c_b_x, c_lambda, c_w_out, moe_w_rg, moe_b_rg, moe_w_re, moe_b_re, moe_w_gate, moe_w_up, moe_w_down):
    bsz, t_len, d = x.shape
    assert d == D_MODEL
    tm, _, _ = _tiles(bsz * t_len)
    assert t_len % tm == 0 and tm % CHUNK == 0

    routers = [_router_weights(moe_w_rg[l], moe_b_rg[l], moe_w_re[l], moe_b_re[l])
               for l in range(2)]
    route_args = [_route_args(norm_ffn_g[l], *routers[l]) for l in range(2)]

    def moe(x_tt, ids, cnt, layer):
        return _moe(x_tt, ids, cnt, norm_ffn_g[layer], *routers[layer],
                    moe_w_gate, moe_w_up, moe_w_down, layer)

    x1, ids, cnt = _mixer_ab(x, norm_mix_g[0], ab_w_in[0], a_ln_g[0], a_ws[0],
                             a_ws_b[0], b_conv_w[0], ab_w_out[0], route_args[0],
                             AB_TILE if t_len % AB_TILE == 0 else tm)
    x3, ids, cnt = _mixer_rglru(x1, moe(x1, ids, cnt, 0), bsz, t_len, norm_mix_g[1],
                                c_w_in[0], c_conv_w[0], c_conv_b[0], c_w_a[0],
                                c_b_a[0], c_w_x[0], c_b_x[0], c_lambda[0],
                                c_w_out[0], route_args[1], tm)
    out = _final(x3, moe(x3, ids, cnt, 1), norm_final_g, tm)
    return out.reshape(bsz, t_len, d)
```

```python
import functools

import numpy as np
import jax
import jax.numpy as jnp
from jax import lax
from jax.experimental import pallas as pl
from jax.experimental.pallas import tpu as pltpu

D_MODEL = 1024
LANES = 128
SUBLANES = 8
TT = D_MODEL // LANES
assert TT == SUBLANES

A_HEADS = 4
A_HEAD_DIM = 128
D_A = A_HEADS * A_HEAD_DIM
CHUNK = 128
D_B = D_MODEL - D_A
B_CONV = 3
D_RNN = D_MODEL
LRU_HEADS = 8
LRU_HEAD_DIM = D_RNN // LRU_HEADS
C_CONV = 4
LRU_C = 8.0
N_GROUPS = 4
EXPERTS_PER_GROUP = 8
N_EXPERTS = N_GROUPS * EXPERTS_PER_GROUP
D_EXPERT = 512
EPS = 1e-6

N_PAIRS = EXPERTS_PER_GROUP * (EXPERTS_PER_GROUP - 1) // 2
N_CLASSES = N_GROUPS * N_PAIRS
CLS_PAD = 128
assert N_CLASSES <= CLS_PAD
GROUP_LANE = 0
EXPERT_LANE = 8
ROUTER_ROWS = 48
VMEM_LIMIT = 56 * 1024 * 1024

BF16 = jnp.bfloat16
F32 = jnp.float32

_PAIRS = [(lo, hi) for lo in range(EXPERTS_PER_GROUP)
          for hi in range(lo + 1, EXPERTS_PER_GROUP)]
CLASS_LO = np.array([g * EXPERTS_PER_GROUP + lo
                     for g in range(N_GROUPS) for lo, _ in _PAIRS], np.int32)
CLASS_HI = np.array([g * EXPERTS_PER_GROUP + hi
                     for g in range(N_GROUPS) for _, hi in _PAIRS], np.int32)


def _tiles(n_tok):
    tm = 512 if n_tok % 512 == 0 else 256
    td = n_tok // N_EXPERTS
    assert td * N_EXPERTS == n_tok and td % DMA_UNROLL == 0
    blk = 256 if n_tok >= 8192 else 128
    return tm, td, blk


DMA_UNROLL = 64
STAGE_COLS = 512
DEST_TILE = 4096
AB_TILE = 1024
SCAN_UNROLL = 8


def _load_tt(ref, nrows):
    return jnp.concatenate(
        [ref[pl.ds(s, nrows, stride=TT), :] for s in range(TT)], axis=1)


def _store_tt(ref, val):
    nrows = val.shape[0]
    for s in range(TT):
        ref[pl.ds(s, nrows, stride=TT), :] = val[:, s * LANES:(s + 1) * LANES]


def _rms(x, g):
    ms = jnp.mean(x * x, axis=-1, keepdims=True)
    return x * lax.rsqrt(ms + EPS) * g


def _dot(a, b):
    return jnp.dot(a, b, preferred_element_type=F32)


def _shift_rows(cur, prev, k):
    rolled = pltpu.roll(cur, k, axis=0)
    row = lax.broadcasted_iota(jnp.int32, prev.shape, 0)
    head = jnp.where(row < k, pltpu.roll(prev, k, axis=0), rolled[0:SUBLANES, :])
    return jnp.concatenate([head, rolled[SUBLANES:, :]], axis=0)


def _stage_bf16(w_hbm, w_bf, stage, sem):
    cols = w_hbm.shape[1]
    cw = min(cols, stage.shape[1])
    for c0 in range(0, cols, cw):
        cp = pltpu.make_async_copy(w_hbm.at[:, pl.ds(c0, cw)],
                                   stage.at[:, pl.ds(0, cw)], sem)
        cp.start()
        cp.wait()
        w_bf[:, c0:c0 + cw] = stage[:, 0:cw].astype(BF16)


def _gather_combine(dcur_ref, dnext_ref, x_ref, y_hbm, yg_ref, sems, tm):
    i = pl.program_id(0)
    slot = i % 2

    def issue(dref, sl):
        def body(jj, _):
            for u in range(DMA_UNROLL):
                j = jj * DMA_UNROLL + u
                d = dref[0, 0, j]
                pltpu.make_async_copy(
                    y_hbm.at[pl.ds(pl.multiple_of(d * TT, TT), TT), :],
                    yg_ref.at[sl, pl.ds(pl.multiple_of(j * TT, TT), TT), :],
                    sems.at[sl]).start()
            return 0

        lax.fori_loop(0, tm // DMA_UNROLL, body, 0)

    @pl.when(i == 0)
    def _():
        issue(dcur_ref, 0)

    @pl.when(i + 1 < pl.num_programs(0))
    def _():
        issue(dnext_ref, 1 - slot)

    pltpu.make_async_copy(y_hbm.at[pl.ds(0, tm * TT), :], yg_ref.at[slot],
                          sems.at[slot]).wait()
    return _load_tt(x_ref, tm) + _load_tt(yg_ref.at[slot], tm)


def _gather_combine_specs(n_tiles, tm):
    in_specs = [
        pl.BlockSpec((1, 1, tm), lambda i: (i, 0, 0), memory_space=pltpu.SMEM),
        pl.BlockSpec((1, 1, tm), lambda i: (jnp.minimum(i + 1, n_tiles - 1), 0, 0),
                     memory_space=pltpu.SMEM),
        pl.BlockSpec((tm * TT, LANES), lambda i: (i, 0)),
        pl.BlockSpec(memory_space=pl.ANY),
    ]
    scratch = [
        pltpu.VMEM((2, tm * TT, LANES), F32),
        pltpu.SemaphoreType.DMA((2,)),
    ]
    return in_specs, scratch


def _mixer_ab_kernel(x_ref, g_ref, win_hbm, lng_ref, ws_ref, wsb_ref, cw_ref,
                     wout_hbm, gf_ref, wr_ref, br_ref, o_ref, ids_ref, cnt_ref,
                     carry_ref, win_ref, wout_ref, stage_ref, wsem,
                     rcarry_ref, before_ref, *, tm):
    t = pl.program_id(1)
    first = (pl.program_id(0) == 0) & (t == 0)

    @pl.when(first)
    def _():
        _stage_bf16(win_hbm, win_ref, stage_ref, wsem)
        _stage_bf16(wout_hbm, wout_ref, stage_ref, wsem)

    x = x_ref[0]
    h = _rms(x, g_ref[...]).astype(BF16)

    u = jax.nn.gelu(_dot(h, win_ref[:, 0:D_A]))
    v = jax.nn.gelu(_dot(h, win_ref[:, D_A:2 * D_A]))
    mu = jnp.mean(v, axis=-1, keepdims=True)
    vc = v - mu
    var = jnp.mean(vc * vc, axis=-1, keepdims=True)
    vn = (vc * lax.rsqrt(var + EPS) * lng_ref[...]).astype(BF16)

    r_i = lax.broadcasted_iota(jnp.int32, (CHUNK, CHUNK), 0)
    c_i = lax.broadcasted_iota(jnp.int32, (CHUNK, CHUNK), 1)
    causal = r_i >= c_i
    head_cols = []
    for hh in range(A_HEADS):
        wsh = jnp.where(causal, ws_ref[hh], 0.0).astype(BF16)
        rows = []
        for c in range(tm // CHUNK):
            blk = vn[c * CHUNK:(c + 1) * CHUNK,
                     hh * A_HEAD_DIM:(hh + 1) * A_HEAD_DIM]
            rows.append(_dot(wsh, blk))
        head_cols.append(jnp.concatenate(rows, axis=0))
    mixed = jnp.concatenate(head_cols, axis=1) + wsb_ref[...]
    y_a = (u * mixed).astype(BF16)

    gate_b = _dot(h, win_ref[:, 2 * D_A:2 * D_A + D_B])
    gate_c = _dot(h, win_ref[:, 2 * D_A + D_B:2 * D_A + 2 * D_B])
    xb = _dot(h, win_ref[:, 2 * D_A + 2 * D_B:2 * D_A + 3 * D_B])
    cx = gate_c * xb

    @pl.when(t == 0)
    def _():
        carry_ref[...] = jnp.zeros_like(carry_ref)

    prev = carry_ref[...]
    conv = (_shift_rows(cx, prev, 2) * cw_ref[0:1, :]
            + _shift_rows(cx, prev, 1) * cw_ref[1:2, :]
            + cx * cw_ref[2:3, :])
    carry_ref[...] = cx[tm - SUBLANES:tm, :]
    y_b = (gate_b * conv).astype(BF16)

    out = x + _dot(y_a, wout_ref[0:D_A, :]) + _dot(y_b, wout_ref[D_A:D_MODEL, :])
    _store_tt(o_ref, out)
    _route(out, first, gf_ref, wr_ref, br_ref, ids_ref, cnt_ref, rcarry_ref,
           before_ref, tm)


def _mixer_ab(x, g, w_in, ln_g, ws, ws_b, conv_w, w_out, route_args, tm):
    bsz, t_len, d = x.shape
    nt = t_len // tm
    wsb_full = jnp.tile(jnp.repeat(ws_b.T, A_HEAD_DIM, axis=1), (tm // CHUNK, 1))
    full = lambda shape: pl.BlockSpec(shape, lambda b, t: (0,) * len(shape))
    hbm = pl.BlockSpec(memory_space=pl.ANY)
    r_in, r_out, r_shapes, r_scratch = _route_specs(lambda b, t: b * nt + t, tm,
                                                    bsz * t_len)
    return pl.pallas_call(
        functools.partial(_mixer_ab_kernel, tm=tm),
        grid=(bsz, nt),
        in_specs=[
            pl.BlockSpec((1, tm, d), lambda b, t: (b, t, 0)),
            full((1, d)),
            hbm,
            full((1, D_A)),
            full(ws.shape),
            full((tm, D_A)),
            full(conv_w.shape),
            hbm,
        ] + r_in,
        out_specs=[pl.BlockSpec((tm * TT, LANES), lambda b, t: (b * nt + t, 0))] + r_out,
        out_shape=[jax.ShapeDtypeStruct((bsz * t_len * TT, LANES), F32)] + r_shapes,
        scratch_shapes=[
            pltpu.VMEM((SUBLANES, D_B), F32),
            pltpu.VMEM(w_in.shape, BF16),
            pltpu.VMEM(w_out.shape, BF16),
            pltpu.VMEM((d, STAGE_COLS), F32),
            pltpu.SemaphoreType.DMA(()),
        ] + r_scratch,
        compiler_params=pltpu.CompilerParams(
            dimension_semantics=("arbitrary", "arbitrary"),
            vmem_limit_bytes=VMEM_LIMIT),
        name="mixer_ab",
    )(x, g.reshape(1, d), w_in, ln_g.reshape(1, D_A), ws, wsb_full, conv_w, w_out,
      *route_args)


def _mixer_rglru_kernel(dcur_ref, dnext_ref, x_ref, y_hbm,
                        g_ref, win_hbm, cw_ref, cb_ref, wa_hbm, ba_ref,
                        wx_hbm, bx_ref, lam_ref, wout_hbm, gf_ref, wr_ref, br_ref,
                        o_ref, ids_ref, cnt_ref,
                        yg_ref, gsems,
                        carry_ref, hstate_ref, a_scr, b_scr, h_scr,
                        win_ref, wa_ref, wx_ref, wout_ref, stage_ref, wsem,
                        rcarry_ref, before_ref,
                        *, tm, nt, seg, pitch):
    t = pl.program_id(0) % nt
    first = pl.program_id(0) == 0

    @pl.when(first)
    def _():
        _stage_bf16(win_hbm, win_ref, stage_ref, wsem)
        _stage_bf16(wa_hbm, wa_ref, stage_ref, wsem)
        _stage_bf16(wx_hbm, wx_ref, stage_ref, wsem)
        _stage_bf16(wout_hbm, wout_ref, stage_ref, wsem)

    x = _gather_combine(dcur_ref, dnext_ref, x_ref, y_hbm, yg_ref, gsems, tm)
    h = _rms(x, g_ref[...]).astype(BF16)
    gate = jax.nn.gelu(_dot(h, win_ref[:, 0:D_RNN]))
    xr0 = _dot(h, win_ref[:, D_RNN:2 * D_RNN])

    @pl.when(t == 0)
    def _():
        carry_ref[...] = jnp.zeros_like(carry_ref)
        hstate_ref[...] = jnp.zeros_like(hstate_ref)

    prev = carry_ref[...]
    xr = (_shift_rows(xr0, prev, 3) * cw_ref[0:1, :]
          + _shift_rows(xr0, prev, 2) * cw_ref[1:2, :]
          + _shift_rows(xr0, prev, 1) * cw_ref[2:3, :]
          + xr0 * cw_ref[3:4, :]) + cb_ref[...]
    carry_ref[...] = xr0[tm - SUBLANES:tm, :]

    xrb = xr.astype(BF16)
    r_cols, i_cols = [], []
    for hh in range(LRU_HEADS):
        blk = xrb[:, hh * LRU_HEAD_DIM:(hh + 1) * LRU_HEAD_DIM]
        rows = slice(hh * LRU_HEAD_DIM, (hh + 1) * LRU_HEAD_DIM)
        r_cols.append(_dot(blk, wa_ref[rows, :]))
        i_cols.append(_dot(blk, wx_ref[rows, :]))
    sigmoid = lambda v: 0.5 * jnp.tanh(0.5 * v) + 0.5
    r = sigmoid(jnp.concatenate(r_cols, axis=1) + ba_ref[...])
    i = sigmoid(jnp.concatenate(i_cols, axis=1) + bx_ref[...])

    lam = lam_ref[...]
    log_sig = jnp.minimum(lam, 0.0) - jnp.log1p(jnp.exp(-jnp.abs(lam)))
    log_a = (LRU_C * r) * log_sig
    a = jnp.exp(log_a)
    z = -jnp.tanh(log_a) * (a * a + 1.0)
    b = jnp.where(z > 0.0, z * lax.rsqrt(z), 0.0) * (i * xr)

    n_lg = D_RNN // LANES

    def put(scr, val):
        for c in range(n_lg):
            for s in range(SUBLANES):
                scr[c, s * pitch:s * pitch + seg, :] = (
                    val[s * seg:(s + 1) * seg, c * LANES:(c + 1) * LANES])

    def ld(scr, j):
        return jnp.concatenate(
            [scr[c, pl.ds(j, SUBLANES, stride=pitch), :] for c in range(n_lg)],
            axis=1)

    def st(scr, j, val):
        for c in range(n_lg):
            scr[c, pl.ds(j, SUBLANES, stride=pitch), :] = (
                val[:, c * LANES:(c + 1) * LANES])

    put(a_scr, a)
    put(b_scr, b)

    def seg_step(j, carry):
        hl, pl_ = carry
        aj = ld(a_scr, j)
        hl = aj * hl + ld(b_scr, j)
        pl_ = aj * pl_
        st(h_scr, j, hl)
        st(a_scr, j, pl_)
        return hl, pl_

    zeros = jnp.zeros((SUBLANES, D_RNN), F32)
    def seg_steps(jj, carry):
        for u in range(SCAN_UNROLL):
            carry = seg_step(jj * SCAN_UNROLL + u, carry)
        return carry

    h_end, p_end = lax.fori_loop(0, seg // SCAN_UNROLL, seg_steps, (zeros, zeros + 1.0))

    def seg_rows(scr, s):
        return jnp.concatenate(
            [scr[c, s * pitch:s * pitch + seg, :] for c in range(n_lg)], axis=1)

    c = hstate_ref[...]
    segs = []
    for s in range(SUBLANES):
        segs.append(seg_rows(h_scr, s) + seg_rows(a_scr, s) * c)
        c = h_end[s:s + 1, :] + p_end[s:s + 1, :] * c
    hstate_ref[...] = c
    hseq = jnp.concatenate(segs, axis=0)
    out = x + _dot((gate * hseq).astype(BF16), wout_ref[...])
    _store_tt(o_ref, out)
    _route(out, first, gf_ref, wr_ref, br_ref, ids_ref, cnt_ref, rcarry_ref,
           before_ref, tm)


def _mixer_rglru(x_tt, moe_out, bsz, t_len, g, w_in, conv_w, conv_b, w_a, b_a,
                 w_x, b_x, lam, w_out, route_args, tm):
    y, dest_tiles = moe_out
    nt = t_len // tm
    n_tiles = bsz * nt
    seg = tm // SUBLANES
    pitch = seg + SUBLANES
    d = D_MODEL
    full = lambda shape: pl.BlockSpec(shape, lambda i: (0,) * len(shape))
    hbm = pl.BlockSpec(memory_space=pl.ANY)
    row = lambda v: v.reshape(1, -1)
    gc_specs, gc_scratch = _gather_combine_specs(n_tiles, tm)
    r_in, r_out, r_shapes, r_scratch = _route_specs(lambda i: i, tm, n_tiles * tm)
    return pl.pallas_call(
        functools.partial(_mixer_rglru_kernel, tm=tm, nt=nt, seg=seg, pitch=pitch),
        grid=(n_tiles,),
        in_specs=gc_specs + [
            full((1, d)),
            hbm,
            full(conv_w.shape),
            full((1, D_RNN)),
            hbm,
            full((1, D_RNN)),
            hbm,
            full((1, D_RNN)),
            full((1, D_RNN)),
            hbm,
        ] + r_in,
        out_specs=[pl.BlockSpec((tm * TT, LANES), lambda i: (i, 0))] + r_out,
        out_shape=[jax.ShapeDtypeStruct(x_tt.shape, F32)] + r_shapes,
        scratch_shapes=gc_scratch + [
            pltpu.VMEM((SUBLANES, D_RNN), F32),
            pltpu.VMEM((1, D_RNN), F32),
            pltpu.VMEM((D_RNN // LANES, SUBLANES * pitch, LANES), F32),
            pltpu.VMEM((D_RNN // LANES, SUBLANES * pitch, LANES), F32),
            pltpu.VMEM((D_RNN // LANES, SUBLANES * pitch, LANES), F32),
            pltpu.VMEM((d, 2 * D_RNN), BF16),
            pltpu.VMEM((D_RNN, LRU_HEAD_DIM), BF16),
            pltpu.VMEM((D_RNN, LRU_HEAD_DIM), BF16),
            pltpu.VMEM((D_RNN, d), BF16),
            pltpu.VMEM((d, STAGE_COLS), F32),
            pltpu.SemaphoreType.DMA(()),
        ] + r_scratch,
        compiler_params=pltpu.CompilerParams(
            dimension_semantics=("arbitrary",),
            vmem_limit_bytes=VMEM_LIMIT),
        name="mixer_rglru",
    )(dest_tiles, dest_tiles, x_tt, y, row(g), w_in, conv_w, row(conv_b),
      w_a.reshape(D_RNN, LRU_HEAD_DIM), row(b_a), w_x.reshape(D_RNN, LRU_HEAD_DIM),
      row(b_x), row(lam), w_out, *route_args)


def _route(x, first, gf_ref, wr_ref, br_ref, ids_ref, cnt_ref, carry_ref,
           before_ref, tm):
    h = _rms(x, gf_ref[...]).astype(BF16)
    lt = lax.dot_general(wr_ref[...], h, (((1,), (1,)), ((), ())),
                         preferred_element_type=F32) + br_ref[...]
    gl = lt[GROUP_LANE:GROUP_LANE + N_GROUPS, :]
    gmax = jnp.max(gl, axis=0, keepdims=True)
    g_iota = lax.broadcasted_iota(jnp.int32, gl.shape, 0)
    g_idx = jnp.min(jnp.where(gl == gmax, g_iota, N_GROUPS), axis=0, keepdims=True)

    esel = lt[EXPERT_LANE:EXPERT_LANE + EXPERTS_PER_GROUP, :]
    for gidx in range(1, N_GROUPS):
        lo = EXPERT_LANE + gidx * EXPERTS_PER_GROUP
        esel = jnp.where(g_idx == gidx, lt[lo:lo + EXPERTS_PER_GROUP, :], esel)
    e_iota = lax.broadcasted_iota(jnp.int32, esel.shape, 0)
    top1 = jnp.max(esel, axis=0, keepdims=True)
    i1 = jnp.min(jnp.where(esel == top1, e_iota, EXPERTS_PER_GROUP), axis=0, keepdims=True)
    rest = jnp.where(e_iota == i1, -jnp.inf, esel)
    top2 = jnp.max(rest, axis=0, keepdims=True)
    i2 = jnp.min(jnp.where(rest == top2, e_iota, EXPERTS_PER_GROUP), axis=0, keepdims=True)

    lo_e = jnp.minimum(i1, i2)
    hi_e = jnp.maximum(i1, i2)
    pair = jnp.right_shift(lo_e * (2 * EXPERTS_PER_GROUP - 1 - lo_e), 1) + (hi_e - lo_e - 1)
    cls = g_idx * N_PAIRS + pair

    c_iota = lax.broadcasted_iota(jnp.int32, (CLS_PAD, tm), 0)
    hit = c_iota == cls
    onehot = jnp.where(hit, 1.0, 0.0)

    @pl.when(first)
    def _():
        carry_ref[...] = jnp.zeros_like(carry_ref)
        s_i = lax.broadcasted_iota(jnp.int32, (tm, tm), 0)
        t_i = lax.broadcasted_iota(jnp.int32, (tm, tm), 1)
        before_ref[...] = jnp.where(s_i < t_i, 1.0, 0.0).astype(BF16)

    prefix = _dot(onehot.astype(BF16), before_ref[...]) + carry_ref[:, 0:1]
    rank = jnp.sum(jnp.where(hit, prefix, 0.0), axis=0, keepdims=True)
    carry_ref[...] = carry_ref[...] + jnp.sum(onehot, axis=1, keepdims=True)

    ids_ref[0:1, :] = cls
    ids_ref[1:2, :] = rank.astype(jnp.int32)
    ids_ref[2:8, :] = jnp.zeros((6, tm), jnp.int32)
    cnt_ref[...] = carry_ref[...].astype(jnp.int32)


def _router_weights(w_rg, b_rg, w_re, b_re):
    wr = jnp.zeros((ROUTER_ROWS, D_MODEL), F32)
    wr = wr.at[GROUP_LANE:GROUP_LANE + N_GROUPS].set(w_rg.T)
    wr = wr.at[EXPERT_LANE:EXPERT_LANE + N_EXPERTS].set(w_re.T)
    br = jnp.zeros((ROUTER_ROWS,), F32)
    br = br.at[GROUP_LANE:GROUP_LANE + N_GROUPS].set(b_rg)
    br = br.at[EXPERT_LANE:EXPERT_LANE + N_EXPERTS].set(b_re)
    return wr, br


def _route_specs(tile_idx, tm, n_tok):
    const = lambda shape: pl.BlockSpec(shape, lambda *_: (0,) * len(shape))
    in_specs = [const((1, D_MODEL)), const((ROUTER_ROWS, D_MODEL)),
                const((ROUTER_ROWS, 1))]
    out_specs = [pl.BlockSpec((8, tm), lambda *g: (0, tile_idx(*g))),
                 const((CLS_PAD, LANES))]
    out_shapes = [jax.ShapeDtypeStruct((8, n_tok), jnp.int32),
                  jax.ShapeDtypeStruct((CLS_PAD, LANES), jnp.int32)]
    scratch = [pltpu.VMEM((CLS_PAD, LANES), F32), pltpu.VMEM((tm, tm), BF16)]
    return in_specs, out_specs, out_shapes, scratch


def _route_args(g, wr, br):
    return g.reshape(1, D_MODEL), wr.astype(BF16), br.reshape(ROUTER_ROWS, 1)


def _dest_kernel(ids_ref, pstart_ref, dest_ref):
    ids = ids_ref[...]
    c_iota = lax.broadcasted_iota(jnp.int32, (CLS_PAD, ids.shape[1]), 0)
    pstart = pstart_ref[:, 0:1]
    hit = c_iota == ids[0:1, :]
    base = jnp.sum(jnp.where(hit, pstart, 0), axis=0, keepdims=True)
    dest_ref[0:1, :] = base + ids[1:2, :]
    dest_ref[1:8, :] = jnp.zeros((7, ids.shape[1]), jnp.int32)


def _dest(ids, pad_starts, t):
    n_tok = ids.shape[1]
    pstart = jnp.broadcast_to(pad_starts[:, None], (CLS_PAD, LANES))
    return pl.pallas_call(
        _dest_kernel,
        grid=(n_tok // t,),
        in_specs=[
            pl.BlockSpec((8, t), lambda i: (0, i)),
            pl.BlockSpec((CLS_PAD, LANES), lambda i: (0, 0)),
        ],
        out_specs=pl.BlockSpec((8, t), lambda i: (0, i)),
        out_shape=jax.ShapeDtypeStruct((8, n_tok), jnp.int32),
        name="moe_dest",
    )(ids, pstart)


def _dispatch_kernel(pstart_ref, pend_ref, dest_ref, x_ref, zero_ref,
                     wg_ref, wu_ref, wd_ref, xs_hbm, wg_out, wu_out, wd_out,
                     sem, zsem, *, tm, blk, n_blocks):
    i = pl.program_id(0)

    @pl.when(i == 0)
    def _():
        def zcopy(c):
            start = pl.multiple_of((pend_ref[c] - blk) * TT, blk * TT)
            return pltpu.make_async_copy(
                zero_ref, xs_hbm.at[pl.ds(start, blk * TT), :], zsem)

        def zstart(c, _):
            @pl.when(pend_ref[c] > pstart_ref[c])
            def _():
                zcopy(c).start()
            return 0

        def zwait(c, _):
            @pl.when(pend_ref[c] > pstart_ref[c])
            def _():
                zcopy(c).wait()
            return 0

        def tcopy(b):
            return pltpu.make_async_copy(
                zero_ref,
                xs_hbm.at[pl.ds(pl.multiple_of(b * (blk * TT), blk * TT), blk * TT), :],
                zsem)

        def tstart(b, _):
            tcopy(b).start()
            return 0

        def twait(b, _):
            tcopy(b).wait()
            return 0

        n_used = pend_ref[N_CLASSES - 1] // blk
        lax.fori_loop(0, N_CLASSES, zstart, 0)
        lax.fori_loop(n_used, n_blocks, tstart, 0)
        lax.fori_loop(0, N_CLASSES, zwait, 0)
        lax.fori_loop(n_used, n_blocks, twait, 0)

    def issue(jj, _):
        for u in range(DMA_UNROLL):
            j = jj * DMA_UNROLL + u
            d = dest_ref[0, 0, j]
            pltpu.make_async_copy(
                x_ref.at[pl.ds(pl.multiple_of(j * TT, TT), TT), :],
                xs_hbm.at[pl.ds(pl.multiple_of(d * TT, TT), TT), :], sem).start()
        return 0

    lax.fori_loop(0, tm // DMA_UNROLL, issue, 0)
    wg_out[0] = wg_ref[0, 0].astype(BF16)
    wu_out[0] = wu_ref[0, 0].astype(BF16)
    wd_out[0] = wd_ref[0, 0].astype(BF16)
    pltpu.make_async_copy(x_ref, xs_hbm.at[pl.ds(0, tm * TT), :], sem).wait()


def _dispatch(x_tt, dest_tiles, pad_starts, pad_ends, w_gate, w_up, w_down, layer,
              n_rows, tm, blk):
    n_tok = x_tt.shape[0] // TT
    n_tiles = n_tok // tm
    assert n_tiles == N_EXPERTS
    zeros = jnp.zeros((blk * TT, LANES), F32)
    w_in = lambda i, ps, pe: (layer, i, 0, 0)
    w_o = lambda i, ps, pe: (i, 0, 0)
    return pl.pallas_call(
        functools.partial(_dispatch_kernel, tm=tm, blk=blk, n_blocks=n_rows // blk),
        grid_spec=pltpu.PrefetchScalarGridSpec(
            num_scalar_prefetch=2,
            grid=(n_tiles,),
            in_specs=[
                pl.BlockSpec((1, 1, tm), lambda i, ps, pe: (i, 0, 0),
                             memory_space=pltpu.SMEM),
                pl.BlockSpec((tm * TT, LANES), lambda i, ps, pe: (i, 0)),
                pl.BlockSpec((blk * TT, LANES), lambda i, ps, pe: (0, 0)),
                pl.BlockSpec((1, 1, D_MODEL, D_EXPERT), w_in),
                pl.BlockSpec((1, 1, D_MODEL, D_EXPERT), w_in),
                pl.BlockSpec((1, 1, D_EXPERT, D_MODEL), w_in),
            ],
            out_specs=[
                pl.BlockSpec(memory_space=pl.ANY),
                pl.BlockSpec((1, D_MODEL, D_EXPERT), w_o),
                pl.BlockSpec((1, D_MODEL, D_EXPERT), w_o),
                pl.BlockSpec((1, D_EXPERT, D_MODEL), w_o),
            ],
            scratch_shapes=[pltpu.SemaphoreType.DMA(()), pltpu.SemaphoreType.DMA(())],
        ),
        out_shape=[
            jax.ShapeDtypeStruct((n_rows * TT, LANES), F32),
            jax.ShapeDtypeStruct((N_EXPERTS, D_MODEL, D_EXPERT), BF16),
            jax.ShapeDtypeStruct((N_EXPERTS, D_MODEL, D_EXPERT), BF16),
            jax.ShapeDtypeStruct((N_EXPERTS, D_EXPERT, D_MODEL), BF16),
        ],
        compiler_params=pltpu.CompilerParams(
            dimension_semantics=("arbitrary",), vmem_limit_bytes=VMEM_LIMIT),
        name="moe_dispatch",
    )(pad_starts, pad_ends, dest_tiles, x_tt, zeros, w_gate, w_up, w_down)


def _expert_kernel(ea_ref, eb_ref, nu_ref, xs_ref, g_ref, wr_ref, br_ref,
                   wga_ref, wua_ref, wda_ref, wgb_ref, wub_ref, wdb_ref, y_ref,
                   *, blk):
    i = pl.program_id(0)
    ea = ea_ref[i]
    eb = eb_ref[i]

    @pl.when(i < nu_ref[0])
    def _():
        x = _load_tt(xs_ref, blk)
        h = _rms(x, g_ref[...]).astype(BF16)

        logits = _dot(h, wr_ref[...]) + br_ref[...]
        lane = lax.broadcasted_iota(jnp.int32, logits.shape, 1)
        is_group = (lane >= GROUP_LANE) & (lane < GROUP_LANE + N_GROUPS)
        pick = lambda l: jnp.sum(jnp.where(lane == l, logits, 0.0), axis=-1, keepdims=True)
        gmax = jnp.max(jnp.where(is_group, logits, -jnp.inf), axis=-1, keepdims=True)
        gsum = jnp.sum(jnp.where(is_group, jnp.exp(logits - gmax), 0.0),
                       axis=-1, keepdims=True)
        g_prob = jnp.exp(pick(GROUP_LANE + ea // EXPERTS_PER_GROUP) - gmax) / gsum
        la = pick(EXPERT_LANE + ea)
        lb = pick(EXPERT_LANE + eb)
        m = jnp.maximum(la, lb)
        pa = jnp.exp(la - m)
        pb = jnp.exp(lb - m)
        scale = g_prob / (pa + pb)
        gate_a = pa * scale
        gate_b = pb * scale

        hid_a = (jax.nn.silu(_dot(h, wga_ref[0])) * _dot(h, wua_ref[0])).astype(BF16)
        hid_b = (jax.nn.silu(_dot(h, wgb_ref[0])) * _dot(h, wub_ref[0])).astype(BF16)
        _store_tt(y_ref, gate_a * _dot(hid_a, wda_ref[0])
                  + gate_b * _dot(hid_b, wdb_ref[0]))

    @pl.when(i >= nu_ref[0])
    def _():
        y_ref[...] = jnp.zeros_like(y_ref)


def _experts(xs, g, wr, br, w_gate, w_up, w_down, block_ea, block_eb, n_used, blk):
    n_blocks = xs.shape[0] // (blk * TT)
    last = lambda i, ea, eb, nu: (jnp.minimum(i, nu[0] - 1), 0)
    const = lambda i, ea, eb, nu: (0, 0)
    w_a = lambda i, ea, eb, nu: (ea[i], 0, 0)
    w_b = lambda i, ea, eb, nu: (eb[i], 0, 0)
    wr2 = jnp.zeros((D_MODEL, LANES), F32).at[:, 0:ROUTER_ROWS].set(wr.T)
    br2 = jnp.zeros((1, LANES), F32).at[0, 0:ROUTER_ROWS].set(br)
    return pl.pallas_call(
        functools.partial(_expert_kernel, blk=blk),
        grid_spec=pltpu.PrefetchScalarGridSpec(
            num_scalar_prefetch=3,
            grid=(n_blocks,),
            in_specs=[
                pl.BlockSpec((blk * TT, LANES), last),
                pl.BlockSpec((1, D_MODEL), const),
                pl.BlockSpec((D_MODEL, LANES), const),
                pl.BlockSpec((1, LANES), const),
                pl.BlockSpec((1, D_MODEL, D_EXPERT), w_a),
                pl.BlockSpec((1, D_MODEL, D_EXPERT), w_a),
                pl.BlockSpec((1, D_EXPERT, D_MODEL), w_a),
                pl.BlockSpec((1, D_MODEL, D_EXPERT), w_b),
                pl.BlockSpec((1, D_MODEL, D_EXPERT), w_b),
                pl.BlockSpec((1, D_EXPERT, D_MODEL), w_b),
            ],
            out_specs=pl.BlockSpec((blk * TT, LANES), lambda i, ea, eb, nu: (i, 0)),
        ),
        out_shape=jax.ShapeDtypeStruct(xs.shape, F32),
        compiler_params=pltpu.CompilerParams(
            dimension_semantics=("arbitrary",),
            vmem_limit_bytes=VMEM_LIMIT),
        name="moe_experts",
    )(block_ea, block_eb, n_used, xs, g.reshape(1, D_MODEL), wr2.astype(BF16), br2,
      w_gate, w_up, w_down, w_gate, w_up, w_down)


def _final_kernel(dcur_ref, dnext_ref, x_ref, y_hbm, gf_ref, o_ref, yg_ref, gsems,
                  *, tm):
    out = _gather_combine(dcur_ref, dnext_ref, x_ref, y_hbm, yg_ref, gsems, tm)
    o_ref[...] = _rms(out, gf_ref[...])


def _final(x_tt, moe_out, g_final, tm):
    y, dest_tiles = moe_out
    n_tok = x_tt.shape[0] // TT
    n_tiles = n_tok // tm
    gc_specs, gc_scratch = _gather_combine_specs(n_tiles, tm)
    return pl.pallas_call(
        functools.partial(_final_kernel, tm=tm),
        grid=(n_tiles,),
        in_specs=gc_specs + [pl.BlockSpec((1, D_MODEL), lambda i: (0, 0))],
        out_specs=pl.BlockSpec((tm, D_MODEL), lambda i: (i, 0)),
        out_shape=jax.ShapeDtypeStruct((n_tok, D_MODEL), F32),
        scratch_shapes=gc_scratch,
        compiler_params=pltpu.CompilerParams(
            dimension_semantics=("arbitrary",),
            vmem_limit_bytes=VMEM_LIMIT),
        name="moe_combine_final",
    )(dest_tiles, dest_tiles, x_tt, y, g_final.reshape(1, D_MODEL))


def _moe(x_tt, ids, cnt, g, wr, br, w_gate, w_up, w_down, layer):
    n_tok = x_tt.shape[0] // TT
    tm, td, blk = _tiles(n_tok)
    n_blocks = n_tok // blk + N_CLASSES

    counts = cnt[:, 0]
    padded = (counts + blk - 1) // blk * blk
    pad_ends = jnp.cumsum(padded).astype(jnp.int32)
    pad_starts = pad_ends - padded
    block_start = jnp.arange(n_blocks, dtype=jnp.int32) * blk
    block_cls = jnp.minimum(
        jnp.sum(pad_ends[None, :N_CLASSES] <= block_start[:, None], axis=1),
        N_CLASSES - 1)
    block_ea = jnp.asarray(CLASS_LO)[block_cls]
    block_eb = jnp.asarray(CLASS_HI)[block_cls]
    n_used = (pad_ends[N_CLASSES - 1:N_CLASSES] // blk).astype(jnp.int32)

    dest = _dest(ids, pad_starts, DEST_TILE if n_tok % DEST_TILE == 0 else tm)
    per_tile = lambda t: dest[0].reshape(n_tok // t, 1, t)

    xs, wg_bf, wu_bf, wd_bf = _dispatch(x_tt, per_tile(td), pad_starts, pad_ends,
                                        w_gate, w_up, w_down, layer,
                                        n_blocks * blk, td, blk)
    y = _experts(xs, g, wr, br, wg_bf, wu_bf, wd_bf, block_ea, block_eb, n_used, blk)
    return y, per_tile(tm)


def kernel(x, norm_mix_g, norm_ffn_g, norm_final_g, ab_w_in, a_ln_g, a_ws, a_ws_b, b_conv_w, ab_w_out, c_w_in, c_conv_w, c_conv_b, c_w_a, c_b_a, c_w_x, c_b_x, c_lambda, c_w_out, moe_w_rg, moe_b_rg, moe_w_re, moe_b_re, moe_w_gate, moe_w_up, moe_w_down):
    bsz, t_len, d = x.shape
    assert d == D_MODEL
    tm, _, _ = _tiles(bsz * t_len)
    assert t_len % tm == 0 and tm % CHUNK == 0

    routers = [_router_weights(moe_w_rg[l], moe_b_rg[l], moe_w_re[l], moe_b_re[l])
               for l in range(2)]
    route_args = [_route_args(norm_ffn_g[l], *routers[l]) for l in range(2)]

    def moe(x_tt, ids, cnt, layer):
        return _moe(x_tt, ids, cnt, norm_ffn_g[layer], *routers[layer],
                    moe_w_gate, moe_w_up, moe_w_down, layer)

    x1, ids, cnt = _mixer_ab(x, norm_mix_g[0], ab_w_in[0], a_ln_g[0], a_ws[0],
                             a_ws_b[0], b_conv_w[0], ab_w_out[0], route_args[0],
                             AB_TILE if t_len % AB_TILE == 0 else tm)
    x3, ids, cnt = _mixer_rglru(x1, moe(x1, ids, cnt, 0), bsz, t_len, norm_mix_g[1],
                                c_w_in[0], c_conv_w[0], c_conv_b[0], c_w_a[0],
                                c_b_a[0], c_w_x[0], c_b_x[0], c_lambda[0],
                                c_w_out[0], route_args[1], tm)
    out = _final(x3, moe(x3, ids, cnt, 1), norm_final_g, tm)
    return out.reshape(bsz, t_len, d)
```

```python
import functools

import numpy as np
import jax
import jax.numpy as jnp
from jax import lax
from jax.experimental import pallas as pl
from jax.experimental.pallas import tpu as pltpu

D_MODEL = 1024
LANES = 128
SUBLANES = 8
TT = D_MODEL // LANES
assert TT == SUBLANES

A_HEADS = 4
A_HEAD_DIM = 128
D_A = A_HEADS * A_HEAD_DIM
CHUNK = 128
D_B = D_MODEL - D_A
B_CONV = 3
D_RNN = D_MODEL
LRU_HEADS = 8
LRU_HEAD_DIM = D_RNN // LRU_HEADS
C_CONV = 4
LRU_C = 8.0
N_GROUPS = 4
EXPERTS_PER_GROUP = 8
N_EXPERTS = N_GROUPS * EXPERTS_PER_GROUP
D_EXPERT = 512
EPS = 1e-6

N_PAIRS = EXPERTS_PER_GROUP * (EXPERTS_PER_GROUP - 1) // 2
N_CLASSES = N_GROUPS * N_PAIRS
CLS_PAD = 128
assert N_CLASSES <= CLS_PAD
GROUP_LANE = 0
EXPERT_LANE = 8
ROUTER_ROWS = 48
VMEM_LIMIT = 56 * 1024 * 1024

BF16 = jnp.bfloat16
F32 = jnp.float32

_PAIRS = [(lo, hi) for lo in range(EXPERTS_PER_GROUP)
          for hi in range(lo + 1, EXPERTS_PER_GROUP)]
CLASS_LO = np.array([g * EXPERTS_PER_GROUP + lo
                     for g in range(N_GROUPS) for lo, _ in _PAIRS], np.int32)
CLASS_HI = np.array([g * EXPERTS_PER_GROUP + hi
                     for g in range(N_GROUPS) for _, hi in _PAIRS], np.int32)


def _tiles(n_tok):
    tm = 512 if n_tok % 512 == 0 else 256
    td = n_tok // N_EXPERTS
    assert td * N_EXPERTS == n_tok and td % DMA_UNROLL == 0
    blk = 256 if n_tok >= 8192 else 128
    return tm, td, blk


DMA_UNROLL = 64
STAGE_COLS = 512
DEST_TILE = 4096
AB_TILE = 1024
SCAN_UNROLL = 8
EXPERT_UNITS = 2


def _load_tt(ref, nrows, row0=0):
    return jnp.concatenate(
        [ref[pl.ds(row0 * TT + s, nrows, stride=TT), :] for s in range(TT)], axis=1)


def _store_tt(ref, val, row0=0):
    nrows = val.shape[0]
    for s in range(TT):
        ref[pl.ds(row0 * TT + s, nrows, stride=TT), :] = val[:, s * LANES:(s + 1) * LANES]


def _rms(x, g):
    ms = jnp.mean(x * x, axis=-1, keepdims=True)
    return x * lax.rsqrt(ms + EPS) * g


def _dot(a, b):
    return jnp.dot(a, b, preferred_element_type=F32)


def _shift_rows(cur, prev, k):
    rolled = pltpu.roll(cur, k, axis=0)
    row = lax.broadcasted_iota(jnp.int32, prev.shape, 0)
    head = jnp.where(row < k, pltpu.roll(prev, k, axis=0), rolled[0:SUBLANES, :])
    return jnp.concatenate([head, rolled[SUBLANES:, :]], axis=0)


def _stage_bf16(w_hbm, w_bf, stage, sem):
    cols = w_hbm.shape[1]
    cw = min(cols, stage.shape[1])
    for c0 in range(0, cols, cw):
        cp = pltpu.make_async_copy(w_hbm.at[:, pl.ds(c0, cw)],
                                   stage.at[:, pl.ds(0, cw)], sem)
        cp.start()
        cp.wait()
        w_bf[:, c0:c0 + cw] = stage[:, 0:cw].astype(BF16)


def _gather_combine(dcur_ref, dnext_ref, x_ref, y_hbm, yg_ref, sems, tm):
    i = pl.program_id(0)
    slot = i % 2

    def issue(dref, sl):
        def body(jj, _):
            for u in range(DMA_UNROLL):
                j = jj * DMA_UNROLL + u
                d = dref[0, 0, j]
                pltpu.make_async_copy(
                    y_hbm.at[pl.ds(pl.multiple_of(d * TT, TT), TT), :],
                    yg_ref.at[sl, pl.ds(pl.multiple_of(j * TT, TT), TT), :],
                    sems.at[sl]).start()
            return 0

        lax.fori_loop(0, tm // DMA_UNROLL, body, 0)

    @pl.when(i == 0)
    def _():
        issue(dcur_ref, 0)

    @pl.when(i + 1 < pl.num_programs(0))
    def _():
        issue(dnext_ref, 1 - slot)

    pltpu.make_async_copy(y_hbm.at[pl.ds(0, tm * TT), :], yg_ref.at[slot],
                          sems.at[slot]).wait()
    return _load_tt(x_ref, tm) + _load_tt(yg_ref.at[slot], tm)


def _gather_combine_specs(n_tiles, tm):
    in_specs = [
        pl.BlockSpec((1, 1, tm), lambda i: (i, 0, 0), memory_space=pltpu.SMEM),
        pl.BlockSpec((1, 1, tm), lambda i: (jnp.minimum(i + 1, n_tiles - 1), 0, 0),
                     memory_space=pltpu.SMEM),
        pl.BlockSpec((tm * TT, LANES), lambda i: (i, 0)),
        pl.BlockSpec(memory_space=pl.ANY),
    ]
    scratch = [
        pltpu.VMEM((2, tm * TT, LANES), F32),
        pltpu.SemaphoreType.DMA((2,)),
    ]
    return in_specs, scratch


def _mixer_ab_kernel(x_ref, g_ref, win_hbm, lng_ref, ws_ref, wsb_ref, cw_ref,
                     wout_hbm, gf_ref, wr_ref, br_ref, o_ref, ids_ref, cnt_ref,
                     carry_ref, win_ref, wout_ref, stage_ref, wsem,
                     rcarry_ref, before_ref, *, tm):
    t = pl.program_id(1)
    first = (pl.program_id(0) == 0) & (t == 0)

    @pl.when(first)
    def _():
        _stage_bf16(win_hbm, win_ref, stage_ref, wsem)
        _stage_bf16(wout_hbm, wout_ref, stage_ref, wsem)

    x = x_ref[0]
    h = _rms(x, g_ref[...]).astype(BF16)

    u = jax.nn.gelu(_dot(h, win_ref[:, 0:D_A]))
    v = jax.nn.gelu(_dot(h, win_ref[:, D_A:2 * D_A]))
    mu = jnp.mean(v, axis=-1, keepdims=True)
    vc = v - mu
    var = jnp.mean(vc * vc, axis=-1, keepdims=True)
    vn = (vc * lax.rsqrt(var + EPS) * lng_ref[...]).astype(BF16)

    r_i = lax.broadcasted_iota(jnp.int32, (CHUNK, CHUNK), 0)
    c_i = lax.broadcasted_iota(jnp.int32, (CHUNK, CHUNK), 1)
    causal = r_i >= c_i
    head_cols = []
    for hh in range(A_HEADS):
        wsh = jnp.where(causal, ws_ref[hh], 0.0).astype(BF16)
        rows = []
        for c in range(tm // CHUNK):
            blk = vn[c * CHUNK:(c + 1) * CHUNK,
                     hh * A_HEAD_DIM:(hh + 1) * A_HEAD_DIM]
            rows.append(_dot(wsh, blk))
        head_cols.append(jnp.concatenate(rows, axis=0))
    mixed = jnp.concatenate(head_cols, axis=1) + wsb_ref[...]
    y_a = (u * mixed).astype(BF16)

    gate_b = _dot(h, win_ref[:, 2 * D_A:2 * D_A + D_B])
    gate_c = _dot(h, win_ref[:, 2 * D_A + D_B:2 * D_A + 2 * D_B])
    xb = _dot(h, win_ref[:, 2 * D_A + 2 * D_B:2 * D_A + 3 * D_B])
    cx = gate_c * xb

    @pl.when(t == 0)
    def _():
        carry_ref[...] = jnp.zeros_like(carry_ref)

    prev = carry_ref[...]
    conv = (_shift_rows(cx, prev, 2) * cw_ref[0:1, :]
            + _shift_rows(cx, prev, 1) * cw_ref[1:2, :]
            + cx * cw_ref[2:3, :])
    carry_ref[...] = cx[tm - SUBLANES:tm, :]
    y_b = (gate_b * conv).astype(BF16)

    out = x + _dot(y_a, wout_ref[0:D_A, :]) + _dot(y_b, wout_ref[D_A:D_MODEL, :])
    _store_tt(o_ref, out)
    _route(out, first, gf_ref, wr_ref, br_ref, ids_ref, cnt_ref, rcarry_ref,
           before_ref, tm)


def _mixer_ab(x, g, w_in, ln_g, ws, ws_b, conv_w, w_out, route_args, tm):
    bsz, t_len, d = x.shape
    nt = t_len // tm
    wsb_full = jnp.tile(jnp.repeat(ws_b.T, A_HEAD_DIM, axis=1), (tm // CHUNK, 1))
    full = lambda shape: pl.BlockSpec(shape, lambda b, t: (0,) * len(shape))
    hbm = pl.BlockSpec(memory_space=pl.ANY)
    r_in, r_out, r_shapes, r_scratch = _route_specs(lambda b, t: b * nt + t, tm,
                                                    bsz * t_len)
    return pl.pallas_call(
        functools.partial(_mixer_ab_kernel, tm=tm),
        grid=(bsz, nt),
        in_specs=[
            pl.BlockSpec((1, tm, d), lambda b, t: (b, t, 0)),
            full((1, d)),
            hbm,
            full((1, D_A)),
            full(ws.shape),
            full((tm, D_A)),
            full(conv_w.shape),
            hbm,
        ] + r_in,
        out_specs=[pl.BlockSpec((tm * TT, LANES), lambda b, t: (b * nt + t, 0))] + r_out,
        out_shape=[jax.ShapeDtypeStruct((bsz * t_len * TT, LANES), F32)] + r_shapes,
        scratch_shapes=[
            pltpu.VMEM((SUBLANES, D_B), F32),
            pltpu.VMEM(w_in.shape, BF16),
            pltpu.VMEM(w_out.shape, BF16),
            pltpu.VMEM((d, STAGE_COLS), F32),
            pltpu.SemaphoreType.DMA(()),
        ] + r_scratch,
        compiler_params=pltpu.CompilerParams(
            dimension_semantics=("arbitrary", "arbitrary"),
            vmem_limit_bytes=VMEM_LIMIT),
        name="mixer_ab",
    )(x, g.reshape(1, d), w_in, ln_g.reshape(1, D_A), ws, wsb_full, conv_w, w_out,
      *route_args)


def _mixer_rglru_kernel(dcur_ref, dnext_ref, x_ref, y_hbm,
                        g_ref, win_hbm, cw_ref, cb_ref, wa_hbm, ba_ref,
                        wx_hbm, bx_ref, lam_ref, wout_hbm, gf_ref, wr_ref, br_ref,
                        o_ref, ids_ref, cnt_ref,
                        yg_ref, gsems,
                        carry_ref, hstate_ref, a_scr, b_scr, h_scr,
                        win_ref, wa_ref, wx_ref, wout_ref, stage_ref, wsem,
                        rcarry_ref, before_ref,
                        *, tm, nt, seg, pitch):
    t = pl.program_id(0) % nt
    first = pl.program_id(0) == 0

    @pl.when(first)
    def _():
        _stage_bf16(win_hbm, win_ref, stage_ref, wsem)
        _stage_bf16(wa_hbm, wa_ref, stage_ref, wsem)
        _stage_bf16(wx_hbm, wx_ref, stage_ref, wsem)
        _stage_bf16(wout_hbm, wout_ref, stage_ref, wsem)

    x = _gather_combine(dcur_ref, dnext_ref, x_ref, y_hbm, yg_ref, gsems, tm)
    h = _rms(x, g_ref[...]).astype(BF16)
    gate = jax.nn.gelu(_dot(h, win_ref[:, 0:D_RNN]))
    xr0 = _dot(h, win_ref[:, D_RNN:2 * D_RNN])

    @pl.when(t == 0)
    def _():
        carry_ref[...] = jnp.zeros_like(carry_ref)
        hstate_ref[...] = jnp.zeros_like(hstate_ref)

    prev = carry_ref[...]
    xr = (_shift_rows(xr0, prev, 3) * cw_ref[0:1, :]
          + _shift_rows(xr0, prev, 2) * cw_ref[1:2, :]
          + _shift_rows(xr0, prev, 1) * cw_ref[2:3, :]
          + xr0 * cw_ref[3:4, :]) + cb_ref[...]
    carry_ref[...] = xr0[tm - SUBLANES:tm, :]

    xrb = xr.astype(BF16)
    r_cols, i_cols = [], []
    for hh in range(LRU_HEADS):
        blk = xrb[:, hh * LRU_HEAD_DIM:(hh + 1) * LRU_HEAD_DIM]
        rows = slice(hh * LRU_HEAD_DIM, (hh + 1) * LRU_HEAD_DIM)
        r_cols.append(_dot(blk, wa_ref[rows, :]))
        i_cols.append(_dot(blk, wx_ref[rows, :]))
    sigmoid = lambda v: 0.5 * jnp.tanh(0.5 * v) + 0.5
    r = sigmoid(jnp.concatenate(r_cols, axis=1) + ba_ref[...])
    i = sigmoid(jnp.concatenate(i_cols, axis=1) + bx_ref[...])

    lam = lam_ref[...]
    log_sig = jnp.minimum(lam, 0.0) - jnp.log1p(jnp.exp(-jnp.abs(lam)))
    log_a = (LRU_C * r) * log_sig
    a = jnp.exp(log_a)
    z = -jnp.tanh(log_a) * (a * a + 1.0)
    b = jnp.where(z > 0.0, z * lax.rsqrt(z), 0.0) * (i * xr)

    n_lg = D_RNN // LANES

    def put(scr, val):
        for c in range(n_lg):
            for s in range(SUBLANES):
                scr[c, s * pitch:s * pitch + seg, :] = (
                    val[s * seg:(s + 1) * seg, c * LANES:(c + 1) * LANES])

    def ld(scr, j):
        return jnp.concatenate(
            [scr[c, pl.ds(j, SUBLANES, stride=pitch), :] for c in range(n_lg)],
            axis=1)

    def st(scr, j, val):
        for c in range(n_lg):
            scr[c, pl.ds(j, SUBLANES, stride=pitch), :] = (
                val[:, c * LANES:(c + 1) * LANES])

    put(a_scr, a)
    put(b_scr, b)

    def seg_step(j, carry):
        hl, pl_ = carry
        aj = ld(a_scr, j)
        hl = aj * hl + ld(b_scr, j)
        pl_ = aj * pl_
        st(h_scr, j, hl)
        st(a_scr, j, pl_)
        return hl, pl_

    zeros = jnp.zeros((SUBLANES, D_RNN), F32)
    def seg_steps(jj, carry):
        for u in range(SCAN_UNROLL):
            carry = seg_step(jj * SCAN_UNROLL + u, carry)
        return carry

    h_end, p_end = lax.fori_loop(0, seg // SCAN_UNROLL, seg_steps, (zeros, zeros + 1.0))

    def seg_rows(scr, s):
        return jnp.concatenate(
            [scr[c, s * pitch:s * pitch + seg, :] for c in range(n_lg)], axis=1)

    c = hstate_ref[...]
    segs = []
    for s in range(SUBLANES):
        segs.append(seg_rows(h_scr, s) + seg_rows(a_scr, s) * c)
        c = h_end[s:s + 1, :] + p_end[s:s + 1, :] * c
    hstate_ref[...] = c
    hseq = jnp.concatenate(segs, axis=0)
    out = x + _dot((gate * hseq).astype(BF16), wout_ref[...])
    _store_tt(o_ref, out)
    _route(out, first, gf_ref, wr_ref, br_ref, ids_ref, cnt_ref, rcarry_ref,
           before_ref, tm)


def _mixer_rglru(x_tt, moe_out, bsz, t_len, g, w_in, conv_w, conv_b, w_a, b_a,
                 w_x, b_x, lam, w_out, route_args, tm):
    y, dest_tiles = moe_out
    nt = t_len // tm
    n_tiles = bsz * nt
    seg = tm // SUBLANES
    pitch = seg + SUBLANES
    d = D_MODEL
    full = lambda shape: pl.BlockSpec(shape, lambda i: (0,) * len(shape))
    hbm = pl.BlockSpec(memory_space=pl.ANY)
    row = lambda v: v.reshape(1, -1)
    gc_specs, gc_scratch = _gather_combine_specs(n_tiles, tm)
    r_in, r_out, r_shapes, r_scratch = _route_specs(lambda i: i, tm, n_tiles * tm)
    return pl.pallas_call(
        functools.partial(_mixer_rglru_kernel, tm=tm, nt=nt, seg=seg, pitch=pitch),
        grid=(n_tiles,),
        in_specs=gc_specs + [
            full((1, d)),
            hbm,
            full(conv_w.shape),
            full((1, D_RNN)),
            hbm,
            full((1, D_RNN)),
            hbm,
            full((1, D_RNN)),
            full((1, D_RNN)),
            hbm,
        ] + r_in,
        out_specs=[pl.BlockSpec((tm * TT, LANES), lambda i: (i, 0))] + r_out,
        out_shape=[jax.ShapeDtypeStruct(x_tt.shape, F32)] + r_shapes,
        scratch_shapes=gc_scratch + [
            pltpu.VMEM((SUBLANES, D_RNN), F32),
            pltpu.VMEM((1, D_RNN), F32),
            pltpu.VMEM((D_RNN // LANES, SUBLANES * pitch, LANES), F32),
            pltpu.VMEM((D_RNN // LANES, SUBLANES * pitch, LANES), F32),
            pltpu.VMEM((D_RNN // LANES, SUBLANES * pitch, LANES), F32),
            pltpu.VMEM((d, 2 * D_RNN), BF16),
            pltpu.VMEM((D_RNN, LRU_HEAD_DIM), BF16),
            pltpu.VMEM((D_RNN, LRU_HEAD_DIM), BF16),
            pltpu.VMEM((D_RNN, d), BF16),
            pltpu.VMEM((d, STAGE_COLS), F32),
            pltpu.SemaphoreType.DMA(()),
        ] + r_scratch,
        compiler_params=pltpu.CompilerParams(
            dimension_semantics=("arbitrary",),
            vmem_limit_bytes=VMEM_LIMIT),
        name="mixer_rglru",
    )(dest_tiles, dest_tiles, x_tt, y, row(g), w_in, conv_w, row(conv_b),
      w_a.reshape(D_RNN, LRU_HEAD_DIM), row(b_a), w_x.reshape(D_RNN, LRU_HEAD_DIM),
      row(b_x), row(lam), w_out, *route_args)


def _route(x, first, gf_ref, wr_ref, br_ref, ids_ref, cnt_ref, carry_ref,
           before_ref, tm):
    h = _rms(x, gf_ref[...]).astype(BF16)
    lt = lax.dot_general(wr_ref[...], h, (((1,), (1,)), ((), ())),
                         preferred_element_type=F32) + br_ref[...]
    gl = lt[GROUP_LANE:GROUP_LANE + N_GROUPS, :]
    gmax = jnp.max(gl, axis=0, keepdims=True)
    g_iota = lax.broadcasted_iota(jnp.int32, gl.shape, 0)
    g_idx = jnp.min(jnp.where(gl == gmax, g_iota, N_GROUPS), axis=0, keepdims=True)

    esel = lt[EXPERT_LANE:EXPERT_LANE + EXPERTS_PER_GROUP, :]
    for gidx in range(1, N_GROUPS):
        lo = EXPERT_LANE + gidx * EXPERTS_PER_GROUP
        esel = jnp.where(g_idx == gidx, lt[lo:lo + EXPERTS_PER_GROUP, :], esel)
    e_iota = lax.broadcasted_iota(jnp.int32, esel.shape, 0)
    top1 = jnp.max(esel, axis=0, keepdims=True)
    i1 = jnp.min(jnp.where(esel == top1, e_iota, EXPERTS_PER_GROUP), axis=0, keepdims=True)
    rest = jnp.where(e_iota == i1, -jnp.inf, esel)
    top2 = jnp.max(rest, axis=0, keepdims=True)
    i2 = jnp.min(jnp.where(rest == top2, e_iota, EXPERTS_PER_GROUP), axis=0, keepdims=True)

    lo_e = jnp.minimum(i1, i2)
    hi_e = jnp.maximum(i1, i2)
    pair = jnp.right_shift(lo_e * (2 * EXPERTS_PER_GROUP - 1 - lo_e), 1) + (hi_e - lo_e - 1)
    cls = g_idx * N_PAIRS + pair

    c_iota = lax.broadcasted_iota(jnp.int32, (CLS_PAD, tm), 0)
    hit = c_iota == cls
    onehot = jnp.where(hit, 1.0, 0.0)

    @pl.when(first)
    def _():
        carry_ref[...] = jnp.zeros_like(carry_ref)
        s_i = lax.broadcasted_iota(jnp.int32, (tm, tm), 0)
        t_i = lax.broadcasted_iota(jnp.int32, (tm, tm), 1)
        before_ref[...] = jnp.where(s_i < t_i, 1.0, 0.0).astype(BF16)

    prefix = _dot(onehot.astype(BF16), before_ref[...]) + carry_ref[:, 0:1]
    rank = jnp.sum(jnp.where(hit, prefix, 0.0), axis=0, keepdims=True)
    carry_ref[...] = carry_ref[...] + jnp.sum(onehot, axis=1, keepdims=True)

    ids_ref[0:1, :] = cls
    ids_ref[1:2, :] = rank.astype(jnp.int32)
    ids_ref[2:8, :] = jnp.zeros((6, tm), jnp.int32)
    cnt_ref[...] = carry_ref[...].astype(jnp.int32)


def _router_weights(w_rg, b_rg, w_re, b_re):
    wr = jnp.zeros((ROUTER_ROWS, D_MODEL), F32)
    wr = wr.at[GROUP_LANE:GROUP_LANE + N_GROUPS].set(w_rg.T)
    wr = wr.at[EXPERT_LANE:EXPERT_LANE + N_EXPERTS].set(w_re.T)
    br = jnp.zeros((ROUTER_ROWS,), F32)
    br = br.at[GROUP_LANE:GROUP_LANE + N_GROUPS].set(b_rg)
    br = br.at[EXPERT_LANE:EXPERT_LANE + N_EXPERTS].set(b_re)
    return wr, br


def _route_specs(tile_idx, tm, n_tok):
    const = lambda shape: pl.BlockSpec(shape, lambda *_: (0,) * len(shape))
    in_specs = [const((1, D_MODEL)), const((ROUTER_ROWS, D_MODEL)),
                const((ROUTER_ROWS, 1))]
    out_specs = [pl.BlockSpec((8, tm), lambda *g: (0, tile_idx(*g))),
                 const((CLS_PAD, LANES))]
    out_shapes = [jax.ShapeDtypeStruct((8, n_tok), jnp.int32),
                  jax.ShapeDtypeStruct((CLS_PAD, LANES), jnp.int32)]
    scratch = [pltpu.VMEM((CLS_PAD, LANES), F32), pltpu.VMEM((tm, tm), BF16)]
    return in_specs, out_specs, out_shapes, scratch


def _route_args(g, wr, br):
    return g.reshape(1, D_MODEL), wr.astype(BF16), br.reshape(ROUTER_ROWS, 1)


def _dest_kernel(ids_ref, pstart_ref, dest_ref):
    ids = ids_ref[...]
    c_iota = lax.broadcasted_iota(jnp.int32, (CLS_PAD, ids.shape[1]), 0)
    pstart = pstart_ref[:, 0:1]
    hit = c_iota == ids[0:1, :]
    base = jnp.sum(jnp.where(hit, pstart, 0), axis=0, keepdims=True)
    dest_ref[0:1, :] = base + ids[1:2, :]
    dest_ref[1:8, :] = jnp.zeros((7, ids.shape[1]), jnp.int32)


def _dest(ids, pad_starts, t):
    n_tok = ids.shape[1]
    pstart = jnp.broadcast_to(pad_starts[:, None], (CLS_PAD, LANES))
    return pl.pallas_call(
        _dest_kernel,
        grid=(n_tok // t,),
        in_specs=[
            pl.BlockSpec((8, t), lambda i: (0, i)),
            pl.BlockSpec((CLS_PAD, LANES), lambda i: (0, 0)),
        ],
        out_specs=pl.BlockSpec((8, t), lambda i: (0, i)),
        out_shape=jax.ShapeDtypeStruct((8, n_tok), jnp.int32),
        name="moe_dest",
    )(ids, pstart)


def _dispatch_kernel(pstart_ref, pend_ref, dest_ref, x_ref, zero_ref,
                     wg_ref, wu_ref, wd_ref, xs_hbm, wg_out, wu_out, wd_out,
                     sem, zsem, *, tm, blk, n_blocks):
    i = pl.program_id(0)

    @pl.when(i == 0)
    def _():
        def zcopy(c):
            start = pl.multiple_of((pend_ref[c] - blk) * TT, blk * TT)
            return pltpu.make_async_copy(
                zero_ref, xs_hbm.at[pl.ds(start, blk * TT), :], zsem)

        def zstart(c, _):
            @pl.when(pend_ref[c] > pstart_ref[c])
            def _():
                zcopy(c).start()
            return 0

        def zwait(c, _):
            @pl.when(pend_ref[c] > pstart_ref[c])
            def _():
                zcopy(c).wait()
            return 0

        def tcopy(b):
            return pltpu.make_async_copy(
                zero_ref,
                xs_hbm.at[pl.ds(pl.multiple_of(b * (blk * TT), blk * TT), blk * TT), :],
                zsem)

        def tstart(b, _):
            tcopy(b).start()
            return 0

        def twait(b, _):
            tcopy(b).wait()
            return 0

        n_used = pend_ref[N_CLASSES - 1] // blk
        lax.fori_loop(0, N_CLASSES, zstart, 0)
        lax.fori_loop(n_used, n_blocks, tstart, 0)
        lax.fori_loop(0, N_CLASSES, zwait, 0)
        lax.fori_loop(n_used, n_blocks, twait, 0)

    def issue(jj, _):
        for u in range(DMA_UNROLL):
            j = jj * DMA_UNROLL + u
            d = dest_ref[0, 0, j]
            pltpu.make_async_copy(
                x_ref.at[pl.ds(pl.multiple_of(j * TT, TT), TT), :],
                xs_hbm.at[pl.ds(pl.multiple_of(d * TT, TT), TT), :], sem).start()
        return 0

    lax.fori_loop(0, tm // DMA_UNROLL, issue, 0)
    wg_out[0] = wg_ref[0, 0].astype(BF16)
    wu_out[0] = wu_ref[0, 0].astype(BF16)
    wd_out[0] = wd_ref[0, 0].astype(BF16)
    pltpu.make_async_copy(x_ref, xs_hbm.at[pl.ds(0, tm * TT), :], sem).wait()


def _dispatch(x_tt, dest_tiles, pad_starts, pad_ends, w_gate, w_up, w_down, layer,
              n_rows, tm, blk):
    n_tok = x_tt.shape[0] // TT
    n_tiles = n_tok // tm
    assert n_tiles == N_EXPERTS
    zeros = jnp.zeros((blk * TT, LANES), F32)
    w_in = lambda i, ps, pe: (layer, i, 0, 0)
    w_o = lambda i, ps, pe: (i, 0, 0)
    return pl.pallas_call(
        functools.partial(_dispatch_kernel, tm=tm, blk=blk, n_blocks=n_rows // blk),
        grid_spec=pltpu.PrefetchScalarGridSpec(
            num_scalar_prefetch=2,
            grid=(n_tiles,),
            in_specs=[
                pl.BlockSpec((1, 1, tm), lambda i, ps, pe: (i, 0, 0),
                             memory_space=pltpu.SMEM),
                pl.BlockSpec((tm * TT, LANES), lambda i, ps, pe: (i, 0)),
                pl.BlockSpec((blk * TT, LANES), lambda i, ps, pe: (0, 0)),
                pl.BlockSpec((1, 1, D_MODEL, D_EXPERT), w_in),
                pl.BlockSpec((1, 1, D_MODEL, D_EXPERT), w_in),
                pl.BlockSpec((1, 1, D_EXPERT, D_MODEL), w_in),
            ],
            out_specs=[
                pl.BlockSpec(memory_space=pl.ANY),
                pl.BlockSpec((1, D_MODEL, D_EXPERT), w_o),
                pl.BlockSpec((1, D_MODEL, D_EXPERT), w_o),
                pl.BlockSpec((1, D_EXPERT, D_MODEL), w_o),
            ],
            scratch_shapes=[pltpu.SemaphoreType.DMA(()), pltpu.SemaphoreType.DMA(())],
        ),
        out_shape=[
            jax.ShapeDtypeStruct((n_rows * TT, LANES), F32),
            jax.ShapeDtypeStruct((N_EXPERTS, D_MODEL, D_EXPERT), BF16),
            jax.ShapeDtypeStruct((N_EXPERTS, D_MODEL, D_EXPERT), BF16),
            jax.ShapeDtypeStruct((N_EXPERTS, D_EXPERT, D_MODEL), BF16),
        ],
        compiler_params=pltpu.CompilerParams(
            dimension_semantics=("arbitrary",), vmem_limit_bytes=VMEM_LIMIT),
        name="moe_dispatch",
    )(pad_starts, pad_ends, dest_tiles, x_tt, zeros, w_gate, w_up, w_down)


def _expert_kernel(ea_ref, eb_ref, ns_ref, xs_ref, g_ref, wr_ref, br_ref, *refs, blk):
    i = pl.program_id(0)
    w_refs, y_ref = refs[:-1], refs[-1]

    def unit(k):
        ea = ea_ref[i * EXPERT_UNITS + k]
        eb = eb_ref[i * EXPERT_UNITS + k]
        wga_ref, wua_ref, wda_ref, wgb_ref, wub_ref, wdb_ref = w_refs[6 * k:6 * k + 6]
        x = _load_tt(xs_ref, blk, k * blk)
        h = _rms(x, g_ref[...]).astype(BF16)

        logits = _dot(h, wr_ref[...]) + br_ref[...]
        lane = lax.broadcasted_iota(jnp.int32, logits.shape, 1)
        is_group = (lane >= GROUP_LANE) & (lane < GROUP_LANE + N_GROUPS)
        pick = lambda l: jnp.sum(jnp.where(lane == l, logits, 0.0), axis=-1, keepdims=True)
        gmax = jnp.max(jnp.where(is_group, logits, -jnp.inf), axis=-1, keepdims=True)
        gsum = jnp.sum(jnp.where(is_group, jnp.exp(logits - gmax), 0.0),
                       axis=-1, keepdims=True)
        g_prob = jnp.exp(pick(GROUP_LANE + ea // EXPERTS_PER_GROUP) - gmax) / gsum
        la = pick(EXPERT_LANE + ea)
        lb = pick(EXPERT_LANE + eb)
        m = jnp.maximum(la, lb)
        pa = jnp.exp(la - m)
        pb = jnp.exp(lb - m)
        scale = g_prob / (pa + pb)
        gate_a = pa * scale
        gate_b = pb * scale

        hid_a = (jax.nn.silu(_dot(h, wga_ref[0])) * _dot(h, wua_ref[0])).astype(BF16)
        hid_b = (jax.nn.silu(_dot(h, wgb_ref[0])) * _dot(h, wub_ref[0])).astype(BF16)
        _store_tt(y_ref, gate_a * _dot(hid_a, wda_ref[0])
                  + gate_b * _dot(hid_b, wdb_ref[0]), k * blk)

    @pl.when(i < ns_ref[0])
    def _():
        for k in range(EXPERT_UNITS):
            unit(k)

    @pl.when(i >= ns_ref[0])
    def _():
        y_ref[...] = jnp.zeros_like(y_ref)


def _experts(xs, g, wr, br, w_gate, w_up, w_down, block_ea, block_eb, n_used, blk):
    n_blocks = xs.shape[0] // (blk * TT)
    assert n_blocks % EXPERT_UNITS == 0
    n_steps = (n_used + EXPERT_UNITS - 1) // EXPERT_UNITS
    last = lambda i, ea, eb, ns: (jnp.minimum(i, ns[0] - 1), 0)
    const = lambda i, ea, eb, ns: (0, 0)
    w_specs = []
    for k in range(EXPERT_UNITS):
        w_a = lambda i, ea, eb, ns, k=k: (ea[i * EXPERT_UNITS + k], 0, 0)
        w_b = lambda i, ea, eb, ns, k=k: (eb[i * EXPERT_UNITS + k], 0, 0)
        for idx in (w_a, w_b):
            w_specs += [pl.BlockSpec((1, D_MODEL, D_EXPERT), idx),
                        pl.BlockSpec((1, D_MODEL, D_EXPERT), idx),
                        pl.BlockSpec((1, D_EXPERT, D_MODEL), idx)]
    wr2 = jnp.zeros((D_MODEL, LANES), F32).at[:, 0:ROUTER_ROWS].set(wr.T)
    br2 = jnp.zeros((1, LANES), F32).at[0, 0:ROUTER_ROWS].set(br)
    rows = EXPERT_UNITS * blk * TT
    return pl.pallas_call(
        functools.partial(_expert_kernel, blk=blk),
        grid_spec=pltpu.PrefetchScalarGridSpec(
            num_scalar_prefetch=3,
            grid=(n_blocks // EXPERT_UNITS,),
            in_specs=[
                pl.BlockSpec((rows, LANES), last),
                pl.BlockSpec((1, D_MODEL), const),
                pl.BlockSpec((D_MODEL, LANES), const),
                pl.BlockSpec((1, LANES), const),
            ] + w_specs,
            out_specs=pl.BlockSpec((rows, LANES), lambda i, ea, eb, ns: (i, 0)),
        ),
        out_shape=jax.ShapeDtypeStruct(xs.shape, F32),
        compiler_params=pltpu.CompilerParams(
            dimension_semantics=("arbitrary",),
            vmem_limit_bytes=VMEM_LIMIT),
        name="moe_experts",
    )(block_ea, block_eb, n_steps, xs, g.reshape(1, D_MODEL), wr2.astype(BF16), br2,
      *([w_gate, w_up, w_down] * (2 * EXPERT_UNITS)))


def _final_kernel(dcur_ref, dnext_ref, x_ref, y_hbm, gf_ref, o_ref, yg_ref, gsems,
                  *, tm):
    out = _gather_combine(dcur_ref, dnext_ref, x_ref, y_hbm, yg_ref, gsems, tm)
    o_ref[...] = _rms(out, gf_ref[...])


def _final(x_tt, moe_out, g_final, tm):
    y, dest_tiles = moe_out
    n_tok = x_tt.shape[0] // TT
    n_tiles = n_tok // tm
    gc_specs, gc_scratch = _gather_combine_specs(n_tiles, tm)
    return pl.pallas_call(
        functools.partial(_final_kernel, tm=tm),
        grid=(n_tiles,),
        in_specs=gc_specs + [pl.BlockSpec((1, D_MODEL), lambda i: (0, 0))],
        out_specs=pl.BlockSpec((tm, D_MODEL), lambda i: (i, 0)),
        out_shape=jax.ShapeDtypeStruct((n_tok, D_MODEL), F32),
        scratch_shapes=gc_scratch,
        compiler_params=pltpu.CompilerParams(
            dimension_semantics=("arbitrary",),
            vmem_limit_bytes=VMEM_LIMIT),
        name="moe_combine_final",
    )(dest_tiles, dest_tiles, x_tt, y, g_final.reshape(1, D_MODEL))


def _moe(x_tt, ids, cnt, g, wr, br, w_gate, w_up, w_down, layer):
    n_tok = x_tt.shape[0] // TT
    tm, td, blk = _tiles(n_tok)
    n_blocks = n_tok // blk + N_CLASSES

    counts = cnt[:, 0]
    padded = (counts + blk - 1) // blk * blk
    pad_ends = jnp.cumsum(padded).astype(jnp.int32)
    pad_starts = pad_ends - padded
    block_start = jnp.arange(n_blocks, dtype=jnp.int32) * blk
    block_cls = jnp.minimum(
        jnp.sum(pad_ends[None, :N_CLASSES] <= block_start[:, None], axis=1),
        N_CLASSES - 1)
    block_ea = jnp.asarray(CLASS_LO)[block_cls]
    block_eb = jnp.asarray(CLASS_HI)[block_cls]
    n_used = (pad_ends[N_CLASSES - 1:N_CLASSES] // blk).astype(jnp.int32)

    dest = _dest(ids, pad_starts, DEST_TILE if n_tok % DEST_TILE == 0 else tm)
    per_tile = lambda t: dest[0].reshape(n_tok // t, 1, t)

    xs, wg_bf, wu_bf, wd_bf = _dispatch(x_tt, per_tile(td), pad_starts, pad_ends,
                                        w_gate, w_up, w_down, layer,
                                        n_blocks * blk, td, blk)
    y = _experts(xs, g, wr, br, wg_bf, wu_bf, wd_bf, block_ea, block_eb, n_used, blk)
    return y, per_tile(tm)


def kernel(x, norm_mix_g, norm_ffn_g, norm_final_g, ab_w_in, a_ln_g, a_ws, a_ws_b, b_conv_w, ab_w_out, c_w_in, c_conv_w, c_conv_b, c_w_a, c_b_a, c_w_x, c_b_x, c_lambda, c_w_out, moe_w_rg, moe_b_rg, moe_w_re, moe_b_re, moe_w_gate, moe_w_up, moe_w_down):
    bsz, t_len, d = x.shape
    assert d == D_MODEL
    tm, _, _ = _tiles(bsz * t_len)
    assert t_len % tm == 0 and tm % CHUNK == 0

    routers = [_router_weights(moe_w_rg[l], moe_b_rg[l], moe_w_re[l], moe_b_re[l])
               for l in range(2)]
    route_args = [_route_args(norm_ffn_g[l], *routers[l]) for l in range(2)]

    def moe(x_tt, ids, cnt, layer):
        return _moe(x_tt, ids, cnt, norm_ffn_g[layer], *routers[layer],
                    moe_w_gate, moe_w_up, moe_w_down, layer)

    x1, ids, cnt = _mixer_ab(x, norm_mix_g[0], ab_w_in[0], a_ln_g[0], a_ws[0],
                             a_ws_b[0], b_conv_w[0], ab_w_out[0], route_args[0],
                             AB_TILE if t_len % AB_TILE == 0 else tm)
    x3, ids, cnt = _mixer_rglru(x1, moe(x1, ids, cnt, 0), bsz, t_len, norm_mix_g[1],
                                c_w_in[0], c_conv_w[0], c_conv_b[0], c_w_a[0],
                                c_b_a[0], c_w_x[0], c_b_x[0], c_lambda[0],
                                c_w_out[0], route_args[1], tm)
    out = _final(x3, moe(x3, ids, cnt, 1), norm_final_g, tm)
    return out.reshape(bsz, t_len, d)
```

```python
import functools

import numpy as np
import jax
import jax.numpy as jnp
from jax import lax
from jax.experimental import pallas as pl
from jax.experimental.pallas import tpu as pltpu

D_MODEL = 1024
LANES = 128
SUBLANES = 8
TT = D_MODEL // LANES
assert TT == SUBLANES

A_HEADS = 4
A_HEAD_DIM = 128
D_A = A_HEADS * A_HEAD_DIM
CHUNK = 128
D_B = D_MODEL - D_A
B_CONV = 3
D_RNN = D_MODEL
LRU_HEADS = 8
LRU_HEAD_DIM = D_RNN // LRU_HEADS
C_CONV = 4
LRU_C = 8.0
N_GROUPS = 4
EXPERTS_PER_GROUP = 8
N_EXPERTS = N_GROUPS * EXPERTS_PER_GROUP
D_EXPERT = 512
EPS = 1e-6

N_PAIRS = EXPERTS_PER_GROUP * (EXPERTS_PER_GROUP - 1) // 2
N_CLASSES = N_GROUPS * N_PAIRS
CLS_PAD = 128
assert N_CLASSES <= CLS_PAD
GROUP_LANE = 0
EXPERT_LANE = 8
ROUTER_ROWS = 48
VMEM_LIMIT = 56 * 1024 * 1024

BF16 = jnp.bfloat16
F32 = jnp.float32

_PAIRS = [(lo, hi) for lo in range(EXPERTS_PER_GROUP)
          for hi in range(lo + 1, EXPERTS_PER_GROUP)]
CLASS_LO = np.array([g * EXPERTS_PER_GROUP + lo
                     for g in range(N_GROUPS) for lo, _ in _PAIRS], np.int32)
CLASS_HI = np.array([g * EXPERTS_PER_GROUP + hi
                     for g in range(N_GROUPS) for _, hi in _PAIRS], np.int32)


def _tiles(n_tok):
    tm = 512 if n_tok % 512 == 0 else 256
    td = n_tok // N_EXPERTS
    assert td * N_EXPERTS == n_tok and td % DMA_UNROLL == 0
    blk = 256 if n_tok >= 8192 else 128
    return tm, td, blk


DMA_UNROLL = 64
DMA_PRIORITIES = 2
STAGE_COLS = 512
DEST_TILE = 4096
AB_TILE = 1024
SCAN_UNROLL = 8
EXPERT_UNITS = 2


def _load_tt(ref, nrows, row0=0):
    return jnp.concatenate(
        [ref[pl.ds(row0 * TT + s, nrows, stride=TT), :] for s in range(TT)], axis=1)


def _store_tt(ref, val, row0=0):
    nrows = val.shape[0]
    for s in range(TT):
        ref[pl.ds(row0 * TT + s, nrows, stride=TT), :] = val[:, s * LANES:(s + 1) * LANES]


def _rms(x, g):
    ms = jnp.mean(x * x, axis=-1, keepdims=True)
    return x * lax.rsqrt(ms + EPS) * g


def _dot(a, b):
    return jnp.dot(a, b, preferred_element_type=F32)


def _shift_rows(cur, prev, k):
    rolled = pltpu.roll(cur, k, axis=0)
    row = lax.broadcasted_iota(jnp.int32, prev.shape, 0)
    head = jnp.where(row < k, pltpu.roll(prev, k, axis=0), rolled[0:SUBLANES, :])
    return jnp.concatenate([head, rolled[SUBLANES:, :]], axis=0)


def _stage_bf16(w_hbm, w_bf, stage, sem):
    cols = w_hbm.shape[1]
    cw = min(cols, stage.shape[1])
    for c0 in range(0, cols, cw):
        cp = pltpu.make_async_copy(w_hbm.at[:, pl.ds(c0, cw)],
                                   stage.at[:, pl.ds(0, cw)], sem)
        cp.start()
        cp.wait()
        w_bf[:, c0:c0 + cw] = stage[:, 0:cw].astype(BF16)


def _gather_combine(dcur_ref, dnext_ref, x_ref, y_hbm, yg_ref, sems, tm):
    i = pl.program_id(0)
    slot = i % 2

    def issue(dref, sl):
        def body(jj, _):
            for u in range(DMA_UNROLL):
                j = jj * DMA_UNROLL + u
                d = dref[0, 0, j]
                pltpu.make_async_copy(
                    y_hbm.at[pl.ds(pl.multiple_of(d * TT, TT), TT), :],
                    yg_ref.at[sl, pl.ds(pl.multiple_of(j * TT, TT), TT), :],
                    sems.at[sl]).start(priority=u % DMA_PRIORITIES)
            return 0

        lax.fori_loop(0, tm // DMA_UNROLL, body, 0)

    @pl.when(i == 0)
    def _():
        issue(dcur_ref, 0)

    @pl.when(i + 1 < pl.num_programs(0))
    def _():
        issue(dnext_ref, 1 - slot)

    pltpu.make_async_copy(y_hbm.at[pl.ds(0, tm * TT), :], yg_ref.at[slot],
                          sems.at[slot]).wait()
    return _load_tt(x_ref, tm) + _load_tt(yg_ref.at[slot], tm)


def _gather_combine_specs(n_tiles, tm):
    in_specs = [
        pl.BlockSpec((1, 1, tm), lambda i: (i, 0, 0), memory_space=pltpu.SMEM),
        pl.BlockSpec((1, 1, tm), lambda i: (jnp.minimum(i + 1, n_tiles - 1), 0, 0),
                     memory_space=pltpu.SMEM),
        pl.BlockSpec((tm * TT, LANES), lambda i: (i, 0)),
        pl.BlockSpec(memory_space=pl.ANY),
    ]
    scratch = [
        pltpu.VMEM((2, tm * TT, LANES), F32),
        pltpu.SemaphoreType.DMA((2,)),
    ]
    return in_specs, scratch


def _mixer_ab_kernel(x_ref, g_ref, win_hbm, lng_ref, ws_ref, wsb_ref, cw_ref,
                     wout_hbm, gf_ref, wr_ref, br_ref, o_ref, ids_ref, cnt_ref,
                     carry_ref, win_ref, wout_ref, stage_ref, wsem,
                     rcarry_ref, before_ref, *, tm):
    t = pl.program_id(1)
    first = (pl.program_id(0) == 0) & (t == 0)

    @pl.when(first)
    def _():
        _stage_bf16(win_hbm, win_ref, stage_ref, wsem)
        _stage_bf16(wout_hbm, wout_ref, stage_ref, wsem)

    x = x_ref[0]
    h = _rms(x, g_ref[...]).astype(BF16)

    u = jax.nn.gelu(_dot(h, win_ref[:, 0:D_A]))
    v = jax.nn.gelu(_dot(h, win_ref[:, D_A:2 * D_A]))
    mu = jnp.mean(v, axis=-1, keepdims=True)
    vc = v - mu
    var = jnp.mean(vc * vc, axis=-1, keepdims=True)
    vn = (vc * lax.rsqrt(var + EPS) * lng_ref[...]).astype(BF16)

    r_i = lax.broadcasted_iota(jnp.int32, (CHUNK, CHUNK), 0)
    c_i = lax.broadcasted_iota(jnp.int32, (CHUNK, CHUNK), 1)
    causal = r_i >= c_i
    head_cols = []
    for hh in range(A_HEADS):
        wsh = jnp.where(causal, ws_ref[hh], 0.0).astype(BF16)
        rows = []
        for c in range(tm // CHUNK):
            blk = vn[c * CHUNK:(c + 1) * CHUNK,
                     hh * A_HEAD_DIM:(hh + 1) * A_HEAD_DIM]
            rows.append(_dot(wsh, blk))
        head_cols.append(jnp.concatenate(rows, axis=0))
    mixed = jnp.concatenate(head_cols, axis=1) + wsb_ref[...]
    y_a = (u * mixed).astype(BF16)

    gate_b = _dot(h, win_ref[:, 2 * D_A:2 * D_A + D_B])
    gate_c = _dot(h, win_ref[:, 2 * D_A + D_B:2 * D_A + 2 * D_B])
    xb = _dot(h, win_ref[:, 2 * D_A + 2 * D_B:2 * D_A + 3 * D_B])
    cx = gate_c * xb

    @pl.when(t == 0)
    def _():
        carry_ref[...] = jnp.zeros_like(carry_ref)

    prev = carry_ref[...]
    conv = (_shift_rows(cx, prev, 2) * cw_ref[0:1, :]
            + _shift_rows(cx, prev, 1) * cw_ref[1:2, :]
            + cx * cw_ref[2:3, :])
    carry_ref[...] = cx[tm - SUBLANES:tm, :]
    y_b = (gate_b * conv).astype(BF16)

    out = x + _dot(y_a, wout_ref[0:D_A, :]) + _dot(y_b, wout_ref[D_A:D_MODEL, :])
    _store_tt(o_ref, out)
    _route(out, first, gf_ref, wr_ref, br_ref, ids_ref, cnt_ref, rcarry_ref,
           before_ref, tm)


def _mixer_ab(x, g, w_in, ln_g, ws, ws_b, conv_w, w_out, route_args, tm):
    bsz, t_len, d = x.shape
    nt = t_len // tm
    wsb_full = jnp.tile(jnp.repeat(ws_b.T, A_HEAD_DIM, axis=1), (tm // CHUNK, 1))
    full = lambda shape: pl.BlockSpec(shape, lambda b, t: (0,) * len(shape))
    hbm = pl.BlockSpec(memory_space=pl.ANY)
    r_in, r_out, r_shapes, r_scratch = _route_specs(lambda b, t: b * nt + t, tm,
                                                    bsz * t_len)
    return pl.pallas_call(
        functools.partial(_mixer_ab_kernel, tm=tm),
        grid=(bsz, nt),
        in_specs=[
            pl.BlockSpec((1, tm, d), lambda b, t: (b, t, 0)),
            full((1, d)),
            hbm,
            full((1, D_A)),
            full(ws.shape),
            full((tm, D_A)),
            full(conv_w.shape),
            hbm,
        ] + r_in,
        out_specs=[pl.BlockSpec((tm * TT, LANES), lambda b, t: (b * nt + t, 0))] + r_out,
        out_shape=[jax.ShapeDtypeStruct((bsz * t_len * TT, LANES), F32)] + r_shapes,
        scratch_shapes=[
            pltpu.VMEM((SUBLANES, D_B), F32),
            pltpu.VMEM(w_in.shape, BF16),
            pltpu.VMEM(w_out.shape, BF16),
            pltpu.VMEM((d, STAGE_COLS), F32),
            pltpu.SemaphoreType.DMA(()),
        ] + r_scratch,
        compiler_params=pltpu.CompilerParams(
            dimension_semantics=("arbitrary", "arbitrary"),
            vmem_limit_bytes=VMEM_LIMIT),
        name="mixer_ab",
    )(x, g.reshape(1, d), w_in, ln_g.reshape(1, D_A), ws, wsb_full, conv_w, w_out,
      *route_args)


def _mixer_rglru_kernel(dcur_ref, dnext_ref, x_ref, y_hbm,
                        g_ref, win_hbm, cw_ref, cb_ref, wa_hbm, ba_ref,
                        wx_hbm, bx_ref, lam_ref, wout_hbm, gf_ref, wr_ref, br_ref,
                        o_ref, ids_ref, cnt_ref,
                        yg_ref, gsems,
                        carry_ref, hstate_ref, a_scr, b_scr, h_scr,
                        win_ref, wa_ref, wx_ref, wout_ref, stage_ref, wsem,
                        rcarry_ref, before_ref,
                        *, tm, nt, seg, pitch):
    t = pl.program_id(0) % nt
    first = pl.program_id(0) == 0

    @pl.when(first)
    def _():
        _stage_bf16(win_hbm, win_ref, stage_ref, wsem)
        _stage_bf16(wa_hbm, wa_ref, stage_ref, wsem)
        _stage_bf16(wx_hbm, wx_ref, stage_ref, wsem)
        _stage_bf16(wout_hbm, wout_ref, stage_ref, wsem)

    x = _gather_combine(dcur_ref, dnext_ref, x_ref, y_hbm, yg_ref, gsems, tm)
    h = _rms(x, g_ref[...]).astype(BF16)
    gate = jax.nn.gelu(_dot(h, win_ref[:, 0:D_RNN]))
    xr0 = _dot(h, win_ref[:, D_RNN:2 * D_RNN])

    @pl.when(t == 0)
    def _():
        carry_ref[...] = jnp.zeros_like(carry_ref)
        hstate_ref[...] = jnp.zeros_like(hstate_ref)

    prev = carry_ref[...]
    xr = (_shift_rows(xr0, prev, 3) * cw_ref[0:1, :]
          + _shift_rows(xr0, prev, 2) * cw_ref[1:2, :]
          + _shift_rows(xr0, prev, 1) * cw_ref[2:3, :]
          + xr0 * cw_ref[3:4, :]) + cb_ref[...]
    carry_ref[...] = xr0[tm - SUBLANES:tm, :]

    xrb = xr.astype(BF16)
    r_cols, i_cols = [], []
    for hh in range(LRU_HEADS):
        blk = xrb[:, hh * LRU_HEAD_DIM:(hh + 1) * LRU_HEAD_DIM]
        rows = slice(hh * LRU_HEAD_DIM, (hh + 1) * LRU_HEAD_DIM)
        r_cols.append(_dot(blk, wa_ref[rows, :]))
        i_cols.append(_dot(blk, wx_ref[rows, :]))
    sigmoid = lambda v: 0.5 * jnp.tanh(0.5 * v) + 0.5
    r = sigmoid(jnp.concatenate(r_cols, axis=1) + ba_ref[...])
    i = sigmoid(jnp.concatenate(i_cols, axis=1) + bx_ref[...])

    lam = lam_ref[...]
    log_sig = jnp.minimum(lam, 0.0) - jnp.log1p(jnp.exp(-jnp.abs(lam)))
    log_a = (LRU_C * r) * log_sig
    a = jnp.exp(log_a)
    z = -jnp.tanh(log_a) * (a * a + 1.0)
    b = jnp.where(z > 0.0, z * lax.rsqrt(z), 0.0) * (i * xr)

    n_lg = D_RNN // LANES

    def put(scr, val):
        for c in range(n_lg):
            for s in range(SUBLANES):
                scr[c, s * pitch:s * pitch + seg, :] = (
                    val[s * seg:(s + 1) * seg, c * LANES:(c + 1) * LANES])

    def ld(scr, j):
        return jnp.concatenate(
            [scr[c, pl.ds(j, SUBLANES, stride=pitch), :] for c in range(n_lg)],
            axis=1)

    def st(scr, j, val):
        for c in range(n_lg):
            scr[c, pl.ds(j, SUBLANES, stride=pitch), :] = (
                val[:, c * LANES:(c + 1) * LANES])

    put(a_scr, a)
    put(b_scr, b)

    def seg_step(j, carry):
        hl, pl_ = carry
        aj = ld(a_scr, j)
        hl = aj * hl + ld(b_scr, j)
        pl_ = aj * pl_
        st(h_scr, j, hl)
        st(a_scr, j, pl_)
        return hl, pl_

    zeros = jnp.zeros((SUBLANES, D_RNN), F32)
    def seg_steps(jj, carry):
        for u in range(SCAN_UNROLL):
            carry = seg_step(jj * SCAN_UNROLL + u, carry)
        return carry

    h_end, p_end = lax.fori_loop(0, seg // SCAN_UNROLL, seg_steps, (zeros, zeros + 1.0))

    def seg_rows(scr, s):
        return jnp.concatenate(
            [scr[c, s * pitch:s * pitch + seg, :] for c in range(n_lg)], axis=1)

    c = hstate_ref[...]
    segs = []
    for s in range(SUBLANES):
        segs.append(seg_rows(h_scr, s) + seg_rows(a_scr, s) * c)
        c = h_end[s:s + 1, :] + p_end[s:s + 1, :] * c
    hstate_ref[...] = c
    hseq = jnp.concatenate(segs, axis=0)
    out = x + _dot((gate * hseq).astype(BF16), wout_ref[...])
    _store_tt(o_ref, out)
    _route(out, first, gf_ref, wr_ref, br_ref, ids_ref, cnt_ref, rcarry_ref,
           before_ref, tm)


def _mixer_rglru(x_tt, moe_out, bsz, t_len, g, w_in, conv_w, conv_b, w_a, b_a,
                 w_x, b_x, lam, w_out, route_args, tm):
    y, dest_tiles = moe_out
    nt = t_len // tm
    n_tiles = bsz * nt
    seg = tm // SUBLANES
    pitch = seg + SUBLANES
    d = D_MODEL
    full = lambda shape: pl.BlockSpec(shape, lambda i: (0,) * len(shape))
    hbm = pl.BlockSpec(memory_space=pl.ANY)
    row = lambda v: v.reshape(1, -1)
    gc_specs, gc_scratch = _gather_combine_specs(n_tiles, tm)
    r_in, r_out, r_shapes, r_scratch = _route_specs(lambda i: i, tm, n_tiles * tm)
    return pl.pallas_call(
        functools.partial(_mixer_rglru_kernel, tm=tm, nt=nt, seg=seg, pitch=pitch),
        grid=(n_tiles,),
        in_specs=gc_specs + [
            full((1, d)),
            hbm,
            full(conv_w.shape),
            full((1, D_RNN)),
            hbm,
            full((1, D_RNN)),
            hbm,
            full((1, D_RNN)),
            full((1, D_RNN)),
            hbm,
        ] + r_in,
        out_specs=[pl.BlockSpec((tm * TT, LANES), lambda i: (i, 0))] + r_out,
        out_shape=[jax.ShapeDtypeStruct(x_tt.shape, F32)] + r_shapes,
        scratch_shapes=gc_scratch + [
            pltpu.VMEM((SUBLANES, D_RNN), F32),
            pltpu.VMEM((1, D_RNN), F32),
            pltpu.VMEM((D_RNN // LANES, SUBLANES * pitch, LANES), F32),
            pltpu.VMEM((D_RNN // LANES, SUBLANES * pitch, LANES), F32),
            pltpu.VMEM((D_RNN // LANES, SUBLANES * pitch, LANES), F32),
            pltpu.VMEM((d, 2 * D_RNN), BF16),
            pltpu.VMEM((D_RNN, LRU_HEAD_DIM), BF16),
            pltpu.VMEM((D_RNN, LRU_HEAD_DIM), BF16),
            pltpu.VMEM((D_RNN, d), BF16),
            pltpu.VMEM((d, STAGE_COLS), F32),
            pltpu.SemaphoreType.DMA(()),
        ] + r_scratch,
        compiler_params=pltpu.CompilerParams(
            dimension_semantics=("arbitrary",),
            vmem_limit_bytes=VMEM_LIMIT),
        name="mixer_rglru",
    )(dest_tiles, dest_tiles, x_tt, y, row(g), w_in, conv_w, row(conv_b),
      w_a.reshape(D_RNN, LRU_HEAD_DIM), row(b_a), w_x.reshape(D_RNN, LRU_HEAD_DIM),
      row(b_x), row(lam), w_out, *route_args)


def _route(x, first, gf_ref, wr_ref, br_ref, ids_ref, cnt_ref, carry_ref,
           before_ref, tm):
    h = _rms(x, gf_ref[...]).astype(BF16)
    lt = lax.dot_general(wr_ref[...], h, (((1,), (1,)), ((), ())),
                         preferred_element_type=F32) + br_ref[...]
    gl = lt[GROUP_LANE:GROUP_LANE + N_GROUPS, :]
    gmax = jnp.max(gl, axis=0, keepdims=True)
    g_iota = lax.broadcasted_iota(jnp.int32, gl.shape, 0)
    g_idx = jnp.min(jnp.where(gl == gmax, g_iota, N_GROUPS), axis=0, keepdims=True)

    esel = lt[EXPERT_LANE:EXPERT_LANE + EXPERTS_PER_GROUP, :]
    for gidx in range(1, N_GROUPS):
        lo = EXPERT_LANE + gidx * EXPERTS_PER_GROUP
        esel = jnp.where(g_idx == gidx, lt[lo:lo + EXPERTS_PER_GROUP, :], esel)
    e_iota = lax.broadcasted_iota(jnp.int32, esel.shape, 0)
    top1 = jnp.max(esel, axis=0, keepdims=True)
    i1 = jnp.min(jnp.where(esel == top1, e_iota, EXPERTS_PER_GROUP), axis=0, keepdims=True)
    rest = jnp.where(e_iota == i1, -jnp.inf, esel)
    top2 = jnp.max(rest, axis=0, keepdims=True)
    i2 = jnp.min(jnp.where(rest == top2, e_iota, EXPERTS_PER_GROUP), axis=0, keepdims=True)

    lo_e = jnp.minimum(i1, i2)
    hi_e = jnp.maximum(i1, i2)
    pair = jnp.right_shift(lo_e * (2 * EXPERTS_PER_GROUP - 1 - lo_e), 1) + (hi_e - lo_e - 1)
    cls = g_idx * N_PAIRS + pair

    c_iota = lax.broadcasted_iota(jnp.int32, (CLS_PAD, tm), 0)
    hit = c_iota == cls
    onehot = jnp.where(hit, 1.0, 0.0)

    @pl.when(first)
    def _():
        carry_ref[...] = jnp.zeros_like(carry_ref)
        s_i = lax.broadcasted_iota(jnp.int32, (tm, tm), 0)
        t_i = lax.broadcasted_iota(jnp.int32, (tm, tm), 1)
        before_ref[...] = jnp.where(s_i < t_i, 1.0, 0.0).astype(BF16)

    prefix = _dot(onehot.astype(BF16), before_ref[...]) + carry_ref[:, 0:1]
    rank = jnp.sum(jnp.where(hit, prefix, 0.0), axis=0, keepdims=True)
    carry_ref[...] = carry_ref[...] + jnp.sum(onehot, axis=1, keepdims=True)

    ids_ref[0:1, :] = cls
    ids_ref[1:2, :] = rank.astype(jnp.int32)
    ids_ref[2:8, :] = jnp.zeros((6, tm), jnp.int32)
    cnt_ref[...] = carry_ref[...].astype(jnp.int32)


def _router_weights(w_rg, b_rg, w_re, b_re):
    wr = jnp.zeros((ROUTER_ROWS, D_MODEL), F32)
    wr = wr.at[GROUP_LANE:GROUP_LANE + N_GROUPS].set(w_rg.T)
    wr = wr.at[EXPERT_LANE:EXPERT_LANE + N_EXPERTS].set(w_re.T)
    br = jnp.zeros((ROUTER_ROWS,), F32)
    br = br.at[GROUP_LANE:GROUP_LANE + N_GROUPS].set(b_rg)
    br = br.at[EXPERT_LANE:EXPERT_LANE + N_EXPERTS].set(b_re)
    return wr, br


def _route_specs(tile_idx, tm, n_tok):
    const = lambda shape: pl.BlockSpec(shape, lambda *_: (0,) * len(shape))
    in_specs = [const((1, D_MODEL)), const((ROUTER_ROWS, D_MODEL)),
                const((ROUTER_ROWS, 1))]
    out_specs = [pl.BlockSpec((8, tm), lambda *g: (0, tile_idx(*g))),
                 const((CLS_PAD, LANES))]
    out_shapes = [jax.ShapeDtypeStruct((8, n_tok), jnp.int32),
                  jax.ShapeDtypeStruct((CLS_PAD, LANES), jnp.int32)]
    scratch = [pltpu.VMEM((CLS_PAD, LANES), F32), pltpu.VMEM((tm, tm), BF16)]
    return in_specs, out_specs, out_shapes, scratch


def _route_args(g, wr, br):
    return g.reshape(1, D_MODEL), wr.astype(BF16), br.reshape(ROUTER_ROWS, 1)


def _dest_kernel(ids_ref, pstart_ref, dest_ref):
    ids = ids_ref[...]
    c_iota = lax.broadcasted_iota(jnp.int32, (CLS_PAD, ids.shape[1]), 0)
    pstart = pstart_ref[:, 0:1]
    hit = c_iota == ids[0:1, :]
    base = jnp.sum(jnp.where(hit, pstart, 0), axis=0, keepdims=True)
    dest_ref[0:1, :] = base + ids[1:2, :]
    dest_ref[1:8, :] = jnp.zeros((7, ids.shape[1]), jnp.int32)


def _dest(ids, pad_starts, t):
    n_tok = ids.shape[1]
    pstart = jnp.broadcast_to(pad_starts[:, None], (CLS_PAD, LANES))
    return pl.pallas_call(
        _dest_kernel,
        grid=(n_tok // t,),
        in_specs=[
            pl.BlockSpec((8, t), lambda i: (0, i)),
            pl.BlockSpec((CLS_PAD, LANES), lambda i: (0, 0)),
        ],
        out_specs=pl.BlockSpec((8, t), lambda i: (0, i)),
        out_shape=jax.ShapeDtypeStruct((8, n_tok), jnp.int32),
        name="moe_dest",
    )(ids, pstart)


def _dispatch_kernel(pstart_ref, pend_ref, dest_ref, x_ref, zero_ref,
                     wg_ref, wu_ref, wd_ref, xs_hbm, wg_out, wu_out, wd_out,
                     sem, zsem, *, tm, blk, n_blocks):
    i = pl.program_id(0)

    @pl.when(i == 0)
    def _():
        def zcopy(c):
            start = pl.multiple_of((pend_ref[c] - blk) * TT, blk * TT)
            return pltpu.make_async_copy(
                zero_ref, xs_hbm.at[pl.ds(start, blk * TT), :], zsem)

        def zstart(c, _):
            @pl.when(pend_ref[c] > pstart_ref[c])
            def _():
                zcopy(c).start()
            return 0

        def zwait(c, _):
            @pl.when(pend_ref[c] > pstart_ref[c])
            def _():
                zcopy(c).wait()
            return 0

        def tcopy(b):
            return pltpu.make_async_copy(
                zero_ref,
                xs_hbm.at[pl.ds(pl.multiple_of(b * (blk * TT), blk * TT), blk * TT), :],
                zsem)

        def tstart(b, _):
            tcopy(b).start()
            return 0

        def twait(b, _):
            tcopy(b).wait()
            return 0

        n_used = pend_ref[N_CLASSES - 1] // blk
        lax.fori_loop(0, N_CLASSES, zstart, 0)
        lax.fori_loop(n_used, n_blocks, tstart, 0)
        lax.fori_loop(0, N_CLASSES, zwait, 0)
        lax.fori_loop(n_used, n_blocks, twait, 0)

    def issue(jj, _):
        for u in range(DMA_UNROLL):
            j = jj * DMA_UNROLL + u
            d = dest_ref[0, 0, j]
            pltpu.make_async_copy(
                x_ref.at[pl.ds(pl.multiple_of(j * TT, TT), TT), :],
                xs_hbm.at[pl.ds(pl.multiple_of(d * TT, TT), TT), :],
                sem).start(priority=u % DMA_PRIORITIES)
        return 0

    lax.fori_loop(0, tm // DMA_UNROLL, issue, 0)
    wg_out[0] = wg_ref[0, 0].astype(BF16)
    wu_out[0] = wu_ref[0, 0].astype(BF16)
    wd_out[0] = wd_ref[0, 0].astype(BF16)
    pltpu.make_async_copy(x_ref, xs_hbm.at[pl.ds(0, tm * TT), :], sem).wait()


def _dispatch(x_tt, dest_tiles, pad_starts, pad_ends, w_gate, w_up, w_down, layer,
              n_rows, tm, blk):
    n_tok = x_tt.shape[0] // TT
    n_tiles = n_tok // tm
    assert n_tiles == N_EXPERTS
    zeros = jnp.zeros((blk * TT, LANES), F32)
    w_in = lambda i, ps, pe: (layer, i, 0, 0)
    w_o = lambda i, ps, pe: (i, 0, 0)
    return pl.pallas_call(
        functools.partial(_dispatch_kernel, tm=tm, blk=blk, n_blocks=n_rows // blk),
        grid_spec=pltpu.PrefetchScalarGridSpec(
            num_scalar_prefetch=2,
            grid=(n_tiles,),
            in_specs=[
                pl.BlockSpec((1, 1, tm), lambda i, ps, pe: (i, 0, 0),
                             memory_space=pltpu.SMEM),
                pl.BlockSpec((tm * TT, LANES), lambda i, ps, pe: (i, 0)),
                pl.BlockSpec((blk * TT, LANES), lambda i, ps, pe: (0, 0)),
                pl.BlockSpec((1, 1, D_MODEL, D_EXPERT), w_in),
                pl.BlockSpec((1, 1, D_MODEL, D_EXPERT), w_in),
                pl.BlockSpec((1, 1, D_EXPERT, D_MODEL), w_in),
            ],
            out_specs=[
                pl.BlockSpec(memory_space=pl.ANY),
                pl.BlockSpec((1, D_MODEL, D_EXPERT), w_o),
                pl.BlockSpec((1, D_MODEL, D_EXPERT), w_o),
                pl.BlockSpec((1, D_EXPERT, D_MODEL), w_o),
            ],
            scratch_shapes=[pltpu.SemaphoreType.DMA(()), pltpu.SemaphoreType.DMA(())],
        ),
        out_shape=[
            jax.ShapeDtypeStruct((n_rows * TT, LANES), F32),
            jax.ShapeDtypeStruct((N_EXPERTS, D_MODEL, D_EXPERT), BF16),
            jax.ShapeDtypeStruct((N_EXPERTS, D_MODEL, D_EXPERT), BF16),
            jax.ShapeDtypeStruct((N_EXPERTS, D_EXPERT, D_MODEL), BF16),
        ],
        compiler_params=pltpu.CompilerParams(
            dimension_semantics=("arbitrary",), vmem_limit_bytes=VMEM_LIMIT),
        name="moe_dispatch",
    )(pad_starts, pad_ends, dest_tiles, x_tt, zeros, w_gate, w_up, w_down)


def _expert_kernel(ea_ref, eb_ref, ns_ref, xs_ref, g_ref, wr_ref, br_ref, *refs, blk):
    i = pl.program_id(0)
    w_refs, y_ref = refs[:-1], refs[-1]

    def unit(k):
        ea = ea_ref[i * EXPERT_UNITS + k]
        eb = eb_ref[i * EXPERT_UNITS + k]
        wga_ref, wua_ref, wda_ref, wgb_ref, wub_ref, wdb_ref = w_refs[6 * k:6 * k + 6]
        x = _load_tt(xs_ref, blk, k * blk)
        h = _rms(x, g_ref[...]).astype(BF16)

        logits = _dot(h, wr_ref[...]) + br_ref[...]
        lane = lax.broadcasted_iota(jnp.int32, logits.shape, 1)
        is_group = (lane >= GROUP_LANE) & (lane < GROUP_LANE + N_GROUPS)
        pick = lambda l: jnp.sum(jnp.where(lane == l, logits, 0.0), axis=-1, keepdims=True)
        gmax = jnp.max(jnp.where(is_group, logits, -jnp.inf), axis=-1, keepdims=True)
        gsum = jnp.sum(jnp.where(is_group, jnp.exp(logits - gmax), 0.0),
                       axis=-1, keepdims=True)
        g_prob = jnp.exp(pick(GROUP_LANE + ea // EXPERTS_PER_GROUP) - gmax) / gsum
        la = pick(EXPERT_LANE + ea)
        lb = pick(EXPERT_LANE + eb)
        m = jnp.maximum(la, lb)
        pa = jnp.exp(la - m)
        pb = jnp.exp(lb - m)
        scale = g_prob / (pa + pb)
        gate_a = pa * scale
        gate_b = pb * scale

        hid_a = (jax.nn.silu(_dot(h, wga_ref[0])) * _dot(h, wua_ref[0])).astype(BF16)
        hid_b = (jax.nn.silu(_dot(h, wgb_ref[0])) * _dot(h, wub_ref[0])).astype(BF16)
        _store_tt(y_ref, gate_a * _dot(hid_a, wda_ref[0])
                  + gate_b * _dot(hid_b, wdb_ref[0]), k * blk)

    @pl.when(i < ns_ref[0])
    def _():
        for k in range(EXPERT_UNITS):
            unit(k)

    @pl.when(i >= ns_ref[0])
    def _():
        y_ref[...] = jnp.zeros_like(y_ref)


def _experts(xs, g, wr, br, w_gate, w_up, w_down, block_ea, block_eb, n_used, blk):
    n_blocks = xs.shape[0] // (blk * TT)
    assert n_blocks % EXPERT_UNITS == 0
    n_steps = (n_used + EXPERT_UNITS - 1) // EXPERT_UNITS
    last = lambda i, ea, eb, ns: (jnp.minimum(i, ns[0] - 1), 0)
    const = lambda i, ea, eb, ns: (0, 0)
    w_specs = []
    for k in range(EXPERT_UNITS):
        w_a = lambda i, ea, eb, ns, k=k: (ea[i * EXPERT_UNITS + k], 0, 0)
        w_b = lambda i, ea, eb, ns, k=k: (eb[i * EXPERT_UNITS + k], 0, 0)
        for idx in (w_a, w_b):
            w_specs += [pl.BlockSpec((1, D_MODEL, D_EXPERT), idx),
                        pl.BlockSpec((1, D_MODEL, D_EXPERT), idx),
                        pl.BlockSpec((1, D_EXPERT, D_MODEL), idx)]
    wr2 = jnp.zeros((D_MODEL, LANES), F32).at[:, 0:ROUTER_ROWS].set(wr.T)
    br2 = jnp.zeros((1, LANES), F32).at[0, 0:ROUTER_ROWS].set(br)
    rows = EXPERT_UNITS * blk * TT
    return pl.pallas_call(
        functools.partial(_expert_kernel, blk=blk),
        grid_spec=pltpu.PrefetchScalarGridSpec(
            num_scalar_prefetch=3,
            grid=(n_blocks // EXPERT_UNITS,),
            in_specs=[
                pl.BlockSpec((rows, LANES), last),
                pl.BlockSpec((1, D_MODEL), const),
                pl.BlockSpec((D_MODEL, LANES), const),
                pl.BlockSpec((1, LANES), const),
            ] + w_specs,
            out_specs=pl.BlockSpec((rows, LANES), lambda i, ea, eb, ns: (i, 0)),
        ),
        out_shape=jax.ShapeDtypeStruct(xs.shape, F32),
        compiler_params=pltpu.CompilerParams(
            dimension_semantics=("arbitrary",),
            vmem_limit_bytes=VMEM_LIMIT),
        name="moe_experts",
    )(block_ea, block_eb, n_steps, xs, g.reshape(1, D_MODEL), wr2.astype(BF16), br2,
      *([w_gate, w_up, w_down] * (2 * EXPERT_UNITS)))


def _final_kernel(dcur_ref, dnext_ref, x_ref, y_hbm, gf_ref, o_ref, yg_ref, gsems,
                  *, tm):
    out = _gather_combine(dcur_ref, dnext_ref, x_ref, y_hbm, yg_ref, gsems, tm)
    o_ref[...] = _rms(out, gf_ref[...])


def _final(x_tt, moe_out, g_final, tm):
    y, dest_tiles = moe_out
    n_tok = x_tt.shape[0] // TT
    n_tiles = n_tok // tm
    gc_specs, gc_scratch = _gather_combine_specs(n_tiles, tm)
    return pl.pallas_call(
        functools.partial(_final_kernel, tm=tm),
        grid=(n_tiles,),
        in_specs=gc_specs + [pl.BlockSpec((1, D_MODEL), lambda i: (0, 0))],
        out_specs=pl.BlockSpec((tm, D_MODEL), lambda i: (i, 0)),
        out_shape=jax.ShapeDtypeStruct((n_tok, D_MODEL), F32),
        scratch_shapes=gc_scratch,
        compiler_params=pltpu.CompilerParams(
            dimension_semantics=("arbitrary",),
            vmem_limit_bytes=VMEM_LIMIT),
        name="moe_combine_final",
    )(dest_tiles, dest_tiles, x_tt, y, g_final.reshape(1, D_MODEL))


def _moe(x_tt, ids, cnt, g, wr, br, w_gate, w_up, w_down, layer):
    n_tok = x_tt.shape[0] // TT
    tm, td, blk = _tiles(n_tok)
    n_blocks = n_tok // blk + N_CLASSES

    counts = cnt[:, 0]
    padded = (counts + blk - 1) // blk * blk
    pad_ends = jnp.cumsum(padded).astype(jnp.int32)
    pad_starts = pad_ends - padded
    block_start = jnp.arange(n_blocks, dtype=jnp.int32) * blk
    block_cls = jnp.minimum(
        jnp.sum(pad_ends[None, :N_CLASSES] <= block_start[:, None], axis=1),
        N_CLASSES - 1)
    block_ea = jnp.asarray(CLASS_LO)[block_cls]
    block_eb = jnp.asarray(CLASS_HI)[block_cls]
    n_used = (pad_ends[N_CLASSES - 1:N_CLASSES] // blk).astype(jnp.int32)

    dest = _dest(ids, pad_starts, DEST_TILE if n_tok % DEST_TILE == 0 else tm)
    per_tile = lambda t: dest[0].reshape(n_tok // t, 1, t)

    xs, wg_bf, wu_bf, wd_bf = _dispatch(x_tt, per_tile(td), pad_starts, pad_ends,
                                        w_gate, w_up, w_down, layer,
                                        n_blocks * blk, td, blk)
    y = _experts(xs, g, wr, br, wg_bf, wu_bf, wd_bf, block_ea, block_eb, n_used, blk)
    return y, per_tile(tm)


def kernel(x, norm_mix_g, norm_ffn_g, norm_final_g, ab_w_in, a_ln_g, a_ws, a_ws_b, b_conv_w, ab_w_out, c_w_in, c_conv_w, c_conv_b, c_w_a, c_b_a, c_w_x, c_b_x, c_lambda, c_w_out, moe_w_rg, moe_b_rg, moe_w_re, moe_b_re, moe_w_gate, moe_w_up, moe_w_down):
    bsz, t_len, d = x.shape
    assert d == D_MODEL
    tm, _, _ = _tiles(bsz * t_len)
    assert t_len % tm == 0 and tm % CHUNK == 0

    routers = [_router_weights(moe_w_rg[l], moe_b_rg[l], moe_w_re[l], moe_b_re[l])
               for l in range(2)]
    route_args = [_route_args(norm_ffn_g[l], *routers[l]) for l in range(2)]

    def moe(x_tt, ids, cnt, layer):
        return _moe(x_tt, ids, cnt, norm_ffn_g[layer], *routers[layer],
                    moe_w_gate, moe_w_up, moe_w_down, layer)

    x1, ids, cnt = _mixer_ab(x, norm_mix_g[0], ab_w_in[0], a_ln_g[0], a_ws[0],
                             a_ws_b[0], b_conv_w[0], ab_w_out[0], route_args[0],
                             AB_TILE if t_len % AB_TILE == 0 else tm)
    x3, ids, cnt = _mixer_rglru(x1, moe(x1, ids, cnt, 0), bsz, t_len, norm_mix_g[1],
                                c_w_in[0], c_conv_w[0], c_conv_b[0], c_w_a[0],
                                c_b_a[0], c_w_x[0], c_b_x[0], c_lambda[0],
                                c_w_out[0], route_args[1], tm)
    out = _final(x3, moe(x3, ids, cnt, 1), norm_final_g, tm)
    return out.reshape(bsz, t_len, d)
```

```python
import functools

import numpy as np
import jax
import jax.numpy as jnp
from jax import lax
from jax.experimental import pallas as pl
from jax.experimental.pallas import tpu as pltpu

D_MODEL = 1024
LANES = 128
SUBLANES = 8
TT = D_MODEL // LANES
assert TT == SUBLANES

A_HEADS = 4
A_HEAD_DIM = 128
D_A = A_HEADS * A_HEAD_DIM
CHUNK = 128
D_B = D_MODEL - D_A
B_CONV = 3
D_RNN = D_MODEL
LRU_HEADS = 8
LRU_HEAD_DIM = D_RNN // LRU_HEADS
C_CONV = 4
LRU_C = 8.0
N_GROUPS = 4
EXPERTS_PER_GROUP = 8
N_EXPERTS = N_GROUPS * EXPERTS_PER_GROUP
D_EXPERT = 512
EPS = 1e-6

N_PAIRS = EXPERTS_PER_GROUP * (EXPERTS_PER_GROUP - 1) // 2
N_CLASSES = N_GROUPS * N_PAIRS
CLS_PAD = 128
assert N_CLASSES <= CLS_PAD
GROUP_LANE = 0
EXPERT_LANE = 8
ROUTER_ROWS = 48
VMEM_LIMIT = 56 * 1024 * 1024

BF16 = jnp.bfloat16
F32 = jnp.float32

_PAIRS = [(lo, hi) for lo in range(EXPERTS_PER_GROUP)
          for hi in range(lo + 1, EXPERTS_PER_GROUP)]
CLASS_LO = np.array([g * EXPERTS_PER_GROUP + lo
                     for g in range(N_GROUPS) for lo, _ in _PAIRS], np.int32)
CLASS_HI = np.array([g * EXPERTS_PER_GROUP + hi
                     for g in range(N_GROUPS) for _, hi in _PAIRS], np.int32)


def _tiles(n_tok):
    tm = 512 if n_tok % 512 == 0 else 256
    td = n_tok // N_EXPERTS
    assert td * N_EXPERTS == n_tok and td % DMA_UNROLL == 0
    blk = 256 if n_tok >= 8192 else 128
    return tm, td, blk


DMA_UNROLL = 64
DMA_PRIORITIES = 2
STAGE_COLS = 512
DEST_TILE = 4096
FINAL_TILE = 1024
AB_TILE = 1024
SCAN_UNROLL = 8
EXPERT_UNITS = 2


def _load_tt(ref, nrows, row0=0):
    return jnp.concatenate(
        [ref[pl.ds(row0 * TT + s, nrows, stride=TT), :] for s in range(TT)], axis=1)


def _store_tt(ref, val, row0=0):
    nrows = val.shape[0]
    for s in range(TT):
        ref[pl.ds(row0 * TT + s, nrows, stride=TT), :] = val[:, s * LANES:(s + 1) * LANES]


def _rms(x, g):
    ms = jnp.mean(x * x, axis=-1, keepdims=True)
    return x * lax.rsqrt(ms + EPS) * g


def _dot(a, b):
    return jnp.dot(a, b, preferred_element_type=F32)


def _shift_rows(cur, prev, k):
    rolled = pltpu.roll(cur, k, axis=0)
    row = lax.broadcasted_iota(jnp.int32, prev.shape, 0)
    head = jnp.where(row < k, pltpu.roll(prev, k, axis=0), rolled[0:SUBLANES, :])
    return jnp.concatenate([head, rolled[SUBLANES:, :]], axis=0)


def _stage_bf16(w_hbm, w_bf, stage, sem):
    cols = w_hbm.shape[1]
    cw = min(cols, stage.shape[1])
    for c0 in range(0, cols, cw):
        cp = pltpu.make_async_copy(w_hbm.at[:, pl.ds(c0, cw)],
                                   stage.at[:, pl.ds(0, cw)], sem)
        cp.start()
        cp.wait()
        w_bf[:, c0:c0 + cw] = stage[:, 0:cw].astype(BF16)


def _gather_combine(dcur_ref, dnext_ref, x_ref, y_hbm, yg_ref, sems, tm):
    i = pl.program_id(0)
    slot = i % 2

    def issue(dref, sl):
        def body(jj, _):
            for u in range(DMA_UNROLL):
                j = jj * DMA_UNROLL + u
                d = dref[0, 0, j]
                pltpu.make_async_copy(
                    y_hbm.at[pl.ds(pl.multiple_of(d * TT, TT), TT), :],
                    yg_ref.at[sl, pl.ds(pl.multiple_of(j * TT, TT), TT), :],
                    sems.at[sl]).start(priority=u % DMA_PRIORITIES)
            return 0

        lax.fori_loop(0, tm // DMA_UNROLL, body, 0)

    @pl.when(i == 0)
    def _():
        issue(dcur_ref, 0)

    @pl.when(i + 1 < pl.num_programs(0))
    def _():
        issue(dnext_ref, 1 - slot)

    pltpu.make_async_copy(y_hbm.at[pl.ds(0, tm * TT), :], yg_ref.at[slot],
                          sems.at[slot]).wait()
    return _load_tt(x_ref, tm) + _load_tt(yg_ref.at[slot], tm)


def _gather_combine_specs(n_tiles, tm):
    in_specs = [
        pl.BlockSpec((1, 1, tm), lambda i: (i, 0, 0), memory_space=pltpu.SMEM),
        pl.BlockSpec((1, 1, tm), lambda i: (jnp.minimum(i + 1, n_tiles - 1), 0, 0),
                     memory_space=pltpu.SMEM),
        pl.BlockSpec((tm * TT, LANES), lambda i: (i, 0)),
        pl.BlockSpec(memory_space=pl.ANY),
    ]
    scratch = [
        pltpu.VMEM((2, tm * TT, LANES), F32),
        pltpu.SemaphoreType.DMA((2,)),
    ]
    return in_specs, scratch


def _mixer_ab_kernel(x_ref, g_ref, win_hbm, lng_ref, ws_ref, wsb_ref, cw_ref,
                     wout_hbm, gf_ref, wr_ref, br_ref, o_ref, ids_ref, cnt_ref,
                     carry_ref, win_ref, wout_ref, stage_ref, wsem,
                     rcarry_ref, before_ref, *, tm):
    t = pl.program_id(1)
    first = (pl.program_id(0) == 0) & (t == 0)

    @pl.when(first)
    def _():
        _stage_bf16(win_hbm, win_ref, stage_ref, wsem)
        _stage_bf16(wout_hbm, wout_ref, stage_ref, wsem)

    x = x_ref[0]
    h = _rms(x, g_ref[...]).astype(BF16)

    u = jax.nn.gelu(_dot(h, win_ref[:, 0:D_A]))
    v = jax.nn.gelu(_dot(h, win_ref[:, D_A:2 * D_A]))
    mu = jnp.mean(v, axis=-1, keepdims=True)
    vc = v - mu
    var = jnp.mean(vc * vc, axis=-1, keepdims=True)
    vn = (vc * lax.rsqrt(var + EPS) * lng_ref[...]).astype(BF16)

    r_i = lax.broadcasted_iota(jnp.int32, (CHUNK, CHUNK), 0)
    c_i = lax.broadcasted_iota(jnp.int32, (CHUNK, CHUNK), 1)
    causal = r_i >= c_i
    head_cols = []
    for hh in range(A_HEADS):
        wsh = jnp.where(causal, ws_ref[hh], 0.0).astype(BF16)
        rows = []
        for c in range(tm // CHUNK):
            blk = vn[c * CHUNK:(c + 1) * CHUNK,
                     hh * A_HEAD_DIM:(hh + 1) * A_HEAD_DIM]
            rows.append(_dot(wsh, blk))
        head_cols.append(jnp.concatenate(rows, axis=0))
    mixed = jnp.concatenate(head_cols, axis=1) + wsb_ref[...]
    y_a = (u * mixed).astype(BF16)

    gate_b = _dot(h, win_ref[:, 2 * D_A:2 * D_A + D_B])
    gate_c = _dot(h, win_ref[:, 2 * D_A + D_B:2 * D_A + 2 * D_B])
    xb = _dot(h, win_ref[:, 2 * D_A + 2 * D_B:2 * D_A + 3 * D_B])
    cx = gate_c * xb

    @pl.when(t == 0)
    def _():
        carry_ref[...] = jnp.zeros_like(carry_ref)

    prev = carry_ref[...]
    conv = (_shift_rows(cx, prev, 2) * cw_ref[0:1, :]
            + _shift_rows(cx, prev, 1) * cw_ref[1:2, :]
            + cx * cw_ref[2:3, :])
    carry_ref[...] = cx[tm - SUBLANES:tm, :]
    y_b = (gate_b * conv).astype(BF16)

    out = x + _dot(y_a, wout_ref[0:D_A, :]) + _dot(y_b, wout_ref[D_A:D_MODEL, :])
    _store_tt(o_ref, out)
    _route(out, first, gf_ref, wr_ref, br_ref, ids_ref, cnt_ref, rcarry_ref,
           before_ref, tm)


def _mixer_ab(x, g, w_in, ln_g, ws, ws_b, conv_w, w_out, route_args, tm):
    bsz, t_len, d = x.shape
    nt = t_len // tm
    wsb_full = jnp.tile(jnp.repeat(ws_b.T, A_HEAD_DIM, axis=1), (tm // CHUNK, 1))
    full = lambda shape: pl.BlockSpec(shape, lambda b, t: (0,) * len(shape))
    hbm = pl.BlockSpec(memory_space=pl.ANY)
    r_in, r_out, r_shapes, r_scratch = _route_specs(lambda b, t: b * nt + t, tm,
                                                    bsz * t_len)
    return pl.pallas_call(
        functools.partial(_mixer_ab_kernel, tm=tm),
        grid=(bsz, nt),
        in_specs=[
            pl.BlockSpec((1, tm, d), lambda b, t: (b, t, 0)),
            full((1, d)),
            hbm,
            full((1, D_A)),
            full(ws.shape),
            full((tm, D_A)),
            full(conv_w.shape),
            hbm,
        ] + r_in,
        out_specs=[pl.BlockSpec((tm * TT, LANES), lambda b, t: (b * nt + t, 0))] + r_out,
        out_shape=[jax.ShapeDtypeStruct((bsz * t_len * TT, LANES), F32)] + r_shapes,
        scratch_shapes=[
            pltpu.VMEM((SUBLANES, D_B), F32),
            pltpu.VMEM(w_in.shape, BF16),
            pltpu.VMEM(w_out.shape, BF16),
            pltpu.VMEM((d, STAGE_COLS), F32),
            pltpu.SemaphoreType.DMA(()),
        ] + r_scratch,
        compiler_params=pltpu.CompilerParams(
            dimension_semantics=("arbitrary", "arbitrary"),
            vmem_limit_bytes=VMEM_LIMIT),
        name="mixer_ab",
    )(x, g.reshape(1, d), w_in, ln_g.reshape(1, D_A), ws, wsb_full, conv_w, w_out,
      *route_args)


def _mixer_rglru_kernel(dcur_ref, dnext_ref, x_ref, y_hbm,
                        g_ref, win_hbm, cw_ref, cb_ref, wa_hbm, ba_ref,
                        wx_hbm, bx_ref, lam_ref, wout_hbm, gf_ref, wr_ref, br_ref,
                        o_ref, ids_ref, cnt_ref,
                        yg_ref, gsems,
                        carry_ref, hstate_ref, a_scr, b_scr, h_scr,
                        win_ref, wa_ref, wx_ref, wout_ref, stage_ref, wsem,
                        rcarry_ref, before_ref,
                        *, tm, nt, seg, pitch):
    t = pl.program_id(0) % nt
    first = pl.program_id(0) == 0

    @pl.when(first)
    def _():
        _stage_bf16(win_hbm, win_ref, stage_ref, wsem)
        _stage_bf16(wa_hbm, wa_ref, stage_ref, wsem)
        _stage_bf16(wx_hbm, wx_ref, stage_ref, wsem)
        _stage_bf16(wout_hbm, wout_ref, stage_ref, wsem)

    x = _gather_combine(dcur_ref, dnext_ref, x_ref, y_hbm, yg_ref, gsems, tm)
    h = _rms(x, g_ref[...]).astype(BF16)
    gate = jax.nn.gelu(_dot(h, win_ref[:, 0:D_RNN]))
    xr0 = _dot(h, win_ref[:, D_RNN:2 * D_RNN])

    @pl.when(t == 0)
    def _():
        carry_ref[...] = jnp.zeros_like(carry_ref)
        hstate_ref[...] = jnp.zeros_like(hstate_ref)

    prev = carry_ref[...]
    xr = (_shift_rows(xr0, prev, 3) * cw_ref[0:1, :]
          + _shift_rows(xr0, prev, 2) * cw_ref[1:2, :]
          + _shift_rows(xr0, prev, 1) * cw_ref[2:3, :]
          + xr0 * cw_ref[3:4, :]) + cb_ref[...]
    carry_ref[...] = xr0[tm - SUBLANES:tm, :]

    xrb = xr.astype(BF16)
    r_cols, i_cols = [], []
    for hh in range(LRU_HEADS):
        blk = xrb[:, hh * LRU_HEAD_DIM:(hh + 1) * LRU_HEAD_DIM]
        rows = slice(hh * LRU_HEAD_DIM, (hh + 1) * LRU_HEAD_DIM)
        r_cols.append(_dot(blk, wa_ref[rows, :]))
        i_cols.append(_dot(blk, wx_ref[rows, :]))
    sigmoid = lambda v: 0.5 * jnp.tanh(0.5 * v) + 0.5
    r = sigmoid(jnp.concatenate(r_cols, axis=1) + ba_ref[...])
    i = sigmoid(jnp.concatenate(i_cols, axis=1) + bx_ref[...])

    lam = lam_ref[...]
    log_sig = jnp.minimum(lam, 0.0) - jnp.log1p(jnp.exp(-jnp.abs(lam)))
    log_a = (LRU_C * r) * log_sig
    a = jnp.exp(log_a)
    z = -jnp.tanh(log_a) * (a * a + 1.0)
    b = jnp.where(z > 0.0, z * lax.rsqrt(z), 0.0) * (i * xr)

    n_lg = D_RNN // LANES

    def put(scr, val):
        for c in range(n_lg):
            for s in range(SUBLANES):
                scr[c, s * pitch:s * pitch + seg, :] = (
                    val[s * seg:(s + 1) * seg, c * LANES:(c + 1) * LANES])

    def ld(scr, j):
        return jnp.concatenate(
            [scr[c, pl.ds(j, SUBLANES, stride=pitch), :] for c in range(n_lg)],
            axis=1)

    def st(scr, j, val):
        for c in range(n_lg):
            scr[c, pl.ds(j, SUBLANES, stride=pitch), :] = (
                val[:, c * LANES:(c + 1) * LANES])

    put(a_scr, a)
    put(b_scr, b)

    def seg_step(j, carry):
        hl, pl_ = carry
        aj = ld(a_scr, j)
        hl = aj * hl + ld(b_scr, j)
        pl_ = aj * pl_
        st(h_scr, j, hl)
        st(a_scr, j, pl_)
        return hl, pl_

    zeros = jnp.zeros((SUBLANES, D_RNN), F32)
    def seg_steps(jj, carry):
        for u in range(SCAN_UNROLL):
            carry = seg_step(jj * SCAN_UNROLL + u, carry)
        return carry

    h_end, p_end = lax.fori_loop(0, seg // SCAN_UNROLL, seg_steps, (zeros, zeros + 1.0))

    def seg_rows(scr, s):
        return jnp.concatenate(
            [scr[c, s * pitch:s * pitch + seg, :] for c in range(n_lg)], axis=1)

    c = hstate_ref[...]
    segs = []
    for s in range(SUBLANES):
        segs.append(seg_rows(h_scr, s) + seg_rows(a_scr, s) * c)
        c = h_end[s:s + 1, :] + p_end[s:s + 1, :] * c
    hstate_ref[...] = c
    hseq = jnp.concatenate(segs, axis=0)
    out = x + _dot((gate * hseq).astype(BF16), wout_ref[...])
    _store_tt(o_ref, out)
    _route(out, first, gf_ref, wr_ref, br_ref, ids_ref, cnt_ref, rcarry_ref,
           before_ref, tm)


def _mixer_rglru(x_tt, moe_out, bsz, t_len, g, w_in, conv_w, conv_b, w_a, b_a,
                 w_x, b_x, lam, w_out, route_args, tm):
    y, dest_tiles = moe_out
    nt = t_len // tm
    n_tiles = bsz * nt
    seg = tm // SUBLANES
    pitch = seg + SUBLANES
    d = D_MODEL
    full = lambda shape: pl.BlockSpec(shape, lambda i: (0,) * len(shape))
    hbm = pl.BlockSpec(memory_space=pl.ANY)
    row = lambda v: v.reshape(1, -1)
    gc_specs, gc_scratch = _gather_combine_specs(n_tiles, tm)
    r_in, r_out, r_shapes, r_scratch = _route_specs(lambda i: i, tm, n_tiles * tm)
    return pl.pallas_call(
        functools.partial(_mixer_rglru_kernel, tm=tm, nt=nt, seg=seg, pitch=pitch),
        grid=(n_tiles,),
        in_specs=gc_specs + [
            full((1, d)),
            hbm,
            full(conv_w.shape),
            full((1, D_RNN)),
            hbm,
            full((1, D_RNN)),
            hbm,
            full((1, D_RNN)),
            full((1, D_RNN)),
            hbm,
        ] + r_in,
        out_specs=[pl.BlockSpec((tm * TT, LANES), lambda i: (i, 0))] + r_out,
        out_shape=[jax.ShapeDtypeStruct(x_tt.shape, F32)] + r_shapes,
        scratch_shapes=gc_scratch + [
            pltpu.VMEM((SUBLANES, D_RNN), F32),
            pltpu.VMEM((1, D_RNN), F32),
            pltpu.VMEM((D_RNN // LANES, SUBLANES * pitch, LANES), F32),
            pltpu.VMEM((D_RNN // LANES, SUBLANES * pitch, LANES), F32),
            pltpu.VMEM((D_RNN // LANES, SUBLANES * pitch, LANES), F32),
            pltpu.VMEM((d, 2 * D_RNN), BF16),
            pltpu.VMEM((D_RNN, LRU_HEAD_DIM), BF16),
            pltpu.VMEM((D_RNN, LRU_HEAD_DIM), BF16),
            pltpu.VMEM((D_RNN, d), BF16),
            pltpu.VMEM((d, STAGE_COLS), F32),
            pltpu.SemaphoreType.DMA(()),
        ] + r_scratch,
        compiler_params=pltpu.CompilerParams(
            dimension_semantics=("arbitrary",),
            vmem_limit_bytes=VMEM_LIMIT),
        name="mixer_rglru",
    )(dest_tiles, dest_tiles, x_tt, y, row(g), w_in, conv_w, row(conv_b),
      w_a.reshape(D_RNN, LRU_HEAD_DIM), row(b_a), w_x.reshape(D_RNN, LRU_HEAD_DIM),
      row(b_x), row(lam), w_out, *route_args)


def _route(x, first, gf_ref, wr_ref, br_ref, ids_ref, cnt_ref, carry_ref,
           before_ref, tm):
    h = _rms(x, gf_ref[...]).astype(BF16)
    lt = lax.dot_general(wr_ref[...], h, (((1,), (1,)), ((), ())),
                         preferred_element_type=F32) + br_ref[...]
    gl = lt[GROUP_LANE:GROUP_LANE + N_GROUPS, :]
    gmax = jnp.max(gl, axis=0, keepdims=True)
    g_iota = lax.broadcasted_iota(jnp.int32, gl.shape, 0)
    g_idx = jnp.min(jnp.where(gl == gmax, g_iota, N_GROUPS), axis=0, keepdims=True)

    esel = lt[EXPERT_LANE:EXPERT_LANE + EXPERTS_PER_GROUP, :]
    for gidx in range(1, N_GROUPS):
        lo = EXPERT_LANE + gidx * EXPERTS_PER_GROUP
        esel = jnp.where(g_idx == gidx, lt[lo:lo + EXPERTS_PER_GROUP, :], esel)
    e_iota = lax.broadcasted_iota(jnp.int32, esel.shape, 0)
    top1 = jnp.max(esel, axis=0, keepdims=True)
    i1 = jnp.min(jnp.where(esel == top1, e_iota, EXPERTS_PER_GROUP), axis=0, keepdims=True)
    rest = jnp.where(e_iota == i1, -jnp.inf, esel)
    top2 = jnp.max(rest, axis=0, keepdims=True)
    i2 = jnp.min(jnp.where(rest == top2, e_iota, EXPERTS_PER_GROUP), axis=0, keepdims=True)

    lo_e = jnp.minimum(i1, i2)
    hi_e = jnp.maximum(i1, i2)
    pair = jnp.right_shift(lo_e * (2 * EXPERTS_PER_GROUP - 1 - lo_e), 1) + (hi_e - lo_e - 1)
    cls = g_idx * N_PAIRS + pair

    c_iota = lax.broadcasted_iota(jnp.int32, (CLS_PAD, tm), 0)
    hit = c_iota == cls
    onehot = jnp.where(hit, 1.0, 0.0)

    @pl.when(first)
    def _():
        carry_ref[...] = jnp.zeros_like(carry_ref)
        s_i = lax.broadcasted_iota(jnp.int32, (tm, tm), 0)
        t_i = lax.broadcasted_iota(jnp.int32, (tm, tm), 1)
        before_ref[...] = jnp.where(s_i < t_i, 1.0, 0.0).astype(BF16)

    prefix = _dot(onehot.astype(BF16), before_ref[...]) + carry_ref[:, 0:1]
    rank = jnp.sum(jnp.where(hit, prefix, 0.0), axis=0, keepdims=True)
    carry_ref[...] = carry_ref[...] + jnp.sum(onehot, axis=1, keepdims=True)

    ids_ref[0:1, :] = cls
    ids_ref[1:2, :] = rank.astype(jnp.int32)
    ids_ref[2:8, :] = jnp.zeros((6, tm), jnp.int32)
    cnt_ref[...] = carry_ref[...].astype(jnp.int32)


def _router_weights(w_rg, b_rg, w_re, b_re):
    wr = jnp.zeros((ROUTER_ROWS, D_MODEL), F32)
    wr = wr.at[GROUP_LANE:GROUP_LANE + N_GROUPS].set(w_rg.T)
    wr = wr.at[EXPERT_LANE:EXPERT_LANE + N_EXPERTS].set(w_re.T)
    br = jnp.zeros((ROUTER_ROWS,), F32)
    br = br.at[GROUP_LANE:GROUP_LANE + N_GROUPS].set(b_rg)
    br = br.at[EXPERT_LANE:EXPERT_LANE + N_EXPERTS].set(b_re)
    return wr, br


def _route_specs(tile_idx, tm, n_tok):
    const = lambda shape: pl.BlockSpec(shape, lambda *_: (0,) * len(shape))
    in_specs = [const((1, D_MODEL)), const((ROUTER_ROWS, D_MODEL)),
                const((ROUTER_ROWS, 1))]
    out_specs = [pl.BlockSpec((8, tm), lambda *g: (0, tile_idx(*g))),
                 const((CLS_PAD, LANES))]
    out_shapes = [jax.ShapeDtypeStruct((8, n_tok), jnp.int32),
                  jax.ShapeDtypeStruct((CLS_PAD, LANES), jnp.int32)]
    scratch = [pltpu.VMEM((CLS_PAD, LANES), F32), pltpu.VMEM((tm, tm), BF16)]
    return in_specs, out_specs, out_shapes, scratch


def _route_args(g, wr, br):
    return g.reshape(1, D_MODEL), wr.astype(BF16), br.reshape(ROUTER_ROWS, 1)


def _dest_kernel(ids_ref, pstart_ref, dest_ref):
    ids = ids_ref[...]
    c_iota = lax.broadcasted_iota(jnp.int32, (CLS_PAD, ids.shape[1]), 0)
    pstart = pstart_ref[:, 0:1]
    hit = c_iota == ids[0:1, :]
    base = jnp.sum(jnp.where(hit, pstart, 0), axis=0, keepdims=True)
    dest_ref[0:1, :] = base + ids[1:2, :]
    dest_ref[1:8, :] = jnp.zeros((7, ids.shape[1]), jnp.int32)


def _dest(ids, pad_starts, t):
    n_tok = ids.shape[1]
    pstart = jnp.broadcast_to(pad_starts[:, None], (CLS_PAD, LANES))
    return pl.pallas_call(
        _dest_kernel,
        grid=(n_tok // t,),
        in_specs=[
            pl.BlockSpec((8, t), lambda i: (0, i)),
            pl.BlockSpec((CLS_PAD, LANES), lambda i: (0, 0)),
        ],
        out_specs=pl.BlockSpec((8, t), lambda i: (0, i)),
        out_shape=jax.ShapeDtypeStruct((8, n_tok), jnp.int32),
        name="moe_dest",
    )(ids, pstart)


def _dispatch_kernel(pstart_ref, pend_ref, dest_ref, x_ref, zero_ref,
                     wg_ref, wu_ref, wd_ref, xs_hbm, wg_out, wu_out, wd_out,
                     sem, zsem, *, tm, blk, n_blocks):
    i = pl.program_id(0)

    @pl.when(i == 0)
    def _():
        def zcopy(c):
            start = pl.multiple_of((pend_ref[c] - blk) * TT, blk * TT)
            return pltpu.make_async_copy(
                zero_ref, xs_hbm.at[pl.ds(start, blk * TT), :], zsem)

        def zstart(c, _):
            @pl.when(pend_ref[c] > pstart_ref[c])
            def _():
                zcopy(c).start()
            return 0

        def zwait(c, _):
            @pl.when(pend_ref[c] > pstart_ref[c])
            def _():
                zcopy(c).wait()
            return 0

        def tcopy(b):
            return pltpu.make_async_copy(
                zero_ref,
                xs_hbm.at[pl.ds(pl.multiple_of(b * (blk * TT), blk * TT), blk * TT), :],
                zsem)

        def tstart(b, _):
            tcopy(b).start()
            return 0

        def twait(b, _):
            tcopy(b).wait()
            return 0

        n_used = pend_ref[N_CLASSES - 1] // blk
        lax.fori_loop(0, N_CLASSES, zstart, 0)
        lax.fori_loop(n_used, n_blocks, tstart, 0)
        lax.fori_loop(0, N_CLASSES, zwait, 0)
        lax.fori_loop(n_used, n_blocks, twait, 0)

    def issue(jj, _):
        for u in range(DMA_UNROLL):
            j = jj * DMA_UNROLL + u
            d = dest_ref[0, 0, j]
            pltpu.make_async_copy(
                x_ref.at[pl.ds(pl.multiple_of(j * TT, TT), TT), :],
                xs_hbm.at[pl.ds(pl.multiple_of(d * TT, TT), TT), :],
                sem).start(priority=u % DMA_PRIORITIES)
        return 0

    lax.fori_loop(0, tm // DMA_UNROLL, issue, 0)
    wg_out[0] = wg_ref[0, 0].astype(BF16)
    wu_out[0] = wu_ref[0, 0].astype(BF16)
    wd_out[0] = wd_ref[0, 0].astype(BF16)
    pltpu.make_async_copy(x_ref, xs_hbm.at[pl.ds(0, tm * TT), :], sem).wait()


def _dispatch(x_tt, dest_tiles, pad_starts, pad_ends, w_gate, w_up, w_down, layer,
              n_rows, tm, blk):
    n_tok = x_tt.shape[0] // TT
    n_tiles = n_tok // tm
    assert n_tiles == N_EXPERTS
    zeros = jnp.zeros((blk * TT, LANES), F32)
    w_in = lambda i, ps, pe: (layer, i, 0, 0)
    w_o = lambda i, ps, pe: (i, 0, 0)
    return pl.pallas_call(
        functools.partial(_dispatch_kernel, tm=tm, blk=blk, n_blocks=n_rows // blk),
        grid_spec=pltpu.PrefetchScalarGridSpec(
            num_scalar_prefetch=2,
            grid=(n_tiles,),
            in_specs=[
                pl.BlockSpec((1, 1, tm), lambda i, ps, pe: (i, 0, 0),
                             memory_space=pltpu.SMEM),
                pl.BlockSpec((tm * TT, LANES), lambda i, ps, pe: (i, 0)),
                pl.BlockSpec((blk * TT, LANES), lambda i, ps, pe: (0, 0)),
                pl.BlockSpec((1, 1, D_MODEL, D_EXPERT), w_in),
                pl.BlockSpec((1, 1, D_MODEL, D_EXPERT), w_in),
                pl.BlockSpec((1, 1, D_EXPERT, D_MODEL), w_in),
            ],
            out_specs=[
                pl.BlockSpec(memory_space=pl.ANY),
                pl.BlockSpec((1, D_MODEL, D_EXPERT), w_o),
                pl.BlockSpec((1, D_MODEL, D_EXPERT), w_o),
                pl.BlockSpec((1, D_EXPERT, D_MODEL), w_o),
            ],
            scratch_shapes=[pltpu.SemaphoreType.DMA(()), pltpu.SemaphoreType.DMA(())],
        ),
        out_shape=[
            jax.ShapeDtypeStruct((n_rows * TT, LANES), F32),
            jax.ShapeDtypeStruct((N_EXPERTS, D_MODEL, D_EXPERT), BF16),
            jax.ShapeDtypeStruct((N_EXPERTS, D_MODEL, D_EXPERT), BF16),
            jax.ShapeDtypeStruct((N_EXPERTS, D_EXPERT, D_MODEL), BF16),
        ],
        compiler_params=pltpu.CompilerParams(
            dimension_semantics=("arbitrary",), vmem_limit_bytes=VMEM_LIMIT),
        name="moe_dispatch",
    )(pad_starts, pad_ends, dest_tiles, x_tt, zeros, w_gate, w_up, w_down)


def _expert_kernel(ea_ref, eb_ref, ns_ref, xs_ref, g_ref, wr_ref, br_ref, *refs, blk):
    i = pl.program_id(0)
    w_refs, y_ref = refs[:-1], refs[-1]

    def unit(k):
        ea = ea_ref[i * EXPERT_UNITS + k]
        eb = eb_ref[i * EXPERT_UNITS + k]
        wga_ref, wua_ref, wda_ref, wgb_ref, wub_ref, wdb_ref = w_refs[6 * k:6 * k + 6]
        x = _load_tt(xs_ref, blk, k * blk)
        h = _rms(x, g_ref[...]).astype(BF16)

        logits = _dot(h, wr_ref[...]) + br_ref[...]
        lane = lax.broadcasted_iota(jnp.int32, logits.shape, 1)
        is_group = (lane >= GROUP_LANE) & (lane < GROUP_LANE + N_GROUPS)
        pick = lambda l: jnp.sum(jnp.where(lane == l, logits, 0.0), axis=-1, keepdims=True)
        gmax = jnp.max(jnp.where(is_group, logits, -jnp.inf), axis=-1, keepdims=True)
        gsum = jnp.sum(jnp.where(is_group, jnp.exp(logits - gmax), 0.0),
                       axis=-1, keepdims=True)
        g_prob = jnp.exp(pick(GROUP_LANE + ea // EXPERTS_PER_GROUP) - gmax) / gsum
        la = pick(EXPERT_LANE + ea)
        lb = pick(EXPERT_LANE + eb)
        m = jnp.maximum(la, lb)
        pa = jnp.exp(la - m)
        pb = jnp.exp(lb - m)
        scale = g_prob / (pa + pb)
        gate_a = pa * scale
        gate_b = pb * scale

        hid_a = (jax.nn.silu(_dot(h, wga_ref[0])) * _dot(h, wua_ref[0])).astype(BF16)
        hid_b = (jax.nn.silu(_dot(h, wgb_ref[0])) * _dot(h, wub_ref[0])).astype(BF16)
        _store_tt(y_ref, gate_a * _dot(hid_a, wda_ref[0])
                  + gate_b * _dot(hid_b, wdb_ref[0]), k * blk)

    @pl.when(i < ns_ref[0])
    def _():
        for k in range(EXPERT_UNITS):
            unit(k)

    @pl.when(i >= ns_ref[0])
    def _():
        y_ref[...] = jnp.zeros_like(y_ref)


def _experts(xs, g, wr, br, w_gate, w_up, w_down, block_ea, block_eb, n_used, blk):
    n_blocks = xs.shape[0] // (blk * TT)
    assert n_blocks % EXPERT_UNITS == 0
    n_steps = (n_used + EXPERT_UNITS - 1) // EXPERT_UNITS
    last = lambda i, ea, eb, ns: (jnp.minimum(i, ns[0] - 1), 0)
    const = lambda i, ea, eb, ns: (0, 0)
    w_specs = []
    for k in range(EXPERT_UNITS):
        w_a = lambda i, ea, eb, ns, k=k: (ea[i * EXPERT_UNITS + k], 0, 0)
        w_b = lambda i, ea, eb, ns, k=k: (eb[i * EXPERT_UNITS + k], 0, 0)
        for idx in (w_a, w_b):
            w_specs += [pl.BlockSpec((1, D_MODEL, D_EXPERT), idx),
                        pl.BlockSpec((1, D_MODEL, D_EXPERT), idx),
                        pl.BlockSpec((1, D_EXPERT, D_MODEL), idx)]
    wr2 = jnp.zeros((D_MODEL, LANES), F32).at[:, 0:ROUTER_ROWS].set(wr.T)
    br2 = jnp.zeros((1, LANES), F32).at[0, 0:ROUTER_ROWS].set(br)
    rows = EXPERT_UNITS * blk * TT
    return pl.pallas_call(
        functools.partial(_expert_kernel, blk=blk),
        grid_spec=pltpu.PrefetchScalarGridSpec(
            num_scalar_prefetch=3,
            grid=(n_blocks // EXPERT_UNITS,),
            in_specs=[
                pl.BlockSpec((rows, LANES), last),
                pl.BlockSpec((1, D_MODEL), const),
                pl.BlockSpec((D_MODEL, LANES), const),
                pl.BlockSpec((1, LANES), const),
            ] + w_specs,
            out_specs=pl.BlockSpec((rows, LANES), lambda i, ea, eb, ns: (i, 0)),
        ),
        out_shape=jax.ShapeDtypeStruct(xs.shape, F32),
        compiler_params=pltpu.CompilerParams(
            dimension_semantics=("arbitrary",),
            vmem_limit_bytes=VMEM_LIMIT),
        name="moe_experts",
    )(block_ea, block_eb, n_steps, xs, g.reshape(1, D_MODEL), wr2.astype(BF16), br2,
      *([w_gate, w_up, w_down] * (2 * EXPERT_UNITS)))


def _final_kernel(dcur_ref, dnext_ref, x_ref, y_hbm, gf_ref, o_ref, yg_ref, gsems,
                  *, tm):
    out = _gather_combine(dcur_ref, dnext_ref, x_ref, y_hbm, yg_ref, gsems, tm)
    o_ref[...] = _rms(out, gf_ref[...])


def _final(x_tt, moe_out, g_final, tm):
    y, dest_tiles = moe_out
    n_tok = x_tt.shape[0] // TT
    n_tiles = n_tok // tm
    gc_specs, gc_scratch = _gather_combine_specs(n_tiles, tm)
    return pl.pallas_call(
        functools.partial(_final_kernel, tm=tm),
        grid=(n_tiles,),
        in_specs=gc_specs + [pl.BlockSpec((1, D_MODEL), lambda i: (0, 0))],
        out_specs=pl.BlockSpec((tm, D_MODEL), lambda i: (i, 0)),
        out_shape=jax.ShapeDtypeStruct((n_tok, D_MODEL), F32),
        scratch_shapes=gc_scratch,
        compiler_params=pltpu.CompilerParams(
            dimension_semantics=("arbitrary",),
            vmem_limit_bytes=VMEM_LIMIT),
        name="moe_combine_final",
    )(dest_tiles, dest_tiles, x_tt, y, g_final.reshape(1, D_MODEL))


def _moe(x_tt, ids, cnt, g, wr, br, w_gate, w_up, w_down, layer, combine_tile):
    n_tok = x_tt.shape[0] // TT
    tm, td, blk = _tiles(n_tok)
    n_blocks = n_tok // blk + N_CLASSES

    counts = cnt[:, 0]
    padded = (counts + blk - 1) // blk * blk
    pad_ends = jnp.cumsum(padded).astype(jnp.int32)
    pad_starts = pad_ends - padded
    block_start = jnp.arange(n_blocks, dtype=jnp.int32) * blk
    block_cls = jnp.minimum(
        jnp.sum(pad_ends[None, :N_CLASSES] <= block_start[:, None], axis=1),
        N_CLASSES - 1)
    block_ea = jnp.asarray(CLASS_LO)[block_cls]
    block_eb = jnp.asarray(CLASS_HI)[block_cls]
    n_used = (pad_ends[N_CLASSES - 1:N_CLASSES] // blk).astype(jnp.int32)

    dest = _dest(ids, pad_starts, DEST_TILE if n_tok % DEST_TILE == 0 else tm)
    per_tile = lambda t: dest[0].reshape(n_tok // t, 1, t)

    xs, wg_bf, wu_bf, wd_bf = _dispatch(x_tt, per_tile(td), pad_starts, pad_ends,
                                        w_gate, w_up, w_down, layer,
                                        n_blocks * blk, td, blk)
    y = _experts(xs, g, wr, br, wg_bf, wu_bf, wd_bf, block_ea, block_eb, n_used, blk)
    return y, per_tile(combine_tile)


def kernel(x, norm_mix_g, norm_ffn_g, norm_final_g, ab_w_in, a_ln_g, a_ws, a_ws_b, b_conv_w, ab_w_out, c_w_in, c_conv_w, c_conv_b, c_w_a, c_b_a, c_w_x, c_b_x, c_lambda, c_w_out, moe_w_rg, moe_b_rg, moe_w_re, moe_b_re, moe_w_gate, moe_w_up, moe_w_down):
    bsz, t_len, d = x.shape
    assert d == D_MODEL
    tm, _, _ = _tiles(bsz * t_len)
    assert t_len % tm == 0 and tm % CHUNK == 0

    routers = [_router_weights(moe_w_rg[l], moe_b_rg[l], moe_w_re[l], moe_b_re[l])
               for l in range(2)]
    route_args = [_route_args(norm_ffn_g[l], *routers[l]) for l in range(2)]

    def moe(x_tt, ids, cnt, layer, combine_tile):
        return _moe(x_tt, ids, cnt, norm_ffn_g[layer], *routers[layer],
                    moe_w_gate, moe_w_up, moe_w_down, layer, combine_tile)

    n_tok = bsz * t_len
    tf = FINAL_TILE if n_tok % FINAL_TILE == 0 else tm

    x1, ids, cnt = _mixer_ab(x, norm_mix_g[0], ab_w_in[0], a_ln_g[0], a_ws[0],
                             a_ws_b[0], b_conv_w[0], ab_w_out[0], route_args[0],
                             AB_TILE if t_len % AB_TILE == 0 else tm)
    x3, ids, cnt = _mixer_rglru(x1, moe(x1, ids, cnt, 0, tm), bsz, t_len, norm_mix_g[1],
                                c_w_in[0], c_conv_w[0], c_conv_b[0], c_w_a[0],
                                c_b_a[0], c_w_x[0], c_b_x[0], c_lambda[0],
                                c_w_out[0], route_args[1], tm)
    out = _final(x3, moe(x3, ids, cnt, 1, tf), norm_final_g, tf)
    return out.reshape(bsz, t_len, d)
```
